```python
import math
import jax, jax.numpy as jnp
from jax import lax
import numpy as np

D_MODEL = 2048
BATCH = 8
SEQ = 2048
DEPTH = 1

MIX_WIDTH = D_MODEL
POOL_WIDTH = D_MODEL // 2
POOL_WINDOWS = (2, 4, 8, 16)
POOL_GROUP = POOL_WIDTH // len(POOL_WINDOWS)
RET_WIDTH = MIX_WIDTH - POOL_WIDTH
RET_HEADS = 4
RET_HEAD_DIM = RET_WIDTH // RET_HEADS
RET_CHUNK = 128
ROPE_BASE = 10000.0
D_FF = -(-8 * D_MODEL // (3 * 256)) * 256
IN_WIDTH = POOL_WIDTH + 4 * RET_WIDTH
N_MOD = 6
EPS = 1e-6

kernel_name = 'hybrid_pool_retention_encoder_block'


def rmsnorm(x, g):
    xf = x.astype(jnp.float32)
    y = xf * lax.rsqrt(jnp.mean(xf * xf, axis=-1, keepdims=True) + EPS)
    return (y * g.astype(jnp.float32)).astype(x.dtype)


def rotary(t):
    S, dh = t.shape[2], t.shape[3]
    half = dh // 2
    inv = 1.0 / (ROPE_BASE ** jnp.linspace(0.0, 1.0, half, dtype=jnp.float32))
    ang = jnp.arange(S, dtype=jnp.float32)[:, None] * inv[None, :]
    cos, sin = jnp.cos(ang), jnp.sin(ang)
    t1, t2 = t[..., :half], t[..., half:]
    return jnp.concatenate([t1 * cos - t2 * sin, t1 * sin + t2 * cos], axis=-1)


def multi_scale_pool(u):
    S = u.shape[1]
    t = jnp.arange(S)
    outs = []
    for gi, w in enumerate(POOL_WINDOWS):
        ug = u[..., gi * POOL_GROUP:(gi + 1) * POOL_GROUP].astype(jnp.float32)
        cs = jnp.concatenate([jnp.zeros_like(ug[:, :1]), jnp.cumsum(ug, axis=1)], axis=1)
        lo = jnp.clip(t - w // 2, 0, S)
        hi = jnp.clip(t + w // 2, 0, S)
        total = cs[:, hi] - cs[:, lo]
        count = (hi - lo).astype(jnp.float32)[None, :, None]
        outs.append(total / count - ug)
    return jnp.concatenate(outs, axis=-1)


def retention_one_direction(q, k, v, log_gamma, include_diag):
    B, H, S, dh = q.shape
    C = RET_CHUNK
    n = S // C
    qc = q.reshape(B, H, n, C, dh)
    kc = k.reshape(B, H, n, C, dh)
    vc = v.reshape(B, H, n, C, dh)
    i = jnp.arange(C)
    diff = (i[:, None] - i[None, :]).astype(jnp.float32)
    mask = diff >= 0 if include_diag else diff > 0
    lg = log_gamma[:, None, None]
    dmat = jnp.where(mask, jnp.exp(jnp.where(mask, diff, 0.0) * lg), 0.0)
    scores = jnp.einsum('bhnid,bhnjd->bhnij', qc, kc) * dmat[None, :, None]
    inner = jnp.einsum('bhnij,bhnjd->bhnid', scores, vc)
    k_decay = jnp.exp((C - 1 - i).astype(jnp.float32)[None, :] * log_gamma[:, None])
    q_decay = jnp.exp((i + 1).astype(jnp.float32)[None, :] * log_gamma[:, None])
    chunk_decay = jnp.exp(C * log_gamma)

    def step(state, qkv):
        q_, k_, v_ = qkv
        cross = jnp.einsum('bhid,bhde->bhie', q_, state) * q_decay[None, :, :, None]
        state = state * chunk_decay[None, :, None, None] + jnp.einsum(
            'bhjd,bhje->bhde', k_ * k_decay[None, :, :, None], v_)
        return state, cross

    xs = (jnp.moveaxis(qc, 2, 0), jnp.moveaxis(kc, 2, 0), jnp.moveaxis(vc, 2, 0))
    state0 = jnp.zeros((B, H, dh, dh), jnp.float32)
    _, cross = lax.scan(step, state0, xs)
    cross = jnp.moveaxis(cross, 0, 2)
    return (inner + cross).reshape(B, H, S, dh)


def setup_inputs(seed: int = 0) -> dict:
    key = jax.random.key(seed)
    ks = jax.random.split(key, 20)
    f32 = jnp.float32
    L = DEPTH

    def nrm(k, shape, fan_in):
        return jax.random.normal(k, shape, f32) * (fan_in ** -0.5)

    base = np.log(-np.log(1.0 - 2.0 ** (-5.0 - np.arange(RET_HEADS)))).astype(np.float32)
    return {
        'x': jax.random.normal(ks[0], (BATCH, SEQ, D_MODEL), f32),
        'c': jax.random.normal(ks[1], (BATCH, D_MODEL), f32),
        'w_ada': nrm(ks[2], (L, D_MODEL, N_MOD * D_MODEL), D_MODEL),
        'b_ada': 0.02 * jax.random.normal(ks[3], (L, N_MOD * D_MODEL), f32),
        'norm1_g': 1.0 + 0.02 * jax.random.normal(ks[4], (L, D_MODEL), f32),
        'w_in': nrm(ks[5], (L, D_MODEL, IN_WIDTH), D_MODEL),
        'pool_w': nrm(ks[6], (L, len(POOL_WINDOWS), POOL_GROUP, POOL_GROUP), POOL_GROUP),
        'pool_scale': 1.0 + 0.02 * jax.random.normal(ks[7], (L, POOL_WIDTH), f32),
        'ret_decay_fwd': jnp.asarray(base)[None, :] + 0.05 * jax.random.normal(ks[8], (L, RET_HEADS), f32),
        'ret_decay_bwd': jnp.asarray(base)[None, :] + 0.05 * jax.random.normal(ks[9], (L, RET_HEADS), f32),
        'w_out': nrm(ks[10], (L, MIX_WIDTH, D_MODEL), MIX_WIDTH),
        'norm2_g': 1.0 + 0.02 * jax.random.normal(ks[11], (L, D_MODEL), f32),
        'w_gate': nrm(ks[12], (L, D_MODEL, D_FF), D_MODEL),
        'w_up': nrm(ks[13], (L, D_MODEL, D_FF), D_MODEL),
        'w_down': nrm(ks[14], (L, D_FF, D_MODEL), D_FF),
        'final_g': 1.0 + 0.02 * jax.random.normal(ks[15], (D_MODEL,), f32),
    }


def reference(x, c, w_ada, b_ada, norm1_g, w_in, pool_w, pool_scale, ret_decay_fwd,
              ret_decay_bwd, w_out, norm2_g, w_gate, w_up, w_down, final_g):
    B, S, D = x.shape
    H, dh = RET_HEADS, RET_HEAD_DIM
    c_act = jax.nn.silu(c)

    def to_heads(t):
        return t.reshape(B, S, H, dh).transpose(0, 2, 1, 3).astype(jnp.float32)

    for l in range(DEPTH):
        mod = (c_act @ w_ada[l] + b_ada[l])[:, None, :]
        sh1, sc1, g1, sh2, sc2, g2 = jnp.split(mod, N_MOD, axis=-1)

        h = rmsnorm(x, norm1_g[l]) * (1.0 + sc1) + sh1
        proj = h @ w_in[l]
        o = POOL_WIDTH
        u_pool = proj[..., :o]
        q = proj[..., o:o + RET_WIDTH]
        k = proj[..., o + RET_WIDTH:o + 2 * RET_WIDTH]
        v = proj[..., o + 2 * RET_WIDTH:o + 3 * RET_WIDTH]
        g = proj[..., o + 3 * RET_WIDTH:]

        pooled = multi_scale_pool(u_pool).reshape(B, S, len(POOL_WINDOWS), POOL_GROUP)
        pool_out = jnp.einsum('bsgc,gcd->bsgd', pooled, pool_w[l].astype(jnp.float32))
        pool_out = (pool_out.reshape(B, S, POOL_WIDTH) * pool_scale[l]).astype(x.dtype)

        qh = rotary(to_heads(q)) * (dh ** -0.5)
        kh = rotary(to_heads(k))
        vh = to_heads(v)
        lg_f = -jnp.exp(ret_decay_fwd[l].astype(jnp.float32))
        lg_b = -jnp.exp(ret_decay_bwd[l].astype(jnp.float32))
        ret_f = retention_one_direction(qh, kh, vh, lg_f, True)
        ret_b = jnp.flip(retention_one_direction(
            jnp.flip(qh, 2), jnp.flip(kh, 2), jnp.flip(vh, 2), lg_b, False), 2)
        ret = ret_f + ret_b
        ret = ret * lax.rsqrt(jnp.mean(ret * ret, axis=-1, keepdims=True) + EPS)
        ret = ret.transpose(0, 2, 1, 3).reshape(B, S, RET_WIDTH).astype(x.dtype)
        ret_out = jax.nn.silu(g) * ret

        mixed = jnp.concatenate([pool_out, ret_out], axis=-1) @ w_out[l]
        x = x + g1 * mixed

        h2 = rmsnorm(x, norm2_g[l]) * (1.0 + sc2) + sh2
        ffn = (jax.nn.silu(h2 @ w_gate[l]) * (h2 @ w_up[l])) @ w_down[l]
        x = x + g2 * ffn

    return rmsnorm(x, final_g)
```

```python
import functools

import jax
import jax.numpy as jnp
from jax import lax
from jax.experimental import pallas as pl
from jax.experimental.pallas import tpu as pltpu

_EPS = 1e-6
_POOL_WINDOWS = (2, 4, 8, 16)
_RET_HEADS = 4
_ROPE_BASE = 10000.0
_N_MOD = 6

_V7X_VMEM_BYTES = 64 * 1024 * 1024
_BF16_ROWS = 16
_RET_CHUNK = 256
_POOL_PAD = 16
_ROW_CHUNK = 32

_F32 = jnp.float32
_BF16 = jnp.bfloat16


def _vmem_limit(nbytes):
    return int(min(_V7X_VMEM_BYTES - (2 << 20), nbytes + (12 << 20)))


def _nbytes(shape, dtype):
    n = 1
    for s in shape:
        n *= s
    return n * jnp.dtype(dtype).itemsize


def _dot(a, b):
    return jnp.dot(a, b, preferred_element_type=_F32)


def _silu(x):
    return x * jax.nn.sigmoid(x)


def _rms_rows(x):
    return x * lax.rsqrt(jnp.mean(x * x, axis=-1, keepdims=True) + _EPS)


def _adaln_kernel(c_ref, w_ref, b_ref, o_ref):
    c = c_ref[...]
    o_ref[...] = _dot(_silu(c).astype(_BF16), w_ref[...].astype(_BF16)) + b_ref[...]


def _adaln(c, w, b, *, tn=1024):
    bsz, d = c.shape
    n = w.shape[1]
    vmem = 2 * _nbytes((d, tn), _F32) + _nbytes((d, tn), _BF16) + 4 * _nbytes((bsz, d), _F32)
    return pl.pallas_call(
        _adaln_kernel,
        grid=(n // tn,),
        in_specs=[
            pl.BlockSpec((bsz, d), lambda j: (0, 0)),
            pl.BlockSpec((d, tn), lambda j: (0, j)),
            pl.BlockSpec((1, tn), lambda j: (0, j)),
        ],
        out_specs=pl.BlockSpec((bsz, tn), lambda j: (0, j)),
        out_shape=jax.ShapeDtypeStruct((bsz, n), _F32),
        compiler_params=pltpu.CompilerParams(
            dimension_semantics=("arbitrary",), vmem_limit_bytes=_vmem_limit(vmem)),
        name="adaln",
    )(c, w, b.reshape(1, n))


def _norm_modulate(src_ref, gain_ref, mod_ref, shift_row, scale_row, dst_ref):
    rows = src_ref.shape[0]

    def body(r, carry):
        sl = pl.ds(pl.multiple_of(r * _ROW_CHUNK, _ROW_CHUNK), _ROW_CHUNK)
        y = _rms_rows(src_ref[sl, :]) * gain_ref[...]
        h = y * (1.0 + mod_ref[0, scale_row:scale_row + 1, :]) + mod_ref[0, shift_row:shift_row + 1, :]
        dst_ref[sl, :] = h.astype(dst_ref.dtype)
        return carry

    lax.fori_loop(0, rows // _ROW_CHUNK, body, 0)


def _inproj_kernel(x_ref, mod_ref, gain_ref, w_ref, o_ref, h_ref):
    @pl.when(pl.program_id(1) == 0)
    def _():
        _norm_modulate(x_ref, gain_ref, mod_ref, 0, 1, h_ref)

    o_ref[...] = _dot(h_ref[...], w_ref[...]).astype(o_ref.dtype)


def _inproj(x2d, mod3, gain, w, seq, *, tm=1024, tn=1024):
    t, d = x2d.shape
    n = w.shape[1]
    per_batch = seq // tm
    vmem = (2 * _nbytes((tm, d), _F32) + 2 * _nbytes((d, tn), _BF16) + 2 * _nbytes((tm, tn), _BF16)
            + _nbytes((tm, d), _BF16) + _nbytes((tm, tn), _F32))
    return pl.pallas_call(
        _inproj_kernel,
        grid=(t // tm, n // tn),
        in_specs=[
            pl.BlockSpec((tm, d), lambda i, j: (i, 0)),
            pl.BlockSpec((1, _N_MOD, d), lambda i, j: (i // per_batch, 0, 0)),
            pl.BlockSpec((1, d), lambda i, j: (0, 0)),
            pl.BlockSpec((d, tn), lambda i, j: (0, j)),
        ],
        out_specs=pl.BlockSpec((tm, tn), lambda i, j: (i, j)),
        out_shape=jax.ShapeDtypeStruct((t, n), _BF16),
        scratch_shapes=[pltpu.VMEM((tm, d), _BF16)],
        compiler_params=pltpu.CompilerParams(
            dimension_semantics=("arbitrary", "arbitrary"), vmem_limit_bytes=_vmem_limit(vmem)),
        name="inproj",
    )(x2d, mod3, gain, w)


def _pool_group(pad_ref, pw_ref, ps_ref, out_ref, window, seq):
    half = window // 2
    c = _RET_CHUNK
    for ci in range(seq // c):
        base = _POOL_PAD + ci * c
        total = pad_ref[base - half:base - half + c, :]
        for j in range(1, window):
            total = total + pad_ref[base - half + j:base - half + j + c, :]
        if ci == 0 or ci == seq // c - 1:
            t = ci * c + lax.broadcasted_iota(jnp.int32, total.shape, 0)
            count = jnp.minimum(t + half, seq) - jnp.maximum(t - half, 0)
            mean = total / count.astype(_F32)
        else:
            mean = total * (1.0 / window)
        pooled = mean - pad_ref[base:base + c, :]
        out = _dot(pooled.astype(_BF16), pw_ref[0]) * ps_ref[0]
        out_ref[ci * c:(ci + 1) * c, :] = out.astype(out_ref.dtype)


def _mixer_kernel(u_ref, q_ref, k_ref, v_ref, gate_ref, cos_ref, sin_ref, pw_ref, ps_ref,
                  df_ref, db_ref, pool_ref, ret_ref,
                  pad_ref, qr_ref, kr_ref, acc_ref, dec_ref, sf_ref, sb_ref):
    seq, dh = q_ref.shape
    c = _RET_CHUNK
    n = seq // c
    half = dh // 2

    zeros = jnp.zeros((_POOL_PAD, pad_ref.shape[1]), _F32)
    pad_ref[0:_POOL_PAD, :] = zeros
    pad_ref[_POOL_PAD + seq:, :] = zeros
    pad_ref[_POOL_PAD:_POOL_PAD + seq, :] = u_ref[...].astype(_F32)
    group = pl.program_id(1)
    for gi, window in enumerate(_POOL_WINDOWS):
        @pl.when(group == gi)
        def _(window=window):
            _pool_group(pad_ref, pw_ref, ps_ref, pool_ref, window, seq)

    lg_f = -jnp.exp(df_ref[0])
    lg_b = -jnp.exp(db_ref[0])
    ri = lax.broadcasted_iota(jnp.int32, (c, c), 0).astype(_F32)
    cj = lax.broadcasted_iota(jnp.int32, (c, c), 1).astype(_F32)
    diff = ri - cj
    dec_ref[0] = jnp.exp(jnp.where(diff >= 0, diff * lg_f, -diff * lg_b))
    dec_ref[1] = jnp.exp((ri + 1.0) * lg_f)
    dec_ref[2] = jnp.exp((c - 1.0 - ri) * lg_f)
    dec_ref[3] = jnp.exp((c - ri) * lg_b)
    dec_ref[4] = jnp.exp(ri * lg_b)
    chunk_f = jnp.exp(c * lg_f)
    chunk_b = jnp.exp(c * lg_b)

    def tn_dot(a, b):
        return lax.dot_general(a, b, (((0,), (0,)), ((), ())), preferred_element_type=_F32)

    def nt_dot(a, b):
        return lax.dot_general(a, b, (((1,), (1,)), ((), ())), preferred_element_type=_F32)

    def rotary(t, cos, sin):
        t1, t2 = t[:, :half], t[:, half:]
        return jnp.concatenate([t1 * cos - t2 * sin, t1 * sin + t2 * cos], axis=-1)

    for ci in range(n):
        rows = slice(ci * c, (ci + 1) * c)
        cos, sin = cos_ref[rows, :], sin_ref[rows, :]
        q = (rotary(q_ref[rows, :].astype(_F32), cos, sin) * (dh ** -0.5)).astype(_BF16)
        k32 = rotary(k_ref[rows, :].astype(_F32), cos, sin)
        qr_ref[rows, :] = q
        kr_ref[rows, :] = k32
        v = v_ref[rows, :]
        scores = nt_dot(q, k32.astype(_BF16)) * dec_ref[0]
        out = _dot(scores.astype(_BF16), v)
        if ci > 0:
            out = out + _dot(q, sf_ref[...].astype(_BF16)) * dec_ref[1]
        acc_ref[rows, :] = out
        if ci < n - 1:
            upd = tn_dot((k32 * dec_ref[2]).astype(_BF16), v)
            sf_ref[...] = upd if ci == 0 else sf_ref[...] * chunk_f + upd

    for ci in reversed(range(n)):
        rows = slice(ci * c, (ci + 1) * c)
        out = acc_ref[rows, :]
        if ci < n - 1:
            out = out + _dot(qr_ref[rows, :], sb_ref[...].astype(_BF16)) * dec_ref[3]
        ret = _rms_rows(out)
        ret_ref[rows, :] = (_silu(gate_ref[rows, :].astype(_F32)) * ret).astype(ret_ref.dtype)
        if ci > 0:
            upd = tn_dot((kr_ref[rows, :] * dec_ref[4]).astype(_BF16), v_ref[rows, :])
            sb_ref[...] = upd if ci == n - 1 else sb_ref[...] * chunk_b + upd


def _mixer(proj, cos, sin, pool_w, pool_scale, dec_f, dec_b, bsz, seq):
    t = proj.shape[0]
    heads = _RET_HEADS
    dh = pool_w.shape[-1]
    pool_blocks = len(_POOL_WINDOWS)
    col = lambda first: (lambda b, h: (b, first + h))
    seq_block = lambda first: pl.BlockSpec((seq, dh), col(first))
    per_head = lambda shape: pl.BlockSpec((1,) + shape, lambda b, h: (h, 0, 0))
    const = lambda shape: pl.BlockSpec(shape, lambda b, h: (0, 0))
    scratch = [
        pltpu.VMEM((seq + 2 * _POOL_PAD, dh), _F32),
        pltpu.VMEM((seq, dh), _BF16),
        pltpu.VMEM((seq, dh), _F32),
        pltpu.VMEM((seq, dh), _F32),
        pltpu.VMEM((5, _RET_CHUNK, _RET_CHUNK), _F32),
        pltpu.VMEM((dh, dh), _F32),
        pltpu.VMEM((dh, dh), _F32),
    ]
    vmem = (2 * 7 * _nbytes((seq, dh), _BF16) + 2 * 2 * _nbytes((seq, dh // 2), _F32)
            + 4 * _nbytes((seq, dh), _F32) + 8 * _nbytes((dh, dh), _F32))
    return pl.pallas_call(
        _mixer_kernel,
        grid=(bsz, heads),
        in_specs=[
            seq_block(0),
            seq_block(pool_blocks),
            seq_block(pool_blocks + heads),
            seq_block(pool_blocks + 2 * heads),
            seq_block(pool_blocks + 3 * heads),
            const((seq, dh // 2)), const((seq, dh // 2)),
            per_head((dh, dh)),
            per_head((1, dh)),
            per_head((1, _RET_CHUNK)), per_head((1, _RET_CHUNK)),
        ],
        out_specs=[pl.BlockSpec((seq, dh), lambda b, h: (b, h)),
                   pl.BlockSpec((seq, dh), lambda b, h: (b, h))],
        out_shape=[jax.ShapeDtypeStruct((t, pool_blocks * dh), _BF16),
                   jax.ShapeDtypeStruct((t, heads * dh), _BF16)],
        scratch_shapes=scratch,
        compiler_params=pltpu.CompilerParams(
            dimension_semantics=("arbitrary", "arbitrary"), vmem_limit_bytes=_vmem_limit(vmem)),
        name="mixer",
    )(proj, proj, proj, proj, proj, cos, sin, pool_w, pool_scale, dec_f, dec_b)


def _outproj_kernel(p_ref, r_ref, x_ref, mod_ref, gain_ref, w_ref, x1_ref, h2_ref):
    kp = p_ref.shape[1]
    mixed = _dot(p_ref[...], w_ref[0:kp, :]) + _dot(r_ref[...], w_ref[kp:, :])
    x1_ref[...] = x_ref[...] + mod_ref[0, 2:3, :] * mixed
    _norm_modulate(x1_ref, gain_ref, mod_ref, 3, 4, h2_ref)


def _outproj(pool_out, ret_out, x2d, mod3, gain, w, seq, *, tm=512):
    t, d = x2d.shape
    kp, kr = pool_out.shape[1], ret_out.shape[1]
    per_batch = seq // tm
    vmem = (2 * _nbytes((tm, kp + kr), _BF16) + 4 * _nbytes((tm, d), _F32) + 2 * _nbytes((tm, d), _BF16)
            + 2 * _nbytes((kp + kr, d), _BF16) + _nbytes((tm, d), _F32))
    return pl.pallas_call(
        _outproj_kernel,
        grid=(t // tm,),
        in_specs=[
            pl.BlockSpec((tm, kp), lambda i: (i, 0)),
            pl.BlockSpec((tm, kr), lambda i: (i, 0)),
            pl.BlockSpec((tm, d), lambda i: (i, 0)),
            pl.BlockSpec((1, _N_MOD, d), lambda i: (i // per_batch, 0, 0)),
            pl.BlockSpec((1, d), lambda i: (0, 0)),
            pl.BlockSpec((kp + kr, d), lambda i: (0, 0)),
        ],
        out_specs=[pl.BlockSpec((tm, d), lambda i: (i, 0)),
                   pl.BlockSpec((tm, d), lambda i: (i, 0))],
        out_shape=[jax.ShapeDtypeStruct((t, d), _F32),
                   jax.ShapeDtypeStruct((t, d), _BF16)],
        compiler_params=pltpu.CompilerParams(
            dimension_semantics=("arbitrary",), vmem_limit_bytes=_vmem_limit(vmem)),
        name="outproj",
    )(pool_out, ret_out, x2d, mod3, gain, w)


def _ffn_kernel(h2_ref, wg_ref, wu_ref, wd_ref, x1_ref, mod_ref, gain_ref, o_ref, acc_ref, *, final_norm):
    f = pl.program_id(1)

    @pl.when(f == 0)
    def _():
        acc_ref[...] = jnp.zeros_like(acc_ref)

    h2 = h2_ref[...]
    hidden = (_silu(_dot(h2, wg_ref[...])) * _dot(h2, wu_ref[...])).astype(_BF16)
    acc_ref[...] += _dot(hidden, wd_ref[...])

    @pl.when(f == pl.num_programs(1) - 1)
    def _():
        def body(r, carry):
            sl = pl.ds(pl.multiple_of(r * _ROW_CHUNK, _ROW_CHUNK), _ROW_CHUNK)
            x2 = x1_ref[sl, :] + mod_ref[0, 5:6, :] * acc_ref[sl, :]
            o_ref[sl, :] = _rms_rows(x2) * gain_ref[...] if final_norm else x2
            return carry

        lax.fori_loop(0, o_ref.shape[0] // _ROW_CHUNK, body, 0)


def _ffn(h2, x1, mod3, wg, wu, wd, final_gain, seq, *, final_norm, tm=512, tf=512):
    t, d = x1.shape
    dff = wg.shape[1]
    per_batch = seq // tm
    vmem = (2 * _nbytes((tm, d), _BF16) + 6 * _nbytes((d, tf), _BF16) + 5 * _nbytes((tm, d), _F32)
            + 4 * _nbytes((tm, tf), _F32))
    return pl.pallas_call(
        functools.partial(_ffn_kernel, final_norm=final_norm),
        grid=(t // tm, dff // tf),
        in_specs=[
            pl.BlockSpec((tm, d), lambda i, f: (i, 0)),
            pl.BlockSpec((d, tf), lambda i, f: (0, f)),
            pl.BlockSpec((d, tf), lambda i, f: (0, f)),
            pl.BlockSpec((tf, d), lambda i, f: (f, 0)),
            pl.BlockSpec((tm, d), lambda i, f: (i, 0)),
            pl.BlockSpec((1, _N_MOD, d), lambda i, f: (i // per_batch, 0, 0)),
            pl.BlockSpec((1, d), lambda i, f: (0, 0)),
        ],
        out_specs=pl.BlockSpec((tm, d), lambda i, f: (i, 0)),
        out_shape=jax.ShapeDtypeStruct((t, d), _F32),
        scratch_shapes=[pltpu.VMEM((tm, d), _F32)],
        compiler_params=pltpu.CompilerParams(
            dimension_semantics=("arbitrary", "arbitrary"), vmem_limit_bytes=_vmem_limit(vmem)),
        name="ffn",
    )(h2, wg, wu, wd, x1, mod3, final_gain)


def kernel(x, c, w_ada, b_ada, norm1_g, w_in, pool_w, pool_scale, ret_decay_fwd, ret_decay_bwd,
           w_out, norm2_g, w_gate, w_up, w_down, final_g):
    bsz, seq, d = x.shape
    depth = w_ada.shape[0]
    heads = _RET_HEADS
    pool_width = pool_scale.shape[1]
    dh = (w_in.shape[2] - pool_width) // (4 * heads)
    assert pool_width == len(_POOL_WINDOWS) * dh and seq % _RET_CHUNK == 0 and dh == _RET_CHUNK

    inv = 1.0 / (_ROPE_BASE ** jnp.linspace(0.0, 1.0, dh // 2, dtype=_F32))
    ang = jnp.arange(seq, dtype=_F32)[:, None] * inv[None, :]
    cos, sin = jnp.cos(ang), jnp.sin(ang)

    xf = x.reshape(bsz * seq, d)
    for l in range(depth):
        mod3 = _adaln(c, w_ada[l], b_ada[l]).reshape(bsz, _N_MOD, d)
        proj = _inproj(xf, mod3, norm1_g[l].reshape(1, d), w_in[l].astype(_BF16), seq)
        spread = lambda p: jnp.broadcast_to(p.astype(_F32)[:, None, None], (heads, 1, _RET_CHUNK))
        pool_out, ret_out = _mixer(
            proj, cos, sin, pool_w[l].astype(_BF16), pool_scale[l].reshape(len(_POOL_WINDOWS), 1, dh),
            spread(ret_decay_fwd[l]), spread(ret_decay_bwd[l]), bsz, seq)
        x1, h2 = _outproj(pool_out, ret_out, xf, mod3, norm2_g[l].reshape(1, d), w_out[l].astype(_BF16), seq)
        xf = _ffn(h2, x1, mod3, w_gate[l].astype(_BF16), w_up[l].astype(_BF16), w_down[l].astype(_BF16),
                  final_g.reshape(1, d), seq, final_norm=(l == depth - 1))
    return xf.reshape(bsz, seq, d)
```

```python
import functools

import jax
import jax.numpy as jnp
from jax import lax
from jax.experimental import pallas as pl
from jax.experimental.pallas import tpu as pltpu

_EPS = 1e-6
_POOL_WINDOWS = (2, 4, 8, 16)
_RET_HEADS = 4
_ROPE_BASE = 10000.0
_N_MOD = 6

_V7X_VMEM_BYTES = 64 * 1024 * 1024
_LANES = 128
_RET_CHUNK = 256
_POOL_PAD = 16
_NORM_ROWS = 128
_DOT_ROWS = 256

_F32 = jnp.float32
_BF16 = jnp.bfloat16


def _vmem_limit(nbytes):
    return int(min(_V7X_VMEM_BYTES - (2 << 20), nbytes + (8 << 20)))


def _nbytes(shape, dtype):
    n = 1
    for s in shape:
        n *= s
    return n * jnp.dtype(dtype).itemsize


def _dot(a, b):
    return jnp.dot(a, b, preferred_element_type=_F32)


def _silu(x):
    return x * jax.nn.sigmoid(x)


def _rms_rows(x):
    return x * lax.rsqrt(jnp.mean(x * x, axis=-1, keepdims=True) + _EPS)


def _col_blocks(d):
    return [slice(c0, c0 + _LANES) for c0 in range(0, d, _LANES)]


def _rmsnorm_rows(src_ref, dst_ref, row0, nrows, gain, shift=None):
    d = src_ref.shape[1]
    for r0 in range(row0, row0 + nrows, _NORM_ROWS):
        rows = slice(r0, r0 + _NORM_ROWS)
        ss = None
        for cols in _col_blocks(d):
            xc = src_ref[rows, cols]
            ss = xc * xc if ss is None else ss + xc * xc
        rs = lax.rsqrt(jnp.sum(ss, axis=-1, keepdims=True) / d + _EPS)
        for cols in _col_blocks(d):
            y = src_ref[rows, cols] * rs * gain[:, cols]
            if shift is not None:
                y = y + shift[:, cols]
            dst_ref[rows, cols] = y.astype(dst_ref.dtype)


def _adaln_kernel(c_ref, w_ref, b_ref, o_ref):
    c = c_ref[...]
    o_ref[...] = _dot(_silu(c).astype(_BF16), w_ref[...].astype(_BF16)) + b_ref[...]


def _adaln(c, w, b, *, tn=1024):
    bsz, d = c.shape
    n = w.shape[1]
    vmem = 2 * _nbytes((d, tn), _F32) + _nbytes((d, tn), _BF16) + 4 * _nbytes((bsz, d), _F32)
    return pl.pallas_call(
        _adaln_kernel,
        grid=(n // tn,),
        in_specs=[
            pl.BlockSpec((bsz, d), lambda j: (0, 0)),
            pl.BlockSpec((d, tn), lambda j: (0, j)),
            pl.BlockSpec((1, tn), lambda j: (0, j)),
        ],
        out_specs=pl.BlockSpec((bsz, tn), lambda j: (0, j)),
        out_shape=jax.ShapeDtypeStruct((bsz, n), _F32),
        compiler_params=pltpu.CompilerParams(
            dimension_semantics=("arbitrary",), vmem_limit_bytes=_vmem_limit(vmem)),
        name="adaln",
    )(c, w, b.reshape(1, n))


def _inproj_kernel(x_ref, mod_ref, gain_ref, w_ref, o_ref, h_ref):
    scale = gain_ref[...] * (1.0 + mod_ref[0, 1:2, :])
    shift = mod_ref[0, 0:1, :]
    for r0 in range(0, x_ref.shape[0], _DOT_ROWS):
        rows = slice(r0, r0 + _DOT_ROWS)
        _rmsnorm_rows(x_ref, h_ref, r0, _DOT_ROWS, scale, shift)
        o_ref[rows, :] = _dot(h_ref[rows, :], w_ref[...]).astype(o_ref.dtype)


def _inproj(x2d, mod3, gain, w, seq, *, tm=512):
    t, d = x2d.shape
    n = w.shape[1]
    per_batch = seq // tm
    vmem = (2 * _nbytes((tm, d), _F32) + _nbytes((d, n), _BF16) + 2 * _nbytes((tm, n), _BF16)
            + _nbytes((tm, d), _BF16) + _nbytes((_DOT_ROWS, n), _F32))
    return pl.pallas_call(
        _inproj_kernel,
        grid=(t // tm,),
        in_specs=[
            pl.BlockSpec((tm, d), lambda i: (i, 0)),
            pl.BlockSpec((1, _N_MOD, d), lambda i: (i // per_batch, 0, 0)),
            pl.BlockSpec((1, d), lambda i: (0, 0)),
            pl.BlockSpec((d, n), lambda i: (0, 0), pipeline_mode=pl.Buffered(1)),
        ],
        out_specs=pl.BlockSpec((tm, n), lambda i: (i, 0)),
        out_shape=jax.ShapeDtypeStruct((t, n), _BF16),
        scratch_shapes=[pltpu.VMEM((tm, d), _BF16)],
        compiler_params=pltpu.CompilerParams(
            dimension_semantics=("arbitrary",), vmem_limit_bytes=_vmem_limit(vmem)),
        name="inproj",
    )(x2d, mod3, gain, w)


def _pool_group(pad_ref, pw_ref, ps_ref, out_ref, window, seq):
    half = window // 2
    c = _RET_CHUNK
    for ci in range(seq // c):
        base = _POOL_PAD + ci * c
        total = pad_ref[base - half:base - half + c, :]
        for j in range(1, window):
            total = total + pad_ref[base - half + j:base - half + j + c, :]
        if ci == 0 or ci == seq // c - 1:
            t = ci * c + lax.broadcasted_iota(jnp.int32, total.shape, 0)
            count = jnp.minimum(t + half, seq) - jnp.maximum(t - half, 0)
            mean = total / count.astype(_F32)
        else:
            mean = total * (1.0 / window)
        pooled = mean - pad_ref[base:base + c, :]
        out = _dot(pooled.astype(_BF16), pw_ref[0]) * ps_ref[0]
        out_ref[ci * c:(ci + 1) * c, :] = out.astype(out_ref.dtype)


def _mixer_kernel(u_ref, q_ref, k_ref, v_ref, gate_ref, cos_ref, sin_ref, pw_ref, ps_ref,
                  df_ref, db_ref, pool_ref, ret_ref,
                  pad_ref, qr_ref, kr_ref, acc_ref, dec_ref, sf_ref, sb_ref):
    seq, dh = q_ref.shape
    c = _RET_CHUNK
    n = seq // c
    half = dh // 2

    zeros = jnp.zeros((_POOL_PAD, pad_ref.shape[1]), _F32)
    pad_ref[0:_POOL_PAD, :] = zeros
    pad_ref[_POOL_PAD + seq:, :] = zeros
    pad_ref[_POOL_PAD:_POOL_PAD + seq, :] = u_ref[...].astype(_F32)
    group = pl.program_id(1)
    for gi, window in enumerate(_POOL_WINDOWS):
        @pl.when(group == gi)
        def _(window=window):
            _pool_group(pad_ref, pw_ref, ps_ref, pool_ref, window, seq)

    lg_f = -jnp.exp(df_ref[0])
    lg_b = -jnp.exp(db_ref[0])
    ri = lax.broadcasted_iota(jnp.int32, (c, c), 0).astype(_F32)
    cj = lax.broadcasted_iota(jnp.int32, (c, c), 1).astype(_F32)
    diff = ri - cj
    dec_ref[0] = jnp.exp(jnp.where(diff >= 0, diff * lg_f, -diff * lg_b))
    dec_ref[1] = jnp.exp((ri + 1.0) * lg_f)
    dec_ref[2] = jnp.exp((c - 1.0 - ri) * lg_f)
    dec_ref[3] = jnp.exp((c - ri) * lg_b)
    dec_ref[4] = jnp.exp(ri * lg_b)
    chunk_f = jnp.exp(c * lg_f)
    chunk_b = jnp.exp(c * lg_b)

    def tn_dot(a, b):
        return lax.dot_general(a, b, (((0,), (0,)), ((), ())), preferred_element_type=_F32)

    def nt_dot(a, b):
        return lax.dot_general(a, b, (((1,), (1,)), ((), ())), preferred_element_type=_F32)

    def rotary(t, cos, sin):
        t1, t2 = t[:, :half], t[:, half:]
        return jnp.concatenate([t1 * cos - t2 * sin, t1 * sin + t2 * cos], axis=-1)

    for ci in range(n):
        rows = slice(ci * c, (ci + 1) * c)
        cos, sin = cos_ref[rows, :], sin_ref[rows, :]
        q = (rotary(q_ref[rows, :].astype(_F32), cos, sin) * (dh ** -0.5)).astype(_BF16)
        k32 = rotary(k_ref[rows, :].astype(_F32), cos, sin)
        qr_ref[rows, :] = q
        kr_ref[rows, :] = k32
        v = v_ref[rows, :]
        scores = nt_dot(q, k32.astype(_BF16)) * dec_ref[0]
        out = _dot(scores.astype(_BF16), v)
        if ci > 0:
            out = out + _dot(q, sf_ref[...].astype(_BF16)) * dec_ref[1]
        acc_ref[rows, :] = out
        if ci < n - 1:
            upd = tn_dot((k32 * dec_ref[2]).astype(_BF16), v)
            sf_ref[...] = upd if ci == 0 else sf_ref[...] * chunk_f + upd

    for ci in reversed(range(n)):
        rows = slice(ci * c, (ci + 1) * c)
        out = acc_ref[rows, :]
        if ci < n - 1:
            out = out + _dot(qr_ref[rows, :], sb_ref[...].astype(_BF16)) * dec_ref[3]
        ret = _rms_rows(out)
        ret_ref[rows, :] = (_silu(gate_ref[rows, :].astype(_F32)) * ret).astype(ret_ref.dtype)
        if ci > 0:
            upd = tn_dot((kr_ref[rows, :] * dec_ref[4]).astype(_BF16), v_ref[rows, :])
            sb_ref[...] = upd if ci == n - 1 else sb_ref[...] * chunk_b + upd


def _mixer(proj, cos, sin, pool_w, pool_scale, dec_f, dec_b, bsz, seq):
    t = proj.shape[0]
    heads = _RET_HEADS
    dh = pool_w.shape[-1]
    pool_blocks = len(_POOL_WINDOWS)
    col = lambda first: (lambda b, h: (b, first + h))
    seq_block = lambda first: pl.BlockSpec((seq, dh), col(first))
    per_head = lambda shape: pl.BlockSpec((1,) + shape, lambda b, h: (h, 0, 0))
    const = lambda shape: pl.BlockSpec(shape, lambda b, h: (0, 0))
    scratch = [
        pltpu.VMEM((seq + 2 * _POOL_PAD, dh), _F32),
        pltpu.VMEM((seq, dh), _BF16),
        pltpu.VMEM((seq, dh), _F32),
        pltpu.VMEM((seq, dh), _F32),
        pltpu.VMEM((5, _RET_CHUNK, _RET_CHUNK), _F32),
        pltpu.VMEM((dh, dh), _F32),
        pltpu.VMEM((dh, dh), _F32),
    ]
    vmem = (2 * 7 * _nbytes((seq, dh), _BF16) + 2 * 2 * _nbytes((seq, dh // 2), _F32)
            + 4 * _nbytes((seq, dh), _F32) + 8 * _nbytes((dh, dh), _F32))
    return pl.pallas_call(
        _mixer_kernel,
        grid=(bsz, heads),
        in_specs=[
            seq_block(0),
            seq_block(pool_blocks),
            seq_block(pool_blocks + heads),
            seq_block(pool_blocks + 2 * heads),
            seq_block(pool_blocks + 3 * heads),
            const((seq, dh // 2)), const((seq, dh // 2)),
            per_head((dh, dh)),
            per_head((1, dh)),
            per_head((1, _RET_CHUNK)), per_head((1, _RET_CHUNK)),
        ],
        out_specs=[pl.BlockSpec((seq, dh), lambda b, h: (b, h)),
                   pl.BlockSpec((seq, dh), lambda b, h: (b, h))],
        out_shape=[jax.ShapeDtypeStruct((t, pool_blocks * dh), _BF16),
                   jax.ShapeDtypeStruct((t, heads * dh), _BF16)],
        scratch_shapes=scratch,
        compiler_params=pltpu.CompilerParams(
            dimension_semantics=("arbitrary", "arbitrary"), vmem_limit_bytes=_vmem_limit(vmem)),
        name="mixer",
    )(proj, proj, proj, proj, proj, cos, sin, pool_w, pool_scale, dec_f, dec_b)


def _outproj_kernel(p_ref, r_ref, x_ref, mod_ref, gain_ref, w_ref, x1_ref, h2_ref):
    kp = p_ref.shape[1]
    gate = mod_ref[0, 2:3, :]
    scale = gain_ref[...] * (1.0 + mod_ref[0, 4:5, :])
    shift = mod_ref[0, 3:4, :]
    for r0 in range(0, x_ref.shape[0], _DOT_ROWS):
        rows = slice(r0, r0 + _DOT_ROWS)
        mixed = _dot(p_ref[rows, :], w_ref[0:kp, :]) + _dot(r_ref[rows, :], w_ref[kp:, :])
        x1_ref[rows, :] = x_ref[rows, :] + gate * mixed
        _rmsnorm_rows(x1_ref, h2_ref, r0, _DOT_ROWS, scale, shift)


def _outproj(pool_out, ret_out, x2d, mod3, gain, w, seq, *, tm=512):
    t, d = x2d.shape
    kp, kr = pool_out.shape[1], ret_out.shape[1]
    per_batch = seq // tm
    vmem = (2 * _nbytes((tm, kp + kr), _BF16) + 4 * _nbytes((tm, d), _F32) + 2 * _nbytes((tm, d), _BF16)
            + _nbytes((kp + kr, d), _BF16) + _nbytes((_DOT_ROWS, d), _F32))
    return pl.pallas_call(
        _outproj_kernel,
        grid=(t // tm,),
        in_specs=[
            pl.BlockSpec((tm, kp), lambda i: (i, 0)),
            pl.BlockSpec((tm, kr), lambda i: (i, 0)),
            pl.BlockSpec((tm, d), lambda i: (i, 0)),
            pl.BlockSpec((1, _N_MOD, d), lambda i: (i // per_batch, 0, 0)),
            pl.BlockSpec((1, d), lambda i: (0, 0)),
            pl.BlockSpec((kp + kr, d), lambda i: (0, 0), pipeline_mode=pl.Buffered(1)),
        ],
        out_specs=[pl.BlockSpec((tm, d), lambda i: (i, 0)),
                   pl.BlockSpec((tm, d), lambda i: (i, 0))],
        out_shape=[jax.ShapeDtypeStruct((t, d), _F32),
                   jax.ShapeDtypeStruct((t, d), _BF16)],
        compiler_params=pltpu.CompilerParams(
            dimension_semantics=("arbitrary",), vmem_limit_bytes=_vmem_limit(vmem)),
        name="outproj",
    )(pool_out, ret_out, x2d, mod3, gain, w)


def _ffn_kernel(h2_ref, wg_ref, wu_ref, wd_ref, x1_hbm, mod_ref, gain_ref, o_ref, x1_buf, sem,
                *, final_norm):
    i, f = pl.program_id(0), pl.program_id(1)
    tm, d = o_ref.shape

    def x1_copy():
        return pltpu.make_async_copy(x1_hbm.at[pl.ds(pl.multiple_of(i * tm, tm), tm), :], x1_buf, sem.at[0])

    def down():
        h2 = h2_ref[...]
        hidden = (_silu(_dot(h2, wg_ref[...])) * _dot(h2, wu_ref[...])).astype(_BF16)
        return _dot(hidden, wd_ref[...])

    @pl.when(f == 0)
    def _():
        x1_copy().start()
        o_ref[...] = down()

    @pl.when(f > 0)
    def _():
        o_ref[...] += down()

    @pl.when(f == pl.num_programs(1) - 1)
    def _():
        x1_copy().wait()
        gate = mod_ref[0, 5:6, :]
        gain = gain_ref[...]
        for r0 in range(0, tm, _NORM_ROWS):
            rows = slice(r0, r0 + _NORM_ROWS)
            ss = None
            for cols in _col_blocks(d):
                x2 = x1_buf[rows, cols] + gate[:, cols] * o_ref[rows, cols]
                o_ref[rows, cols] = x2
                ss = x2 * x2 if ss is None else ss + x2 * x2
            if final_norm:
                rs = lax.rsqrt(jnp.sum(ss, axis=-1, keepdims=True) / d + _EPS)
                for cols in _col_blocks(d):
                    o_ref[rows, cols] = o_ref[rows, cols] * rs * gain[:, cols]


def _ffn(h2, x1, mod3, wg, wu, wd, final_gain, seq, *, final_norm, tm=1024, tf=512):
    t, d = x1.shape
    dff = wg.shape[1]
    per_batch = seq // tm
    vmem = (2 * _nbytes((tm, d), _BF16) + 6 * _nbytes((d, tf), _BF16) + 3 * _nbytes((tm, d), _F32)
            + 3 * _nbytes((tm, tf), _F32))
    return pl.pallas_call(
        functools.partial(_ffn_kernel, final_norm=final_norm),
        grid=(t // tm, dff // tf),
        in_specs=[
            pl.BlockSpec((tm, d), lambda i, f: (i, 0)),
            pl.BlockSpec((d, tf), lambda i, f: (0, f)),
            pl.BlockSpec((d, tf), lambda i, f: (0, f)),
            pl.BlockSpec((tf, d), lambda i, f: (f, 0)),
            pl.BlockSpec(memory_space=pl.ANY),
            pl.BlockSpec((1, _N_MOD, d), lambda i, f: (i // per_batch, 0, 0)),
            pl.BlockSpec((1, d), lambda i, f: (0, 0)),
        ],
        out_specs=pl.BlockSpec((tm, d), lambda i, f: (i, 0)),
        out_shape=jax.ShapeDtypeStruct((t, d), _F32),
        scratch_shapes=[pltpu.VMEM((tm, d), _F32), pltpu.SemaphoreType.DMA((1,))],
        compiler_params=pltpu.CompilerParams(
            dimension_semantics=("arbitrary", "arbitrary"), vmem_limit_bytes=_vmem_limit(vmem)),
        name="ffn",
    )(h2, wg, wu, wd, x1, mod3, final_gain)


def kernel(x, c, w_ada, b_ada, norm1_g, w_in, pool_w, pool_scale, ret_decay_fwd, ret_decay_bwd,
           w_out, norm2_g, w_gate, w_up, w_down, final_g):
    bsz, seq, d = x.shape
    depth = w_ada.shape[0]
    heads = _RET_HEADS
    pool_width = pool_scale.shape[1]
    dh = (w_in.shape[2] - pool_width) // (4 * heads)
    assert pool_width == len(_POOL_WINDOWS) * dh and seq % _RET_CHUNK == 0 and dh == _RET_CHUNK

    inv = 1.0 / (_ROPE_BASE ** jnp.linspace(0.0, 1.0, dh // 2, dtype=_F32))
    ang = jnp.arange(seq, dtype=_F32)[:, None] * inv[None, :]
    cos, sin = jnp.cos(ang), jnp.sin(ang)

    xf = x.reshape(bsz * seq, d)
    for l in range(depth):
        mod3 = _adaln(c, w_ada[l], b_ada[l]).reshape(bsz, _N_MOD, d)
        proj = _inproj(xf, mod3, norm1_g[l].reshape(1, d), w_in[l].astype(_BF16), seq)
        spread = lambda p: jnp.broadcast_to(p.astype(_F32)[:, None, None], (heads, 1, _RET_CHUNK))
        pool_out, ret_out = _mixer(
            proj, cos, sin, pool_w[l].astype(_BF16), pool_scale[l].reshape(len(_POOL_WINDOWS), 1, dh),
            spread(ret_decay_fwd[l]), spread(ret_decay_bwd[l]), bsz, seq)
        x1, h2 = _outproj(pool_out, ret_out, xf, mod3, norm2_g[l].reshape(1, d), w_out[l].astype(_BF16), seq)
        xf = _ffn(h2, x1, mod3, w_gate[l].astype(_BF16), w_up[l].astype(_BF16), w_down[l].astype(_BF16),
                  final_g.reshape(1, d), seq, final_norm=(l == depth - 1))
    return xf.reshape(bsz, seq, d)
```

```python
import functools

import jax
import jax.numpy as jnp
from jax import lax
from jax.experimental import pallas as pl
from jax.experimental.pallas import tpu as pltpu

_EPS = 1e-6
_POOL_WINDOWS = (2, 4, 8, 16)
_RET_HEADS = 4
_ROPE_BASE = 10000.0
_N_MOD = 6

_V7X_VMEM_BYTES = 64 * 1024 * 1024
_LANES = 128
_RET_CHUNK = 256
_POOL_HALO = 128
_POOL_EDGE_ROWS = 8
_NORM_ROWS = 128
_DOT_ROWS = 256

_F32 = jnp.float32
_BF16 = jnp.bfloat16


def _vmem_limit(nbytes):
    return int(min(_V7X_VMEM_BYTES - (2 << 20), nbytes + (8 << 20)))


def _nbytes(shape, dtype):
    n = 1
    for s in shape:
        n *= s
    return n * jnp.dtype(dtype).itemsize


def _dot(a, b):
    return jnp.dot(a, b, preferred_element_type=_F32)


def _silu(x):
    return x * jax.nn.sigmoid(x)


def _rms_rows(x):
    return x * lax.rsqrt(jnp.mean(x * x, axis=-1, keepdims=True) + _EPS)


def _col_blocks(d):
    return [slice(c0, c0 + _LANES) for c0 in range(0, d, _LANES)]


def _rmsnorm_rows(src_ref, dst_ref, row0, nrows, gain, shift=None):
    d = src_ref.shape[1]
    for r0 in range(row0, row0 + nrows, _NORM_ROWS):
        rows = slice(r0, r0 + _NORM_ROWS)
        ss = None
        for cols in _col_blocks(d):
            xc = src_ref[rows, cols]
            ss = xc * xc if ss is None else ss + xc * xc
        rs = lax.rsqrt(jnp.sum(ss, axis=-1, keepdims=True) / d + _EPS)
        for cols in _col_blocks(d):
            y = src_ref[rows, cols] * rs * gain[:, cols]
            if shift is not None:
                y = y + shift[:, cols]
            dst_ref[rows, cols] = y.astype(dst_ref.dtype)


def _adaln_kernel(c_ref, w_ref, b_ref, o_ref):
    c = c_ref[...]
    o_ref[...] = _dot(_silu(c).astype(_BF16), w_ref[...].astype(_BF16)) + b_ref[...]


def _adaln(c, w, b, *, tn=1024):
    bsz, d = c.shape
    n = w.shape[1]
    vmem = 2 * _nbytes((d, tn), _F32) + _nbytes((d, tn), _BF16) + 4 * _nbytes((bsz, d), _F32)
    return pl.pallas_call(
        _adaln_kernel,
        grid=(n // tn,),
        in_specs=[
            pl.BlockSpec((bsz, d), lambda j: (0, 0)),
            pl.BlockSpec((d, tn), lambda j: (0, j)),
            pl.BlockSpec((1, tn), lambda j: (0, j)),
        ],
        out_specs=pl.BlockSpec((bsz, tn), lambda j: (0, j)),
        out_shape=jax.ShapeDtypeStruct((bsz, n), _F32),
        compiler_params=pltpu.CompilerParams(
            dimension_semantics=("arbitrary",), vmem_limit_bytes=_vmem_limit(vmem)),
        name="adaln",
    )(c, w, b.reshape(1, n))


def _inproj_kernel(x_ref, mod_ref, gain_ref, w_ref, o_ref, h_ref):
    scale = gain_ref[...] * (1.0 + mod_ref[0, 1:2, :])
    shift = mod_ref[0, 0:1, :]
    for r0 in range(0, x_ref.shape[0], _DOT_ROWS):
        rows = slice(r0, r0 + _DOT_ROWS)
        _rmsnorm_rows(x_ref, h_ref, r0, _DOT_ROWS, scale, shift)
        o_ref[rows, :] = _dot(h_ref[rows, :], w_ref[...]).astype(o_ref.dtype)


def _inproj(x2d, mod3, gain, w, seq, *, tm=512):
    t, d = x2d.shape
    n = w.shape[1]
    per_batch = seq // tm
    vmem = (2 * _nbytes((tm, d), _F32) + _nbytes((d, n), _BF16) + 2 * _nbytes((tm, n), _BF16)
            + _nbytes((tm, d), _BF16) + _nbytes((_DOT_ROWS, n), _F32))
    return pl.pallas_call(
        _inproj_kernel,
        grid=(t // tm,),
        in_specs=[
            pl.BlockSpec((tm, d), lambda i: (i, 0)),
            pl.BlockSpec((1, _N_MOD, d), lambda i: (i // per_batch, 0, 0)),
            pl.BlockSpec((1, d), lambda i: (0, 0)),
            pl.BlockSpec((d, n), lambda i: (0, 0), pipeline_mode=pl.Buffered(1)),
        ],
        out_specs=pl.BlockSpec((tm, n), lambda i: (i, 0)),
        out_shape=jax.ShapeDtypeStruct((t, n), _BF16),
        scratch_shapes=[pltpu.VMEM((tm, d), _BF16)],
        compiler_params=pltpu.CompilerParams(
            dimension_semantics=("arbitrary",), vmem_limit_bytes=_vmem_limit(vmem)),
        name="inproj",
    )(x2d, mod3, gain, w)


def _pool_chunks(u_ref, pad_ref, group, seq):
    c = _RET_CHUNK
    n = seq // c
    halo = _POOL_HALO
    half = jnp.int32(0)
    inv_w = jnp.float32(0)
    for gi, window in enumerate(_POOL_WINDOWS):
        half = jnp.where(group == gi, jnp.int32(window // 2), half)
        inv_w = jnp.where(group == gi, jnp.float32(1.0 / window), inv_w)

    zeros = jnp.zeros((halo, pad_ref.shape[1]), pad_ref.dtype)
    pad_ref[0:halo, :] = zeros
    pad_ref[halo + seq:, :] = zeros
    pad_ref[halo:halo + seq, :] = u_ref[...]

    rel = (lax.broadcasted_iota(jnp.int32, (c, c + 2 * halo), 1) - halo
           - lax.broadcasted_iota(jnp.int32, (c, c + 2 * halo), 0))
    band = (jnp.where((rel >= -half) & (rel < half), inv_w, 0.0)
            - jnp.where(rel == 0, 1.0, 0.0)).astype(_BF16)

    edge = _POOL_EDGE_ROWS
    t_lo = lax.broadcasted_iota(jnp.int32, (edge, pad_ref.shape[1]), 0)
    t_hi = t_lo + (seq - edge)
    width = (2 * half).astype(_F32)
    ratio_lo = width / (t_lo + half - jnp.maximum(t_lo - half, 0)).astype(_F32)
    ratio_hi = width / (jnp.minimum(t_hi + half, seq) - (t_hi - half)).astype(_F32)

    def pooled_chunk(ci):
        pooled = _dot(band, pad_ref[ci * c:ci * c + c + 2 * halo, :])
        if ci == 0:
            u_lo = u_ref[0:edge, :].astype(_F32)
            pooled = jnp.concatenate([(pooled[0:edge] + u_lo) * ratio_lo - u_lo, pooled[edge:]], axis=0)
        if ci == n - 1:
            u_hi = u_ref[seq - edge:seq, :].astype(_F32)
            pooled = jnp.concatenate([pooled[:c - edge], (pooled[c - edge:] + u_hi) * ratio_hi - u_hi], axis=0)
        return pooled.astype(_BF16)

    return pooled_chunk


def _mixer_kernel(u_ref, q_ref, k_ref, v_ref, gate_ref, cos_ref, sin_ref, pw_ref, ps_ref,
                  df_ref, db_ref, pool_ref, ret_ref,
                  pad_ref, pooled_ref, qr_ref, kb_ref, acc_ref, dec_ref, sf_ref, sb_ref):
    seq, dh = q_ref.shape
    c = _RET_CHUNK
    n = seq // c
    half = dh // 2

    pooled_chunk = _pool_chunks(u_ref, pad_ref, pl.program_id(1), seq)

    lg_f = -jnp.exp(df_ref[0])
    lg_b = -jnp.exp(db_ref[0])
    ri = lax.broadcasted_iota(jnp.int32, (c, c), 0).astype(_F32)
    cj = lax.broadcasted_iota(jnp.int32, (c, c), 1).astype(_F32)
    diff = ri - cj
    qk_scale = dh ** -0.5
    dec_ref[0] = jnp.exp(jnp.where(diff >= 0, diff * lg_f, -diff * lg_b)) * qk_scale
    dec_ref[1] = jnp.exp((ri + 1.0) * lg_f) * qk_scale
    dec_ref[2] = jnp.exp((c - 1.0 - ri) * lg_f)
    dec_ref[3] = jnp.exp((c - ri) * lg_b) * qk_scale
    dec_ref[4] = jnp.exp(ri * lg_b)
    chunk_f = jnp.exp(c * lg_f)
    chunk_b = jnp.exp(c * lg_b)

    def tn_dot(a, b):
        return lax.dot_general(a, b, (((0,), (0,)), ((), ())), preferred_element_type=_F32)

    def nt_dot(a, b):
        return lax.dot_general(a, b, (((1,), (1,)), ((), ())), preferred_element_type=_F32)

    def rotary(t, cos, sin):
        t1, t2 = t[:, :half], t[:, half:]
        return jnp.concatenate([t1 * cos - t2 * sin, t1 * sin + t2 * cos], axis=-1)

    for ci in range(n):
        rows = slice(ci * c, (ci + 1) * c)
        cos, sin = cos_ref[rows, :], sin_ref[rows, :]
        q = rotary(q_ref[rows, :].astype(_F32), cos, sin).astype(_BF16)
        k32 = rotary(k_ref[rows, :].astype(_F32), cos, sin)
        qr_ref[rows, :] = q
        if ci > 0:
            kb_ref[rows, :] = (k32 * dec_ref[4]).astype(_BF16)
        v = v_ref[rows, :]
        scores = nt_dot(q, k32.astype(_BF16)) * dec_ref[0]
        out = _dot(scores.astype(_BF16), v)
        if ci > 0:
            out = out + _dot(q, sf_ref[...].astype(_BF16)) * dec_ref[1]
        acc_ref[rows, :] = out
        if ci < n - 1:
            upd = tn_dot((k32 * dec_ref[2]).astype(_BF16), v)
            sf_ref[...] = upd if ci == 0 else sf_ref[...] * chunk_f + upd
        pooled_ref[rows, :] = pooled_chunk(ci)

    pool_ref[...] = (_dot(pooled_ref[...], pw_ref[0]) * ps_ref[0]).astype(pool_ref.dtype)

    for ci in reversed(range(n)):
        rows = slice(ci * c, (ci + 1) * c)
        out = acc_ref[rows, :]
        if ci < n - 1:
            out = out + _dot(qr_ref[rows, :], sb_ref[...].astype(_BF16)) * dec_ref[3]
        ret = _rms_rows(out)
        ret_ref[rows, :] = (_silu(gate_ref[rows, :].astype(_F32)) * ret).astype(ret_ref.dtype)
        if ci > 0:
            upd = tn_dot(kb_ref[rows, :], v_ref[rows, :])
            sb_ref[...] = upd if ci == n - 1 else sb_ref[...] * chunk_b + upd


def _mixer(proj, cos, sin, pool_w, pool_scale, dec_f, dec_b, bsz, seq):
    t = proj.shape[0]
    heads = _RET_HEADS
    dh = pool_w.shape[-1]
    pool_blocks = len(_POOL_WINDOWS)
    col = lambda first: (lambda b, h: (b, first + h))
    seq_block = lambda first: pl.BlockSpec((seq, dh), col(first))
    per_head = lambda shape: pl.BlockSpec((1,) + shape, lambda b, h: (h, 0, 0))
    const = lambda shape: pl.BlockSpec(shape, lambda b, h: (0, 0))
    scratch = [
        pltpu.VMEM((seq + 2 * _POOL_HALO, dh), _BF16),
        pltpu.VMEM((seq, dh), _BF16),
        pltpu.VMEM((seq, dh), _BF16),
        pltpu.VMEM((seq, dh), _BF16),
        pltpu.VMEM((seq, dh), _F32),
        pltpu.VMEM((5, _RET_CHUNK, _RET_CHUNK), _F32),
        pltpu.VMEM((dh, dh), _F32),
        pltpu.VMEM((dh, dh), _F32),
    ]
    vmem = (2 * 7 * _nbytes((seq, dh), _BF16) + 2 * 2 * _nbytes((seq, dh // 2), _F32)
            + 4 * _nbytes((seq, dh), _F32) + 8 * _nbytes((dh, dh), _F32))
    return pl.pallas_call(
        _mixer_kernel,
        grid=(bsz, heads),
        in_specs=[
            seq_block(0),
            seq_block(pool_blocks),
            seq_block(pool_blocks + heads),
            seq_block(pool_blocks + 2 * heads),
            seq_block(pool_blocks + 3 * heads),
            const((seq, dh // 2)), const((seq, dh // 2)),
            per_head((dh, dh)),
            per_head((1, dh)),
            per_head((1, _RET_CHUNK)), per_head((1, _RET_CHUNK)),
        ],
        out_specs=[pl.BlockSpec((seq, dh), lambda b, h: (b, h)),
                   pl.BlockSpec((seq, dh), lambda b, h: (b, h))],
        out_shape=[jax.ShapeDtypeStruct((t, pool_blocks * dh), _BF16),
                   jax.ShapeDtypeStruct((t, heads * dh), _BF16)],
        scratch_shapes=scratch,
        compiler_params=pltpu.CompilerParams(
            dimension_semantics=("arbitrary", "arbitrary"), vmem_limit_bytes=_vmem_limit(vmem)),
        name="mixer",
    )(proj, proj, proj, proj, proj, cos, sin, pool_w, pool_scale, dec_f, dec_b)


def _outproj_kernel(p_ref, r_ref, x_ref, mod_ref, gain_ref, w_ref, x1_ref, h2_ref):
    kp = p_ref.shape[1]
    gate = mod_ref[0, 2:3, :]
    scale = gain_ref[...] * (1.0 + mod_ref[0, 4:5, :])
    shift = mod_ref[0, 3:4, :]
    for r0 in range(0, x_ref.shape[0], _NORM_ROWS):
        rows = slice(r0, r0 + _NORM_ROWS)
        mixed = _dot(p_ref[rows, :], w_ref[0:kp, :]) + _dot(r_ref[rows, :], w_ref[kp:, :])
        x1_ref[rows, :] = x_ref[rows, :] + gate * mixed
        _rmsnorm_rows(x1_ref, h2_ref, r0, _NORM_ROWS, scale, shift)


def _outproj(pool_out, ret_out, x2d, mod3, gain, w, seq, *, tm=512):
    t, d = x2d.shape
    kp, kr = pool_out.shape[1], ret_out.shape[1]
    per_batch = seq // tm
    vmem = (2 * _nbytes((tm, kp + kr), _BF16) + 4 * _nbytes((tm, d), _F32) + 2 * _nbytes((tm, d), _BF16)
            + _nbytes((kp + kr, d), _BF16) + _nbytes((_DOT_ROWS, d), _F32))
    return pl.pallas_call(
        _outproj_kernel,
        grid=(t // tm,),
        in_specs=[
            pl.BlockSpec((tm, kp), lambda i: (i, 0)),
            pl.BlockSpec((tm, kr), lambda i: (i, 0)),
            pl.BlockSpec((tm, d), lambda i: (i, 0)),
            pl.BlockSpec((1, _N_MOD, d), lambda i: (i // per_batch, 0, 0)),
            pl.BlockSpec((1, d), lambda i: (0, 0)),
            pl.BlockSpec((kp + kr, d), lambda i: (0, 0), pipeline_mode=pl.Buffered(1)),
        ],
        out_specs=[pl.BlockSpec((tm, d), lambda i: (i, 0)),
                   pl.BlockSpec((tm, d), lambda i: (i, 0))],
        out_shape=[jax.ShapeDtypeStruct((t, d), _F32),
                   jax.ShapeDtypeStruct((t, d), _BF16)],
        compiler_params=pltpu.CompilerParams(
            dimension_semantics=("arbitrary",), vmem_limit_bytes=_vmem_limit(vmem)),
        name="outproj",
    )(pool_out, ret_out, x2d, mod3, gain, w)


def _ffn_kernel(h2_ref, wg_ref, wu_ref, wd_ref, x1_hbm, mod_ref, gain_ref, o_ref, x1_buf, sem,
                *, final_norm):
    i, f = pl.program_id(0), pl.program_id(1)
    tm, d = o_ref.shape

    def x1_copy():
        return pltpu.make_async_copy(x1_hbm.at[pl.ds(pl.multiple_of(i * tm, tm), tm), :], x1_buf, sem.at[0])

    last = pl.num_programs(1) - 1

    def hidden():
        h2 = h2_ref[...]
        return (_silu(_dot(h2, wg_ref[...])) * _dot(h2, wu_ref[...])).astype(_BF16)

    def finish_rows(r0):
        rows = slice(r0, r0 + _NORM_ROWS)
        gate = mod_ref[0, 5:6, :]
        gain = gain_ref[...]
        ss = None
        for cols in _col_blocks(d):
            x2 = x1_buf[rows, cols] + gate[:, cols] * o_ref[rows, cols]
            o_ref[rows, cols] = x2
            ss = x2 * x2 if ss is None else ss + x2 * x2
        if final_norm:
            rs = lax.rsqrt(jnp.sum(ss, axis=-1, keepdims=True) / d + _EPS)
            for cols in _col_blocks(d):
                o_ref[rows, cols] = o_ref[rows, cols] * rs * gain[:, cols]

    @pl.when(f == 0)
    def _():
        x1_copy().start()
        o_ref[...] = _dot(hidden(), wd_ref[...])

    @pl.when((f > 0) & (f < last))
    def _():
        o_ref[...] += _dot(hidden(), wd_ref[...])

    @pl.when(f == last)
    def _():
        x1_copy().wait()
        hid = hidden()
        for r0 in range(0, tm, _DOT_ROWS):
            rows = slice(r0, r0 + _DOT_ROWS)
            o_ref[rows, :] += _dot(hid[rows, :], wd_ref[...])
            for s0 in range(r0, r0 + _DOT_ROWS, _NORM_ROWS):
                finish_rows(s0)


def _ffn(h2, x1, mod3, wg, wu, wd, final_gain, seq, *, final_norm, tm=1024, tf=512):
    t, d = x1.shape
    dff = wg.shape[1]
    assert dff // tf >= 2
    per_batch = seq // tm
    vmem = (2 * _nbytes((tm, d), _BF16) + 6 * _nbytes((d, tf), _BF16) + 3 * _nbytes((tm, d), _F32)
            + 3 * _nbytes((tm, tf), _F32))
    return pl.pallas_call(
        functools.partial(_ffn_kernel, final_norm=final_norm),
        grid=(t // tm, dff // tf),
        in_specs=[
            pl.BlockSpec((tm, d), lambda i, f: (i, 0)),
            pl.BlockSpec((d, tf), lambda i, f: (0, f)),
            pl.BlockSpec((d, tf), lambda i, f: (0, f)),
            pl.BlockSpec((tf, d), lambda i, f: (f, 0)),
            pl.BlockSpec(memory_space=pl.ANY),
            pl.BlockSpec((1, _N_MOD, d), lambda i, f: (i // per_batch, 0, 0)),
            pl.BlockSpec((1, d), lambda i, f: (0, 0)),
        ],
        out_specs=pl.BlockSpec((tm, d), lambda i, f: (i, 0)),
        out_shape=jax.ShapeDtypeStruct((t, d), _F32),
        scratch_shapes=[pltpu.VMEM((tm, d), _F32), pltpu.SemaphoreType.DMA((1,))],
        compiler_params=pltpu.CompilerParams(
            dimension_semantics=("arbitrary", "arbitrary"), vmem_limit_bytes=_vmem_limit(vmem)),
        name="ffn",
    )(h2, wg, wu, wd, x1, mod3, final_gain)


def kernel(x, c, w_ada, b_ada, norm1_g, w_in, pool_w, pool_scale, ret_decay_fwd, ret_decay_bwd,
           w_out, norm2_g, w_gate, w_up, w_down, final_g):
    bsz, seq, d = x.shape
    depth = w_ada.shape[0]
    heads = _RET_HEADS
    pool_width = pool_scale.shape[1]
    dh = (w_in.shape[2] - pool_width) // (4 * heads)
    assert pool_width == len(_POOL_WINDOWS) * dh and seq % _RET_CHUNK == 0 and dh == _RET_CHUNK
    assert all(w & (w - 1) == 0 and w // 2 <= _POOL_EDGE_ROWS for w in _POOL_WINDOWS)

    inv = 1.0 / (_ROPE_BASE ** jnp.linspace(0.0, 1.0, dh // 2, dtype=_F32))
    ang = jnp.arange(seq, dtype=_F32)[:, None] * inv[None, :]
    cos, sin = jnp.cos(ang), jnp.sin(ang)

    xf = x.reshape(bsz * seq, d)
    for l in range(depth):
        mod3 = _adaln(c, w_ada[l], b_ada[l]).reshape(bsz, _N_MOD, d)
        proj = _inproj(xf, mod3, norm1_g[l].reshape(1, d), w_in[l].astype(_BF16), seq)
        spread = lambda p: jnp.broadcast_to(p.astype(_F32)[:, None, None], (heads, 1, _RET_CHUNK))
        pool_out, ret_out = _mixer(
            proj, cos, sin, pool_w[l].astype(_BF16), pool_scale[l].reshape(len(_POOL_WINDOWS), 1, dh),
            spread(ret_decay_fwd[l]), spread(ret_decay_bwd[l]), bsz, seq)
        x1, h2 = _outproj(pool_out, ret_out, xf, mod3, norm2_g[l].reshape(1, d), w_out[l].astype(_BF16), seq)
        xf = _ffn(h2, x1, mod3, w_gate[l].astype(_BF16), w_up[l].astype(_BF16), w_down[l].astype(_BF16),
                  final_g.reshape(1, d), seq, final_norm=(l == depth - 1))
    return xf.reshape(bsz, seq, d)
```

```python
import functools

import jax
import jax.numpy as jnp
from jax import lax
from jax.experimental import pallas as pl
from jax.experimental.pallas import tpu as pltpu

_EPS = 1e-6
_POOL_WINDOWS = (2, 4, 8, 16)
_RET_HEADS = 4
_ROPE_BASE = 10000.0
_N_MOD = 6

_V7X_VMEM_BYTES = 64 * 1024 * 1024
_LANES = 128
_BF16_ROWS = 16
_RET_CHUNK = 256
_POOL_HALO = 128
_POOL_EDGE_ROWS = 8
_NORM_ROWS = 128
_DOT_ROWS = 256

_F32 = jnp.float32
_BF16 = jnp.bfloat16


def _vmem_limit(nbytes):
    return int(min(_V7X_VMEM_BYTES - (2 << 20), nbytes + (8 << 20)))


def _nbytes(shape, dtype):
    n = 1
    for s in shape:
        n *= s
    return n * jnp.dtype(dtype).itemsize


def _dot(a, b):
    return jnp.dot(a, b, preferred_element_type=_F32)


def _silu(x):
    return x * jax.nn.sigmoid(x)


def _rms_rows(x):
    return x * lax.rsqrt(jnp.mean(x * x, axis=-1, keepdims=True) + _EPS)


def _col_blocks(d):
    return [slice(c0, c0 + _LANES) for c0 in range(0, d, _LANES)]


def _rmsnorm_rows(src_ref, dst_ref, row0, nrows, gain, shift=None):
    d = src_ref.shape[1]
    for r0 in range(row0, row0 + nrows, _NORM_ROWS):
        rows = slice(r0, r0 + _NORM_ROWS)
        ss = None
        for cols in _col_blocks(d):
            xc = src_ref[rows, cols]
            ss = xc * xc if ss is None else ss + xc * xc
        rs = lax.rsqrt(jnp.sum(ss, axis=-1, keepdims=True) / d + _EPS)
        for cols in _col_blocks(d):
            y = src_ref[rows, cols] * rs * gain[:, cols]
            if shift is not None:
                y = y + shift[:, cols]
            dst_ref[rows, cols] = y.astype(dst_ref.dtype)


def _adaln_kernel(c_ref, w_ref, b_ref, o_ref):
    c = c_ref[...]
    o_ref[...] = _dot(_silu(c).astype(_BF16), w_ref[...].astype(_BF16)) + b_ref[...]


def _adaln(c, w, b, *, tn=1024):
    bsz, d = c.shape
    n = w.shape[1]
    vmem = 2 * _nbytes((d, tn), _F32) + _nbytes((d, tn), _BF16) + 4 * _nbytes((bsz, d), _F32)
    return pl.pallas_call(
        _adaln_kernel,
        grid=(n // tn,),
        in_specs=[
            pl.BlockSpec((bsz, d), lambda j: (0, 0)),
            pl.BlockSpec((d, tn), lambda j: (0, j)),
            pl.BlockSpec((1, tn), lambda j: (0, j)),
        ],
        out_specs=pl.BlockSpec((bsz, tn), lambda j: (0, j)),
        out_shape=jax.ShapeDtypeStruct((bsz, n), _F32),
        compiler_params=pltpu.CompilerParams(
            dimension_semantics=("arbitrary",), vmem_limit_bytes=_vmem_limit(vmem)),
        name="adaln",
    )(c, w, b.reshape(1, n))


def _slab_specs(weights, steps, index_map):
    specs, shapes, nbytes = [], [], 0
    for w in weights:
        rows, cols = w.shape
        assert rows % steps == 0 and (rows // steps) % _BF16_ROWS == 0
        specs.append(pl.BlockSpec((rows // steps, cols), index_map))
        shapes.append(jax.ShapeDtypeStruct(w.shape, _BF16))
        nbytes += 2 * _nbytes((rows // steps, cols), _F32) + 2 * _nbytes((rows // steps, cols), _BF16)
    return specs, shapes, nbytes


def _cast_slabs(src_refs, dst_refs):
    for src, dst in zip(src_refs, dst_refs):
        dst[...] = src[...].astype(dst.dtype)


def _inproj_kernel(x_ref, mod_ref, gain_ref, w_ref, *rest, n_cast):
    cast_in, (o_ref, *cast_out), h_ref = rest[:n_cast], rest[n_cast:2 * n_cast + 1], rest[-1]
    scale = gain_ref[...] * (1.0 + mod_ref[0, 1:2, :])
    shift = mod_ref[0, 0:1, :]
    for r0 in range(0, x_ref.shape[0], _DOT_ROWS):
        rows = slice(r0, r0 + _DOT_ROWS)
        _rmsnorm_rows(x_ref, h_ref, r0, _DOT_ROWS, scale, shift)
        o_ref[rows, :] = _dot(h_ref[rows, :], w_ref[...]).astype(o_ref.dtype)
    _cast_slabs(cast_in, cast_out)


def _inproj(x2d, mod3, gain, w, seq, cast_weights, *, tm=512):
    t, d = x2d.shape
    n = w.shape[1]
    per_batch = seq // tm
    cast_specs, cast_shapes, cast_bytes = _slab_specs(cast_weights, t // tm, lambda i: (i, 0))
    vmem = (2 * _nbytes((tm, d), _F32) + _nbytes((d, n), _BF16) + 2 * _nbytes((tm, n), _BF16)
            + _nbytes((tm, d), _BF16) + _nbytes((_DOT_ROWS, n), _F32) + cast_bytes)
    return pl.pallas_call(
        functools.partial(_inproj_kernel, n_cast=len(cast_weights)),
        grid=(t // tm,),
        in_specs=[
            pl.BlockSpec((tm, d), lambda i: (i, 0)),
            pl.BlockSpec((1, _N_MOD, d), lambda i: (i // per_batch, 0, 0)),
            pl.BlockSpec((1, d), lambda i: (0, 0)),
            pl.BlockSpec((d, n), lambda i: (0, 0), pipeline_mode=pl.Buffered(1)),
        ] + cast_specs,
        out_specs=[pl.BlockSpec((tm, n), lambda i: (i, 0))] + cast_specs,
        out_shape=[jax.ShapeDtypeStruct((t, n), _BF16)] + cast_shapes,
        scratch_shapes=[pltpu.VMEM((tm, d), _BF16)],
        compiler_params=pltpu.CompilerParams(
            dimension_semantics=("arbitrary",), vmem_limit_bytes=_vmem_limit(vmem)),
        name="inproj",
    )(x2d, mod3, gain, w, *cast_weights)


def _pool_chunks(u_ref, pad_ref, group, seq):
    c = _RET_CHUNK
    n = seq // c
    halo = _POOL_HALO
    half = jnp.int32(0)
    inv_w = jnp.float32(0)
    for gi, window in enumerate(_POOL_WINDOWS):
        half = jnp.where(group == gi, jnp.int32(window // 2), half)
        inv_w = jnp.where(group == gi, jnp.float32(1.0 / window), inv_w)

    zeros = jnp.zeros((halo, pad_ref.shape[1]), pad_ref.dtype)
    pad_ref[0:halo, :] = zeros
    pad_ref[halo + seq:, :] = zeros
    pad_ref[halo:halo + seq, :] = u_ref[...]

    rel = (lax.broadcasted_iota(jnp.int32, (c, c + 2 * halo), 1) - halo
           - lax.broadcasted_iota(jnp.int32, (c, c + 2 * halo), 0))
    band = (jnp.where((rel >= -half) & (rel < half), inv_w, 0.0)
            - jnp.where(rel == 0, 1.0, 0.0)).astype(_BF16)

    edge = _POOL_EDGE_ROWS
    t_lo = lax.broadcasted_iota(jnp.int32, (edge, pad_ref.shape[1]), 0)
    t_hi = t_lo + (seq - edge)
    width = (2 * half).astype(_F32)
    ratio_lo = width / (t_lo + half - jnp.maximum(t_lo - half, 0)).astype(_F32)
    ratio_hi = width / (jnp.minimum(t_hi + half, seq) - (t_hi - half)).astype(_F32)

    def pooled_chunk(ci):
        pooled = _dot(band, pad_ref[ci * c:ci * c + c + 2 * halo, :])
        if ci == 0:
            u_lo = u_ref[0:edge, :].astype(_F32)
            pooled = jnp.concatenate([(pooled[0:edge] + u_lo) * ratio_lo - u_lo, pooled[edge:]], axis=0)
        if ci == n - 1:
            u_hi = u_ref[seq - edge:seq, :].astype(_F32)
            pooled = jnp.concatenate([pooled[:c - edge], (pooled[c - edge:] + u_hi) * ratio_hi - u_hi], axis=0)
        return pooled.astype(_BF16)

    return pooled_chunk


def _mixer_kernel(u_ref, q_ref, k_ref, v_ref, gate_ref, cos_ref, sin_ref, pw_ref, ps_ref,
                  df_ref, db_ref, *rest, n_cast):
    cast_in, (pool_ref, ret_ref, *cast_out) = rest[:n_cast], rest[n_cast:2 * n_cast + 2]
    pad_ref, pooled_ref, qr_ref, kb_ref, acc_ref, dec_ref, sf_ref, sb_ref = rest[2 * n_cast + 2:]
    _cast_slabs(cast_in, cast_out)
    seq, dh = q_ref.shape
    c = _RET_CHUNK
    n = seq // c
    half = dh // 2

    pooled_chunk = _pool_chunks(u_ref, pad_ref, pl.program_id(1), seq)

    lg_f = -jnp.exp(df_ref[0])
    lg_b = -jnp.exp(db_ref[0])
    ri = lax.broadcasted_iota(jnp.int32, (c, c), 0).astype(_F32)
    cj = lax.broadcasted_iota(jnp.int32, (c, c), 1).astype(_F32)
    diff = ri - cj
    qk_scale = dh ** -0.5
    dec_ref[0] = jnp.exp(jnp.where(diff >= 0, diff * lg_f, -diff * lg_b)) * qk_scale
    dec_ref[1] = jnp.exp((ri + 1.0) * lg_f) * qk_scale
    dec_ref[2] = jnp.exp((c - 1.0 - ri) * lg_f)
    dec_ref[3] = jnp.exp((c - ri) * lg_b) * qk_scale
    dec_ref[4] = jnp.exp(ri * lg_b)
    chunk_f = jnp.exp(c * lg_f)
    chunk_b = jnp.exp(c * lg_b)

    def tn_dot(a, b):
        return lax.dot_general(a, b, (((0,), (0,)), ((), ())), preferred_element_type=_F32)

    def nt_dot(a, b):
        return lax.dot_general(a, b, (((1,), (1,)), ((), ())), preferred_element_type=_F32)

    def rotary(t, cos, sin):
        t1, t2 = t[:, :half], t[:, half:]
        return jnp.concatenate([t1 * cos - t2 * sin, t1 * sin + t2 * cos], axis=-1)

    for ci in range(n):
        rows = slice(ci * c, (ci + 1) * c)
        cos, sin = cos_ref[rows, :], sin_ref[rows, :]
        q = rotary(q_ref[rows, :].astype(_F32), cos, sin).astype(_BF16)
        k32 = rotary(k_ref[rows, :].astype(_F32), cos, sin)
        qr_ref[rows, :] = q
        if ci > 0:
            kb_ref[rows, :] = (k32 * dec_ref[4]).astype(_BF16)
        v = v_ref[rows, :]
        scores = nt_dot(q, k32.astype(_BF16)) * dec_ref[0]
        out = _dot(scores.astype(_BF16), v)
        if ci > 0:
            out = out + _dot(q, sf_ref[...].astype(_BF16)) * dec_ref[1]
        acc_ref[rows, :] = out
        if ci < n - 1:
            upd = tn_dot((k32 * dec_ref[2]).astype(_BF16), v)
            sf_ref[...] = upd if ci == 0 else sf_ref[...] * chunk_f + upd
        pooled_ref[rows, :] = pooled_chunk(ci)

    pool_ref[...] = (_dot(pooled_ref[...], pw_ref[0]) * ps_ref[0]).astype(pool_ref.dtype)

    for ci in reversed(range(n)):
        rows = slice(ci * c, (ci + 1) * c)
        out = acc_ref[rows, :]
        if ci < n - 1:
            out = out + _dot(qr_ref[rows, :], sb_ref[...].astype(_BF16)) * dec_ref[3]
        ret = _rms_rows(out)
        ret_ref[rows, :] = (_silu(gate_ref[rows, :].astype(_F32)) * ret).astype(ret_ref.dtype)
        if ci > 0:
            upd = tn_dot(kb_ref[rows, :], v_ref[rows, :])
            sb_ref[...] = upd if ci == n - 1 else sb_ref[...] * chunk_b + upd


def _mixer(proj, cos, sin, pool_w, pool_scale, dec_f, dec_b, bsz, seq, cast_weights):
    t = proj.shape[0]
    heads = _RET_HEADS
    dh = pool_w.shape[-1]
    pool_blocks = len(_POOL_WINDOWS)
    col = lambda first: (lambda b, h: (b, first + h))
    seq_block = lambda first: pl.BlockSpec((seq, dh), col(first))
    per_head = lambda shape: pl.BlockSpec((1,) + shape, lambda b, h: (h, 0, 0))
    const = lambda shape: pl.BlockSpec(shape, lambda b, h: (0, 0))
    scratch = [
        pltpu.VMEM((seq + 2 * _POOL_HALO, dh), _BF16),
        pltpu.VMEM((seq, dh), _BF16),
        pltpu.VMEM((seq, dh), _BF16),
        pltpu.VMEM((seq, dh), _BF16),
        pltpu.VMEM((seq, dh), _F32),
        pltpu.VMEM((5, _RET_CHUNK, _RET_CHUNK), _F32),
        pltpu.VMEM((dh, dh), _F32),
        pltpu.VMEM((dh, dh), _F32),
    ]
    cast_specs, cast_shapes, cast_bytes = _slab_specs(
        cast_weights, bsz * heads, lambda b, h: (b * heads + h, 0))
    vmem = (2 * 7 * _nbytes((seq, dh), _BF16) + 2 * 2 * _nbytes((seq, dh // 2), _F32)
            + 4 * _nbytes((seq, dh), _F32) + 8 * _nbytes((dh, dh), _F32) + cast_bytes)
    return pl.pallas_call(
        functools.partial(_mixer_kernel, n_cast=len(cast_weights)),
        grid=(bsz, heads),
        in_specs=[
            seq_block(0),
            seq_block(pool_blocks),
            seq_block(pool_blocks + heads),
            seq_block(pool_blocks + 2 * heads),
            seq_block(pool_blocks + 3 * heads),
            const((seq, dh // 2)), const((seq, dh // 2)),
            per_head((dh, dh)),
            per_head((1, dh)),
            per_head((1, _RET_CHUNK)), per_head((1, _RET_CHUNK)),
        ] + cast_specs,
        out_specs=[pl.BlockSpec((seq, dh), lambda b, h: (b, h)),
                   pl.BlockSpec((seq, dh), lambda b, h: (b, h))] + cast_specs,
        out_shape=[jax.ShapeDtypeStruct((t, pool_blocks * dh), _BF16),
                   jax.ShapeDtypeStruct((t, heads * dh), _BF16)] + cast_shapes,
        scratch_shapes=scratch,
        compiler_params=pltpu.CompilerParams(
            dimension_semantics=("arbitrary", "arbitrary"), vmem_limit_bytes=_vmem_limit(vmem)),
        name="mixer",
    )(proj, proj, proj, proj, proj, cos, sin, pool_w, pool_scale, dec_f, dec_b, *cast_weights)


def _outproj_kernel(p_ref, r_ref, x_ref, mod_ref, gain_ref, w_ref, x1_ref, h2_ref):
    kp = p_ref.shape[1]
    gate = mod_ref[0, 2:3, :]
    scale = gain_ref[...] * (1.0 + mod_ref[0, 4:5, :])
    shift = mod_ref[0, 3:4, :]
    for r0 in range(0, x_ref.shape[0], _DOT_ROWS):
        rows = slice(r0, r0 + _DOT_ROWS)
        mixed = _dot(p_ref[rows, :], w_ref[0:kp, :]) + _dot(r_ref[rows, :], w_ref[kp:, :])
        x1_ref[rows, :] = x_ref[rows, :] + gate * mixed
        _rmsnorm_rows(x1_ref, h2_ref, r0, _DOT_ROWS, scale, shift)


def _outproj(pool_out, ret_out, x2d, mod3, gain, w, seq, *, tm=512):
    t, d = x2d.shape
    kp, kr = pool_out.shape[1], ret_out.shape[1]
    per_batch = seq // tm
    vmem = (2 * _nbytes((tm, kp + kr), _BF16) + 4 * _nbytes((tm, d), _F32) + 2 * _nbytes((tm, d), _BF16)
            + _nbytes((kp + kr, d), _BF16) + _nbytes((_DOT_ROWS, d), _F32))
    return pl.pallas_call(
        _outproj_kernel,
        grid=(t // tm,),
        in_specs=[
            pl.BlockSpec((tm, kp), lambda i: (i, 0)),
            pl.BlockSpec((tm, kr), lambda i: (i, 0)),
            pl.BlockSpec((tm, d), lambda i: (i, 0)),
            pl.BlockSpec((1, _N_MOD, d), lambda i: (i // per_batch, 0, 0)),
            pl.BlockSpec((1, d), lambda i: (0, 0)),
            pl.BlockSpec((kp + kr, d), lambda i: (0, 0), pipeline_mode=pl.Buffered(1)),
        ],
        out_specs=[pl.BlockSpec((tm, d), lambda i: (i, 0)),
                   pl.BlockSpec((tm, d), lambda i: (i, 0))],
        out_shape=[jax.ShapeDtypeStruct((t, d), _F32),
                   jax.ShapeDtypeStruct((t, d), _BF16)],
        compiler_params=pltpu.CompilerParams(
            dimension_semantics=("arbitrary",), vmem_limit_bytes=_vmem_limit(vmem)),
        name="outproj",
    )(pool_out, ret_out, x2d, mod3, gain, w)


def _ffn_kernel(h2_ref, wg_ref, wu_ref, wd_ref, x1_hbm, mod_ref, gain_ref, o_ref, x1_buf, sem,
                *, final_norm):
    i, f = pl.program_id(0), pl.program_id(1)
    tm, d = o_ref.shape

    def x1_copy():
        return pltpu.make_async_copy(x1_hbm.at[pl.ds(pl.multiple_of(i * tm, tm), tm), :], x1_buf, sem.at[0])

    last = pl.num_programs(1) - 1

    def hidden():
        h2 = h2_ref[...]
        return (_silu(_dot(h2, wg_ref[...])) * _dot(h2, wu_ref[...])).astype(_BF16)

    def finish_rows(r0):
        rows = slice(r0, r0 + _NORM_ROWS)
        gate = mod_ref[0, 5:6, :]
        gain = gain_ref[...]
        ss = None
        for cols in _col_blocks(d):
            x2 = x1_buf[rows, cols] + gate[:, cols] * o_ref[rows, cols]
            o_ref[rows, cols] = x2
            ss = x2 * x2 if ss is None else ss + x2 * x2
        if final_norm:
            rs = lax.rsqrt(jnp.sum(ss, axis=-1, keepdims=True) / d + _EPS)
            for cols in _col_blocks(d):
                o_ref[rows, cols] = o_ref[rows, cols] * rs * gain[:, cols]

    @pl.when(f == 0)
    def _():
        x1_copy().start()
        o_ref[...] = _dot(hidden(), wd_ref[...])

    @pl.when((f > 0) & (f < last))
    def _():
        o_ref[...] += _dot(hidden(), wd_ref[...])

    @pl.when(f == last)
    def _():
        x1_copy().wait()
        hid = hidden()
        for r0 in range(0, tm, _DOT_ROWS):
            rows = slice(r0, r0 + _DOT_ROWS)
            o_ref[rows, :] += _dot(hid[rows, :], wd_ref[...])
            for s0 in range(r0, r0 + _DOT_ROWS, _NORM_ROWS):
                finish_rows(s0)


def _ffn(h2, x1, mod3, wg, wu, wd, final_gain, seq, *, final_norm, tm=1024, tf=512):
    t, d = x1.shape
    dff = wg.shape[1]
    assert dff // tf >= 2
    per_batch = seq // tm
    vmem = (2 * _nbytes((tm, d), _BF16) + 6 * _nbytes((d, tf), _BF16) + 3 * _nbytes((tm, d), _F32)
            + 3 * _nbytes((tm, tf), _F32))
    return pl.pallas_call(
        functools.partial(_ffn_kernel, final_norm=final_norm),
        grid=(t // tm, dff // tf),
        in_specs=[
            pl.BlockSpec((tm, d), lambda i, f: (i, 0)),
            pl.BlockSpec((d, tf), lambda i, f: (0, f)),
            pl.BlockSpec((d, tf), lambda i, f: (0, f)),
            pl.BlockSpec((tf, d), lambda i, f: (f, 0)),
            pl.BlockSpec(memory_space=pl.ANY),
            pl.BlockSpec((1, _N_MOD, d), lambda i, f: (i // per_batch, 0, 0)),
            pl.BlockSpec((1, d), lambda i, f: (0, 0)),
        ],
        out_specs=pl.BlockSpec((tm, d), lambda i, f: (i, 0)),
        out_shape=jax.ShapeDtypeStruct((t, d), _F32),
        scratch_shapes=[pltpu.VMEM((tm, d), _F32), pltpu.SemaphoreType.DMA((1,))],
        compiler_params=pltpu.CompilerParams(
            dimension_semantics=("arbitrary", "arbitrary"), vmem_limit_bytes=_vmem_limit(vmem)),
        name="ffn",
    )(h2, wg, wu, wd, x1, mod3, final_gain)


def kernel(x, c, w_ada, b_ada, norm1_g, w_in, pool_w, pool_scale, ret_decay_fwd, ret_decay_bwd,
           w_out, norm2_g, w_gate, w_up, w_down, final_g):
    bsz, seq, d = x.shape
    depth = w_ada.shape[0]
    heads = _RET_HEADS
    pool_width = pool_scale.shape[1]
    dh = (w_in.shape[2] - pool_width) // (4 * heads)
    assert pool_width == len(_POOL_WINDOWS) * dh and seq % _RET_CHUNK == 0 and dh == _RET_CHUNK
    assert all(w & (w - 1) == 0 and w // 2 <= _POOL_EDGE_ROWS for w in _POOL_WINDOWS)

    inv = 1.0 / (_ROPE_BASE ** jnp.linspace(0.0, 1.0, dh // 2, dtype=_F32))
    ang = jnp.arange(seq, dtype=_F32)[:, None] * inv[None, :]
    cos, sin = jnp.cos(ang), jnp.sin(ang)

    xf = x.reshape(bsz * seq, d)
    for l in range(depth):
        mod3 = _adaln(c, w_ada[l], b_ada[l]).reshape(bsz, _N_MOD, d)
        proj, wg, wu = _inproj(xf, mod3, norm1_g[l].reshape(1, d), w_in[l].astype(_BF16), seq,
                               [w_gate[l], w_up[l]])
        spread = lambda p: jnp.broadcast_to(p.astype(_F32)[:, None, None], (heads, 1, _RET_CHUNK))
        pool_out, ret_out, wo, wd = _mixer(
            proj, cos, sin, pool_w[l].astype(_BF16), pool_scale[l].reshape(len(_POOL_WINDOWS), 1, dh),
            spread(ret_decay_fwd[l]), spread(ret_decay_bwd[l]), bsz, seq, [w_out[l], w_down[l]])
        x1, h2 = _outproj(pool_out, ret_out, xf, mod3, norm2_g[l].reshape(1, d), wo, seq)
        xf = _ffn(h2, x1, mod3, wg, wu, wd, final_g.reshape(1, d), seq, final_norm=(l == depth - 1))
    return xf.reshape(bsz, seq, d)
```

```python
import functools

import jax
import jax.numpy as jnp
from jax import lax
from jax.experimental import pallas as pl
from jax.experimental.pallas import tpu as pltpu

_EPS = 1e-6
_POOL_WINDOWS = (2, 4, 8, 16)
_RET_HEADS = 4
_ROPE_BASE = 10000.0
_N_MOD = 6

_V7X_VMEM_BYTES = 64 * 1024 * 1024
_LANES = 128
_BF16_ROWS = 16
_RET_CHUNK = 256
_POOL_HALO = 128
_POOL_EDGE_ROWS = 8
_NORM_ROWS = 128
_DOT_ROWS = 256

_F32 = jnp.float32
_BF16 = jnp.bfloat16


def _vmem_limit(nbytes):
    return int(min(_V7X_VMEM_BYTES - (2 << 20), nbytes + (8 << 20)))


def _nbytes(shape, dtype):
    n = 1
    for s in shape:
        n *= s
    return n * jnp.dtype(dtype).itemsize


def _dot(a, b):
    return jnp.dot(a, b, preferred_element_type=_F32)


def _silu(x):
    return x * jax.nn.sigmoid(x)


def _rms_rows(x):
    return x * lax.rsqrt(jnp.mean(x * x, axis=-1, keepdims=True) + _EPS)


def _col_blocks(d):
    return [slice(c0, c0 + _LANES) for c0 in range(0, d, _LANES)]


def _rmsnorm_rows(src_ref, dst_ref, row0, nrows, gain, shift=None):
    d = src_ref.shape[1]
    for r0 in range(row0, row0 + nrows, _NORM_ROWS):
        rows = slice(r0, r0 + _NORM_ROWS)
        ss = None
        for cols in _col_blocks(d):
            xc = src_ref[rows, cols]
            ss = xc * xc if ss is None else ss + xc * xc
        rs = lax.rsqrt(jnp.sum(ss, axis=-1, keepdims=True) / d + _EPS)
        for cols in _col_blocks(d):
            y = src_ref[rows, cols] * rs * gain[:, cols]
            if shift is not None:
                y = y + shift[:, cols]
            dst_ref[rows, cols] = y.astype(dst_ref.dtype)


def _adaln_kernel(c_ref, w_ref, b_ref, o_ref):
    c = c_ref[...]
    o_ref[...] = _dot(_silu(c).astype(_BF16), w_ref[...].astype(_BF16)) + b_ref[...]


def _adaln(c, w, b, *, tn=1024):
    bsz, d = c.shape
    n = w.shape[1]
    vmem = 2 * _nbytes((d, tn), _F32) + _nbytes((d, tn), _BF16) + 4 * _nbytes((bsz, d), _F32)
    return pl.pallas_call(
        _adaln_kernel,
        grid=(n // tn,),
        in_specs=[
            pl.BlockSpec((bsz, d), lambda j: (0, 0)),
            pl.BlockSpec((d, tn), lambda j: (0, j)),
            pl.BlockSpec((1, tn), lambda j: (0, j)),
        ],
        out_specs=pl.BlockSpec((bsz, tn), lambda j: (0, j)),
        out_shape=jax.ShapeDtypeStruct((bsz, n), _F32),
        compiler_params=pltpu.CompilerParams(
            dimension_semantics=("arbitrary",), vmem_limit_bytes=_vmem_limit(vmem)),
        name="adaln",
    )(c, w, b.reshape(1, n))


def _slab_specs(weights, steps, index_map):
    specs, shapes, nbytes = [], [], 0
    for w in weights:
        rows, cols = w.shape
        assert rows % steps == 0 and (rows // steps) % _BF16_ROWS == 0
        specs.append(pl.BlockSpec((rows // steps, cols), index_map))
        shapes.append(jax.ShapeDtypeStruct(w.shape, _BF16))
        nbytes += 2 * _nbytes((rows // steps, cols), _F32) + 2 * _nbytes((rows // steps, cols), _BF16)
    return specs, shapes, nbytes


def _cast_slabs(src_refs, dst_refs):
    for src, dst in zip(src_refs, dst_refs):
        dst[...] = src[...].astype(dst.dtype)


def _inproj_kernel(x_ref, mod_ref, gain_ref, cos_ref, sin_ref, w_ref, *rest, n_cast, pool_width, heads):
    cast_in, (o_ref, *cast_out), h_ref = rest[:n_cast], rest[n_cast:2 * n_cast + 1], rest[-1]
    width = (o_ref.shape[1] - pool_width) // 4
    dh = width // heads
    half = dh // 2
    scale = gain_ref[...] * (1.0 + mod_ref[0, 1:2, :])
    shift = mod_ref[0, 0:1, :]
    for r0 in range(0, x_ref.shape[0], _DOT_ROWS):
        rows = slice(r0, r0 + _DOT_ROWS)
        _rmsnorm_rows(x_ref, h_ref, r0, _DOT_ROWS, scale, shift)
        h = h_ref[rows, :]
        o_ref[rows, 0:pool_width] = _dot(h, w_ref[:, 0:pool_width]).astype(o_ref.dtype)
        cos, sin = cos_ref[rows, :], sin_ref[rows, :]
        for part in range(2):
            c0 = pool_width + part * width
            t = _dot(h, w_ref[:, c0:c0 + width])
            for hd in range(heads):
                t1 = t[:, hd * dh:hd * dh + half]
                t2 = t[:, hd * dh + half:(hd + 1) * dh]
                o_ref[rows, c0 + hd * dh:c0 + hd * dh + half] = (t1 * cos - t2 * sin).astype(o_ref.dtype)
                o_ref[rows, c0 + hd * dh + half:c0 + (hd + 1) * dh] = (t1 * sin + t2 * cos).astype(o_ref.dtype)
        c0 = pool_width + 2 * width
        o_ref[rows, c0:c0 + width] = _dot(h, w_ref[:, c0:c0 + width]).astype(o_ref.dtype)
        c0 = pool_width + 3 * width
        o_ref[rows, c0:c0 + width] = _silu(_dot(h, w_ref[:, c0:c0 + width])).astype(o_ref.dtype)
    _cast_slabs(cast_in, cast_out)


def _inproj(x2d, mod3, gain, cos, sin, w, seq, pool_width, cast_weights, *, tm=512):
    t, d = x2d.shape
    n = w.shape[1]
    per_batch = seq // tm
    cast_specs, cast_shapes, cast_bytes = _slab_specs(cast_weights, t // tm, lambda i: (i, 0))
    vmem = (2 * _nbytes((tm, d), _F32) + _nbytes((d, n), _BF16) + 2 * _nbytes((tm, n), _BF16)
            + _nbytes((tm, d), _BF16) + _nbytes((_DOT_ROWS, n), _F32) + 4 * _nbytes(cos.shape, _F32) // per_batch
            + cast_bytes)
    return pl.pallas_call(
        functools.partial(_inproj_kernel, n_cast=len(cast_weights), pool_width=pool_width, heads=_RET_HEADS),
        grid=(t // tm,),
        in_specs=[
            pl.BlockSpec((tm, d), lambda i: (i, 0)),
            pl.BlockSpec((1, _N_MOD, d), lambda i: (i // per_batch, 0, 0)),
            pl.BlockSpec((1, d), lambda i: (0, 0)),
            pl.BlockSpec((tm, cos.shape[1]), lambda i: (i % per_batch, 0)),
            pl.BlockSpec((tm, sin.shape[1]), lambda i: (i % per_batch, 0)),
            pl.BlockSpec((d, n), lambda i: (0, 0), pipeline_mode=pl.Buffered(1)),
        ] + cast_specs,
        out_specs=[pl.BlockSpec((tm, n), lambda i: (i, 0))] + cast_specs,
        out_shape=[jax.ShapeDtypeStruct((t, n), _BF16)] + cast_shapes,
        scratch_shapes=[pltpu.VMEM((tm, d), _BF16)],
        compiler_params=pltpu.CompilerParams(
            dimension_semantics=("arbitrary",), vmem_limit_bytes=_vmem_limit(vmem)),
        name="inproj",
    )(x2d, mod3, gain, cos, sin, w, *cast_weights)


def _pool_chunks(u_ref, pad_ref, group, seq):
    c = _RET_CHUNK
    n = seq // c
    halo = _POOL_HALO
    half = jnp.int32(0)
    inv_w = jnp.float32(0)
    for gi, window in enumerate(_POOL_WINDOWS):
        half = jnp.where(group == gi, jnp.int32(window // 2), half)
        inv_w = jnp.where(group == gi, jnp.float32(1.0 / window), inv_w)

    zeros = jnp.zeros((halo, pad_ref.shape[1]), pad_ref.dtype)
    pad_ref[0:halo, :] = zeros
    pad_ref[halo + seq:, :] = zeros
    pad_ref[halo:halo + seq, :] = u_ref[...]

    rel = (lax.broadcasted_iota(jnp.int32, (c, c + 2 * halo), 1) - halo
           - lax.broadcasted_iota(jnp.int32, (c, c + 2 * halo), 0))
    band = (jnp.where((rel >= -half) & (rel < half), inv_w, 0.0)
            - jnp.where(rel == 0, 1.0, 0.0)).astype(_BF16)

    edge = _POOL_EDGE_ROWS
    t_lo = lax.broadcasted_iota(jnp.int32, (edge, pad_ref.shape[1]), 0)
    t_hi = t_lo + (seq - edge)
    width = (2 * half).astype(_F32)
    ratio_lo = width / (t_lo + half - jnp.maximum(t_lo - half, 0)).astype(_F32)
    ratio_hi = width / (jnp.minimum(t_hi + half, seq) - (t_hi - half)).astype(_F32)

    def pooled_chunk(ci):
        pooled = _dot(band, pad_ref[ci * c:ci * c + c + 2 * halo, :])
        if ci == 0:
            u_lo = u_ref[0:edge, :].astype(_F32)
            pooled = jnp.concatenate([(pooled[0:edge] + u_lo) * ratio_lo - u_lo, pooled[edge:]], axis=0)
        if ci == n - 1:
            u_hi = u_ref[seq - edge:seq, :].astype(_F32)
            pooled = jnp.concatenate([pooled[:c - edge], (pooled[c - edge:] + u_hi) * ratio_hi - u_hi], axis=0)
        return pooled.astype(_BF16)

    return pooled_chunk


def _mixer_kernel(u_ref, q_ref, k_ref, v_ref, gate_ref, pw_ref, ps_ref, df_ref, db_ref, *rest, n_cast):
    cast_in, (pool_ref, ret_ref, *cast_out) = rest[:n_cast], rest[n_cast:2 * n_cast + 2]
    pad_ref, pooled_ref, kb_ref, acc_ref, dec_ref, sf_ref, sb_ref = rest[2 * n_cast + 2:]
    _cast_slabs(cast_in, cast_out)
    seq, dh = q_ref.shape
    c = _RET_CHUNK
    n = seq // c

    pooled_chunk = _pool_chunks(u_ref, pad_ref, pl.program_id(1), seq)

    lg_f = -jnp.exp(df_ref[0])
    lg_b = -jnp.exp(db_ref[0])
    ri = lax.broadcasted_iota(jnp.int32, (c, c), 0).astype(_F32)
    cj = lax.broadcasted_iota(jnp.int32, (c, c), 1).astype(_F32)
    diff = ri - cj
    qk_scale = dh ** -0.5
    dec_ref[0] = jnp.exp(jnp.where(diff >= 0, diff * lg_f, -diff * lg_b)) * qk_scale
    dec_ref[1] = jnp.exp((ri + 1.0) * lg_f) * qk_scale
    dec_ref[2] = jnp.exp((c - 1.0 - ri) * lg_f)
    dec_ref[3] = jnp.exp((c - ri) * lg_b) * qk_scale
    dec_ref[4] = jnp.exp(ri * lg_b)
    chunk_f = jnp.exp(c * lg_f)
    chunk_b = jnp.exp(c * lg_b)

    def tn_dot(a, b):
        return lax.dot_general(a, b, (((0,), (0,)), ((), ())), preferred_element_type=_F32)

    def nt_dot(a, b):
        return lax.dot_general(a, b, (((1,), (1,)), ((), ())), preferred_element_type=_F32)

    for ci in range(n):
        rows = slice(ci * c, (ci + 1) * c)
        q = q_ref[rows, :]
        k = k_ref[rows, :]
        k32 = k.astype(_F32)
        if ci > 0:
            kb_ref[rows, :] = (k32 * dec_ref[4]).astype(_BF16)
        v = v_ref[rows, :]
        scores = nt_dot(q, k) * dec_ref[0]
        out = _dot(scores.astype(_BF16), v)
        if ci > 0:
            out = out + _dot(q, sf_ref[...].astype(_BF16)) * dec_ref[1]
        acc_ref[rows, :] = out
        if ci < n - 1:
            upd = tn_dot((k32 * dec_ref[2]).astype(_BF16), v)
            sf_ref[...] = upd if ci == 0 else sf_ref[...] * chunk_f + upd
        pooled_ref[rows, :] = pooled_chunk(ci)

    pool_ref[...] = (_dot(pooled_ref[...], pw_ref[0]) * ps_ref[0]).astype(pool_ref.dtype)

    for ci in reversed(range(n)):
        rows = slice(ci * c, (ci + 1) * c)
        out = acc_ref[rows, :]
        if ci < n - 1:
            out = out + _dot(q_ref[rows, :], sb_ref[...].astype(_BF16)) * dec_ref[3]
        ret = _rms_rows(out)
        ret_ref[rows, :] = (gate_ref[rows, :].astype(_F32) * ret).astype(ret_ref.dtype)
        if ci > 0:
            upd = tn_dot(kb_ref[rows, :], v_ref[rows, :])
            sb_ref[...] = upd if ci == n - 1 else sb_ref[...] * chunk_b + upd


def _mixer(proj, pool_w, pool_scale, dec_f, dec_b, bsz, seq, cast_weights):
    t = proj.shape[0]
    heads = _RET_HEADS
    dh = pool_w.shape[-1]
    pool_blocks = len(_POOL_WINDOWS)
    col = lambda first: (lambda b, h: (b, first + h))
    seq_block = lambda first: pl.BlockSpec((seq, dh), col(first))
    per_head = lambda shape: pl.BlockSpec((1,) + shape, lambda b, h: (h, 0, 0))
    scratch = [
        pltpu.VMEM((seq + 2 * _POOL_HALO, dh), _BF16),
        pltpu.VMEM((seq, dh), _BF16),
        pltpu.VMEM((seq, dh), _BF16),
        pltpu.VMEM((seq, dh), _F32),
        pltpu.VMEM((5, _RET_CHUNK, _RET_CHUNK), _F32),
        pltpu.VMEM((dh, dh), _F32),
        pltpu.VMEM((dh, dh), _F32),
    ]
    cast_specs, cast_shapes, cast_bytes = _slab_specs(
        cast_weights, bsz * heads, lambda b, h: (b * heads + h, 0))
    vmem = (2 * 7 * _nbytes((seq, dh), _BF16) + 4 * _nbytes((seq, dh), _F32) + 8 * _nbytes((dh, dh), _F32)
            + cast_bytes)
    return pl.pallas_call(
        functools.partial(_mixer_kernel, n_cast=len(cast_weights)),
        grid=(bsz, heads),
        in_specs=[
            seq_block(0),
            seq_block(pool_blocks),
            seq_block(pool_blocks + heads),
            seq_block(pool_blocks + 2 * heads),
            seq_block(pool_blocks + 3 * heads),
            per_head((dh, dh)),
            per_head((1, dh)),
            per_head((1, _RET_CHUNK)), per_head((1, _RET_CHUNK)),
        ] + cast_specs,
        out_specs=[pl.BlockSpec((seq, dh), lambda b, h: (b, h)),
                   pl.BlockSpec((seq, dh), lambda b, h: (b, h))] + cast_specs,
        out_shape=[jax.ShapeDtypeStruct((t, pool_blocks * dh), _BF16),
                   jax.ShapeDtypeStruct((t, heads * dh), _BF16)] + cast_shapes,
        scratch_shapes=scratch,
        compiler_params=pltpu.CompilerParams(
            dimension_semantics=("arbitrary", "arbitrary"), vmem_limit_bytes=_vmem_limit(vmem)),
        name="mixer",
    )(proj, proj, proj, proj, proj, pool_w, pool_scale, dec_f, dec_b, *cast_weights)


def _outproj_kernel(p_ref, r_ref, x_ref, mod_ref, gain_ref, w_ref, x1_ref, h2_ref):
    kp = p_ref.shape[1]
    gate = mod_ref[0, 2:3, :]
    scale = gain_ref[...] * (1.0 + mod_ref[0, 4:5, :])
    shift = mod_ref[0, 3:4, :]
    for r0 in range(0, x_ref.shape[0], _DOT_ROWS):
        rows = slice(r0, r0 + _DOT_ROWS)
        mixed = _dot(p_ref[rows, :], w_ref[0:kp, :]) + _dot(r_ref[rows, :], w_ref[kp:, :])
        x1_ref[rows, :] = x_ref[rows, :] + gate * mixed
        _rmsnorm_rows(x1_ref, h2_ref, r0, _DOT_ROWS, scale, shift)


def _outproj(pool_out, ret_out, x2d, mod3, gain, w, seq, *, tm=512):
    t, d = x2d.shape
    kp, kr = pool_out.shape[1], ret_out.shape[1]
    per_batch = seq // tm
    vmem = (2 * _nbytes((tm, kp + kr), _BF16) + 4 * _nbytes((tm, d), _F32) + 2 * _nbytes((tm, d), _BF16)
            + _nbytes((kp + kr, d), _BF16) + _nbytes((_DOT_ROWS, d), _F32))
    return pl.pallas_call(
        _outproj_kernel,
        grid=(t // tm,),
        in_specs=[
            pl.BlockSpec((tm, kp), lambda i: (i, 0)),
            pl.BlockSpec((tm, kr), lambda i: (i, 0)),
            pl.BlockSpec((tm, d), lambda i: (i, 0)),
            pl.BlockSpec((1, _N_MOD, d), lambda i: (i // per_batch, 0, 0)),
            pl.BlockSpec((1, d), lambda i: (0, 0)),
            pl.BlockSpec((kp + kr, d), lambda i: (0, 0), pipeline_mode=pl.Buffered(1)),
        ],
        out_specs=[pl.BlockSpec((tm, d), lambda i: (i, 0)),
                   pl.BlockSpec((tm, d), lambda i: (i, 0))],
        out_shape=[jax.ShapeDtypeStruct((t, d), _F32),
                   jax.ShapeDtypeStruct((t, d), _BF16)],
        compiler_params=pltpu.CompilerParams(
            dimension_semantics=("arbitrary",), vmem_limit_bytes=_vmem_limit(vmem)),
        name="outproj",
    )(pool_out, ret_out, x2d, mod3, gain, w)


def _ffn_kernel(h2_ref, wg_ref, wu_ref, wd_ref, x1_hbm, mod_ref, gain_ref, o_ref, x1_buf, sem,
                *, final_norm):
    i, f = pl.program_id(0), pl.program_id(1)
    tm, d = o_ref.shape

    def x1_copy():
        return pltpu.make_async_copy(x1_hbm.at[pl.ds(pl.multiple_of(i * tm, tm), tm), :], x1_buf, sem.at[0])

    last = pl.num_programs(1) - 1

    def hidden():
        h2 = h2_ref[...]
        return (_silu(_dot(h2, wg_ref[...])) * _dot(h2, wu_ref[...])).astype(_BF16)

    def finish_rows(r0):
        rows = slice(r0, r0 + _NORM_ROWS)
        gate = mod_ref[0, 5:6, :]
        gain = gain_ref[...]
        ss = None
        for cols in _col_blocks(d):
            x2 = x1_buf[rows, cols] + gate[:, cols] * o_ref[rows, cols]
            o_ref[rows, cols] = x2
            ss = x2 * x2 if ss is None else ss + x2 * x2
        if final_norm:
            rs = lax.rsqrt(jnp.sum(ss, axis=-1, keepdims=True) / d + _EPS)
            for cols in _col_blocks(d):
                o_ref[rows, cols] = o_ref[rows, cols] * rs * gain[:, cols]

    @pl.when(f == 0)
    def _():
        x1_copy().start()
        o_ref[...] = _dot(hidden(), wd_ref[...])

    @pl.when((f > 0) & (f < last))
    def _():
        o_ref[...] += _dot(hidden(), wd_ref[...])

    @pl.when(f == last)
    def _():
        x1_copy().wait()
        hid = hidden()
        for r0 in range(0, tm, _DOT_ROWS):
            rows = slice(r0, r0 + _DOT_ROWS)
            o_ref[rows, :] += _dot(hid[rows, :], wd_ref[...])
            for s0 in range(r0, r0 + _DOT_ROWS, _NORM_ROWS):
                finish_rows(s0)


def _ffn(h2, x1, mod3, wg, wu, wd, final_gain, seq, *, final_norm, tm=1024, tf=512):
    t, d = x1.shape
    dff = wg.shape[1]
    assert dff // tf >= 2
    per_batch = seq // tm
    vmem = (2 * _nbytes((tm, d), _BF16) + 6 * _nbytes((d, tf), _BF16) + 3 * _nbytes((tm, d), _F32)
            + 3 * _nbytes((tm, tf), _F32))
    return pl.pallas_call(
        functools.partial(_ffn_kernel, final_norm=final_norm),
        grid=(t // tm, dff // tf),
        in_specs=[
            pl.BlockSpec((tm, d), lambda i, f: (i, 0)),
            pl.BlockSpec((d, tf), lambda i, f: (0, f)),
            pl.BlockSpec((d, tf), lambda i, f: (0, f)),
            pl.BlockSpec((tf, d), lambda i, f: (f, 0)),
            pl.BlockSpec(memory_space=pl.ANY),
            pl.BlockSpec((1, _N_MOD, d), lambda i, f: (i // per_batch, 0, 0)),
            pl.BlockSpec((1, d), lambda i, f: (0, 0)),
        ],
        out_specs=pl.BlockSpec((tm, d), lambda i, f: (i, 0)),
        out_shape=jax.ShapeDtypeStruct((t, d), _F32),
        scratch_shapes=[pltpu.VMEM((tm, d), _F32), pltpu.SemaphoreType.DMA((1,))],
        compiler_params=pltpu.CompilerParams(
            dimension_semantics=("arbitrary", "arbitrary"), vmem_limit_bytes=_vmem_limit(vmem)),
        name="ffn",
    )(h2, wg, wu, wd, x1, mod3, final_gain)


def kernel(x, c, w_ada, b_ada, norm1_g, w_in, pool_w, pool_scale, ret_decay_fwd, ret_decay_bwd,
           w_out, norm2_g, w_gate, w_up, w_down, final_g):
    bsz, seq, d = x.shape
    depth = w_ada.shape[0]
    heads = _RET_HEADS
    pool_width = pool_scale.shape[1]
    dh = (w_in.shape[2] - pool_width) // (4 * heads)
    assert pool_width == len(_POOL_WINDOWS) * dh and seq % _RET_CHUNK == 0 and dh == _RET_CHUNK
    assert all(w & (w - 1) == 0 and w // 2 <= _POOL_EDGE_ROWS for w in _POOL_WINDOWS)

    inv = 1.0 / (_ROPE_BASE ** jnp.linspace(0.0, 1.0, dh // 2, dtype=_F32))
    ang = jnp.arange(seq, dtype=_F32)[:, None] * inv[None, :]
    cos, sin = jnp.cos(ang), jnp.sin(ang)

    xf = x.reshape(bsz * seq, d)
    for l in range(depth):
        mod3 = _adaln(c, w_ada[l], b_ada[l]).reshape(bsz, _N_MOD, d)
        proj, wo, wg, wu, wd = _inproj(xf, mod3, norm1_g[l].reshape(1, d), cos, sin, w_in[l].astype(_BF16), seq,
                                       pool_width, [w_out[l], w_gate[l], w_up[l], w_down[l]])
        spread = lambda p: jnp.broadcast_to(p.astype(_F32)[:, None, None], (heads, 1, _RET_CHUNK))
        pool_out, ret_out = _mixer(
            proj, pool_w[l].astype(_BF16), pool_scale[l].reshape(len(_POOL_WINDOWS), 1, dh),
            spread(ret_decay_fwd[l]), spread(ret_decay_bwd[l]), bsz, seq, [])
        x1, h2 = _outproj(pool_out, ret_out, xf, mod3, norm2_g[l].reshape(1, d), wo, seq)
        xf = _ffn(h2, x1, mod3, wg, wu, wd, final_g.reshape(1, d), seq, final_norm=(l == depth - 1))
    return xf.reshape(bsz, seq, d)
```

```python
import functools

import jax
import jax.numpy as jnp
import numpy as np
from jax import lax
from jax.experimental import pallas as pl
from jax.experimental.pallas import tpu as pltpu

_EPS = 1e-6
_POOL_WINDOWS = (2, 4, 8, 16)
_RET_HEADS = 4
_ROPE_BASE = 10000.0
_N_MOD = 6

_V7X_VMEM_BYTES = 64 * 1024 * 1024
_LANES = 128
_BF16_ROWS = 16
_RET_CHUNK = 256
_POOL_HALO = 128
_POOL_EDGE_ROWS = 8
_NORM_ROWS = 128
_DOT_ROWS = 256

_F32 = jnp.float32
_BF16 = jnp.bfloat16


def _vmem_limit(nbytes):
    return int(min(_V7X_VMEM_BYTES - (2 << 20), nbytes + (8 << 20)))


def _nbytes(shape, dtype):
    n = 1
    for s in shape:
        n *= s
    return n * jnp.dtype(dtype).itemsize


def _dot(a, b):
    return jnp.dot(a, b, preferred_element_type=_F32)


def _silu(x):
    return x * jax.nn.sigmoid(x)


def _rms_rows(x):
    return x * lax.rsqrt(jnp.mean(x * x, axis=-1, keepdims=True) + _EPS)


def _col_blocks(d):
    return [slice(c0, c0 + _LANES) for c0 in range(0, d, _LANES)]


def _rmsnorm_rows(src_ref, dst_ref, row0, nrows, gain, shift=None, *, dst_row0=None):
    d = src_ref.shape[1]
    dst_shift = 0 if dst_row0 is None else dst_row0 - row0
    for r0 in range(row0, row0 + nrows, _NORM_ROWS):
        rows = slice(r0, r0 + _NORM_ROWS)
        dst_rows = slice(r0 + dst_shift, r0 + dst_shift + _NORM_ROWS)
        ss = None
        for cols in _col_blocks(d):
            xc = src_ref[rows, cols]
            ss = xc * xc if ss is None else ss + xc * xc
        rs = lax.rsqrt(jnp.sum(ss, axis=-1, keepdims=True) / d + _EPS)
        for cols in _col_blocks(d):
            y = src_ref[rows, cols] * rs * gain[:, cols]
            if shift is not None:
                y = y + shift[:, cols]
            dst_ref[dst_rows, cols] = y.astype(dst_ref.dtype)


def _adaln_kernel(c_ref, w_ref, b_ref, o_ref):
    c = c_ref[...]
    o_ref[...] = _dot(_silu(c).astype(_BF16), w_ref[...].astype(_BF16)) + b_ref[...]


def _adaln(c, w, b, *, tn=1024):
    bsz, d = c.shape
    n = w.shape[1]
    vmem = 2 * _nbytes((d, tn), _F32) + _nbytes((d, tn), _BF16) + 4 * _nbytes((bsz, d), _F32)
    return pl.pallas_call(
        _adaln_kernel,
        grid=(n // tn,),
        in_specs=[
            pl.BlockSpec((bsz, d), lambda j: (0, 0)),
            pl.BlockSpec((d, tn), lambda j: (0, j)),
            pl.BlockSpec((1, tn), lambda j: (0, j)),
        ],
        out_specs=pl.BlockSpec((bsz, tn), lambda j: (0, j)),
        out_shape=jax.ShapeDtypeStruct((bsz, n), _F32),
        compiler_params=pltpu.CompilerParams(
            dimension_semantics=("arbitrary",), vmem_limit_bytes=_vmem_limit(vmem)),
        name="adaln",
    )(c, w, b.reshape(1, n))


def _slab_specs(weights, steps, index_map):
    specs, shapes, nbytes = [], [], 0
    for w in weights:
        rows, cols = w.shape
        assert rows % steps == 0 and (rows // steps) % _BF16_ROWS == 0
        specs.append(pl.BlockSpec((rows // steps, cols), index_map))
        shapes.append(jax.ShapeDtypeStruct(w.shape, _BF16))
        nbytes += 2 * _nbytes((rows // steps, cols), _F32) + 2 * _nbytes((rows // steps, cols), _BF16)
    return specs, shapes, nbytes


def _cast_slabs(src_refs, dst_refs):
    for src, dst in zip(src_refs, dst_refs):
        dst[...] = src[...].astype(dst.dtype)


def _inproj_kernel(x_ref, mod_ref, gain_ref, cos_ref, sin_ref, w_ref, *rest, n_cast, pool_width, heads):
    cast_in, (o_ref, *cast_out), h_ref = rest[:n_cast], rest[n_cast:2 * n_cast + 1], rest[-1]
    width = (o_ref.shape[1] - pool_width) // 4
    dh = width // heads
    half = dh // 2
    scale = gain_ref[...] * (1.0 + mod_ref[0, 1:2, :])
    shift = mod_ref[0, 0:1, :]
    for r0 in range(0, x_ref.shape[0], _DOT_ROWS):
        rows = slice(r0, r0 + _DOT_ROWS)
        _rmsnorm_rows(x_ref, h_ref, r0, _DOT_ROWS, scale, shift)
        h = h_ref[rows, :]
        c0 = pool_width + 3 * width
        o_ref[rows, c0:c0 + width] = _silu(_dot(h, w_ref[:, c0:c0 + width])).astype(o_ref.dtype)
        cos, sin = cos_ref[rows, :], sin_ref[rows, :]
        for part in range(2):
            c0 = pool_width + part * width
            t = _dot(h, w_ref[:, c0:c0 + width])
            for hd in range(heads):
                t1 = t[:, hd * dh:hd * dh + half]
                t2 = t[:, hd * dh + half:(hd + 1) * dh]
                o_ref[rows, c0 + hd * dh:c0 + hd * dh + half] = (t1 * cos - t2 * sin).astype(o_ref.dtype)
                o_ref[rows, c0 + hd * dh + half:c0 + (hd + 1) * dh] = (t1 * sin + t2 * cos).astype(o_ref.dtype)
        if r0 == 0:
            _cast_slabs(cast_in, cast_out)
        c0 = pool_width + 2 * width
        o_ref[rows, c0:c0 + width] = _dot(h, w_ref[:, c0:c0 + width]).astype(o_ref.dtype)
        o_ref[rows, 0:pool_width] = _dot(h, w_ref[:, 0:pool_width]).astype(o_ref.dtype)


def _inproj(x2d, mod3, gain, cos, sin, w, seq, pool_width, cast_weights, *, tm=512):
    t, d = x2d.shape
    n = w.shape[1]
    per_batch = seq // tm
    cast_specs, cast_shapes, cast_bytes = _slab_specs(cast_weights, t // tm, lambda i: (i, 0))
    vmem = (2 * _nbytes((tm, d), _F32) + _nbytes((d, n), _BF16) + 2 * _nbytes((tm, n), _BF16)
            + _nbytes((tm, d), _BF16) + _nbytes((_DOT_ROWS, n), _F32) + 4 * _nbytes(cos.shape, _F32) // per_batch
            + cast_bytes)
    return pl.pallas_call(
        functools.partial(_inproj_kernel, n_cast=len(cast_weights), pool_width=pool_width, heads=_RET_HEADS),
        grid=(t // tm,),
        in_specs=[
            pl.BlockSpec((tm, d), lambda i: (i, 0)),
            pl.BlockSpec((1, _N_MOD, d), lambda i: (i // per_batch, 0, 0)),
            pl.BlockSpec((1, d), lambda i: (0, 0)),
            pl.BlockSpec((tm, cos.shape[1]), lambda i: (i % per_batch, 0)),
            pl.BlockSpec((tm, sin.shape[1]), lambda i: (i % per_batch, 0)),
            pl.BlockSpec((d, n), lambda i: (0, 0), pipeline_mode=pl.Buffered(1)),
        ] + cast_specs,
        out_specs=[pl.BlockSpec((tm, n), lambda i: (i, 0))] + cast_specs,
        out_shape=[jax.ShapeDtypeStruct((t, n), _BF16)] + cast_shapes,
        scratch_shapes=[pltpu.VMEM((tm, d), _BF16)],
        compiler_params=pltpu.CompilerParams(
            dimension_semantics=("arbitrary",), vmem_limit_bytes=_vmem_limit(vmem)),
        name="inproj",
    )(x2d, mod3, gain, cos, sin, w, *cast_weights)


def _pool_chunks(u_ref, pad_ref, group, seq):
    c = _RET_CHUNK
    n = seq // c
    halo = _POOL_HALO
    half = jnp.int32(0)
    inv_w = jnp.float32(0)
    for gi, window in enumerate(_POOL_WINDOWS):
        half = jnp.where(group == gi, jnp.int32(window // 2), half)
        inv_w = jnp.where(group == gi, jnp.float32(1.0 / window), inv_w)

    zeros = jnp.zeros((halo, pad_ref.shape[1]), pad_ref.dtype)
    pad_ref[0:halo, :] = zeros
    pad_ref[halo + seq:, :] = zeros
    pad_ref[halo:halo + seq, :] = u_ref[...]

    rel = (lax.broadcasted_iota(jnp.int32, (c, c + 2 * halo), 1) - halo
           - lax.broadcasted_iota(jnp.int32, (c, c + 2 * halo), 0))
    band = (jnp.where((rel >= -half) & (rel < half), inv_w, 0.0)
            - jnp.where(rel == 0, 1.0, 0.0)).astype(_BF16)

    edge = _POOL_EDGE_ROWS
    t_lo = lax.broadcasted_iota(jnp.int32, (edge, pad_ref.shape[1]), 0)
    t_hi = t_lo + (seq - edge)
    width = (2 * half).astype(_F32)
    ratio_lo = width / (t_lo + half - jnp.maximum(t_lo - half, 0)).astype(_F32)
    ratio_hi = width / (jnp.minimum(t_hi + half, seq) - (t_hi - half)).astype(_F32)

    def pooled_chunk(ci):
        pooled = _dot(band, pad_ref[ci * c:ci * c + c + 2 * halo, :])
        if ci == 0:
            u_lo = u_ref[0:edge, :].astype(_F32)
            pooled = jnp.concatenate([(pooled[0:edge] + u_lo) * ratio_lo - u_lo, pooled[edge:]], axis=0)
        if ci == n - 1:
            u_hi = u_ref[seq - edge:seq, :].astype(_F32)
            pooled = jnp.concatenate([pooled[:c - edge], (pooled[c - edge:] + u_hi) * ratio_hi - u_hi], axis=0)
        return pooled.astype(_BF16)

    return pooled_chunk


def _mixer_kernel(u_ref, q_ref, k_ref, v_ref, gate_ref, pw_ref, ps_ref, df_ref, db_ref, pool_ref, ret_ref,
                  pad_ref, pooled_ref, kb_ref, acc_ref, dec_ref, sf_ref, sb_ref):
    seq, dh = q_ref.shape
    c = _RET_CHUNK
    n = seq // c

    pooled_chunk = _pool_chunks(u_ref, pad_ref, pl.program_id(1), seq)

    lg_f = -jnp.exp(df_ref[0])
    lg_b = -jnp.exp(db_ref[0])
    ri = lax.broadcasted_iota(jnp.int32, (c, c), 0).astype(_F32)
    cj = lax.broadcasted_iota(jnp.int32, (c, c), 1).astype(_F32)
    diff = ri - cj
    qk_scale = dh ** -0.5
    dec_ref[0] = jnp.exp(jnp.where(diff >= 0, diff * lg_f, -diff * lg_b)) * qk_scale
    dec_ref[1] = jnp.exp((ri + 1.0) * lg_f) * qk_scale
    dec_ref[2] = jnp.exp((c - 1.0 - ri) * lg_f)
    dec_ref[3] = jnp.exp((c - ri) * lg_b) * qk_scale
    dec_ref[4] = jnp.exp(ri * lg_b)
    chunk_f = jnp.exp(c * lg_f)
    chunk_b = jnp.exp(c * lg_b)

    def tn_dot(a, b):
        return lax.dot_general(a, b, (((0,), (0,)), ((), ())), preferred_element_type=_F32)

    def nt_dot(a, b):
        return lax.dot_general(a, b, (((1,), (1,)), ((), ())), preferred_element_type=_F32)

    for ci in range(n):
        rows = slice(ci * c, (ci + 1) * c)
        q = q_ref[rows, :]
        k = k_ref[rows, :]
        k32 = k.astype(_F32)
        if ci > 0:
            kb_ref[rows, :] = (k32 * dec_ref[4]).astype(_BF16)
        v = v_ref[rows, :]
        scores = nt_dot(q, k) * dec_ref[0]
        out = _dot(scores.astype(_BF16), v)
        if ci > 0:
            out = out + _dot(q, sf_ref[...].astype(_BF16)) * dec_ref[1]
        acc_ref[rows, :] = out
        if ci < n - 1:
            upd = tn_dot((k32 * dec_ref[2]).astype(_BF16), v)
            sf_ref[...] = upd if ci == 0 else sf_ref[...] * chunk_f + upd
        pooled_ref[rows, :] = pooled_chunk(ci)

    pool_ref[...] = (_dot(pooled_ref[...], pw_ref[0]) * ps_ref[0]).astype(pool_ref.dtype)

    for ci in reversed(range(n)):
        rows = slice(ci * c, (ci + 1) * c)
        out = acc_ref[rows, :]
        if ci < n - 1:
            out = out + _dot(q_ref[rows, :], sb_ref[...].astype(_BF16)) * dec_ref[3]
        ret = _rms_rows(out)
        ret_ref[rows, :] = (gate_ref[rows, :].astype(_F32) * ret).astype(ret_ref.dtype)
        if ci > 0:
            upd = tn_dot(kb_ref[rows, :], v_ref[rows, :])
            sb_ref[...] = upd if ci == n - 1 else sb_ref[...] * chunk_b + upd


def _mixer(proj, pool_w, pool_scale, dec_f, dec_b, bsz, seq):
    t = proj.shape[0]
    heads = _RET_HEADS
    dh = pool_w.shape[-1]
    pool_blocks = len(_POOL_WINDOWS)
    col = lambda first: (lambda b, h: (b, first + h))
    seq_block = lambda first: pl.BlockSpec((seq, dh), col(first))
    per_head = lambda shape: pl.BlockSpec((1,) + shape, lambda b, h: (h, 0, 0))
    scratch = [
        pltpu.VMEM((seq + 2 * _POOL_HALO, dh), _BF16),
        pltpu.VMEM((seq, dh), _BF16),
        pltpu.VMEM((seq, dh), _BF16),
        pltpu.VMEM((seq, dh), _F32),
        pltpu.VMEM((5, _RET_CHUNK, _RET_CHUNK), _F32),
        pltpu.VMEM((dh, dh), _F32),
        pltpu.VMEM((dh, dh), _F32),
    ]
    vmem = 2 * 7 * _nbytes((seq, dh), _BF16) + 4 * _nbytes((seq, dh), _F32) + 8 * _nbytes((dh, dh), _F32)
    return pl.pallas_call(
        _mixer_kernel,
        grid=(bsz, heads),
        in_specs=[
            seq_block(0),
            seq_block(pool_blocks),
            seq_block(pool_blocks + heads),
            seq_block(pool_blocks + 2 * heads),
            seq_block(pool_blocks + 3 * heads),
            per_head((dh, dh)),
            per_head((1, dh)),
            per_head((1, _RET_CHUNK)), per_head((1, _RET_CHUNK)),
        ],
        out_specs=[pl.BlockSpec((seq, dh), lambda b, h: (b, h)),
                   pl.BlockSpec((seq, dh), lambda b, h: (b, h))],
        out_shape=[jax.ShapeDtypeStruct((t, pool_blocks * dh), _BF16),
                   jax.ShapeDtypeStruct((t, heads * dh), _BF16)],
        scratch_shapes=scratch,
        compiler_params=pltpu.CompilerParams(
            dimension_semantics=("arbitrary", "arbitrary"), vmem_limit_bytes=_vmem_limit(vmem)),
        name="mixer",
    )(proj, proj, proj, proj, proj, pool_w, pool_scale, dec_f, dec_b)


def _outproj_kernel(p_ref, r_ref, x_ref, mod_ref, gain_ref, w_ref, x1_ref, h2_ref):
    kp = p_ref.shape[1]
    gate = mod_ref[0, 2:3, :]
    scale = gain_ref[...] * (1.0 + mod_ref[0, 4:5, :])
    shift = mod_ref[0, 3:4, :]
    for r0 in range(0, x_ref.shape[0], _DOT_ROWS):
        rows = slice(r0, r0 + _DOT_ROWS)
        mixed = _dot(p_ref[rows, :], w_ref[0:kp, :]) + _dot(r_ref[rows, :], w_ref[kp:, :])
        x1_ref[rows, :] = x_ref[rows, :] + gate * mixed
    _rmsnorm_rows(x1_ref, h2_ref, 0, x_ref.shape[0], scale, shift)


def _outproj(pool_out, ret_out, x2d, mod3, gain, w, seq, *, tm=512):
    t, d = x2d.shape
    kp, kr = pool_out.shape[1], ret_out.shape[1]
    per_batch = seq // tm
    vmem = (2 * _nbytes((tm, kp + kr), _BF16) + 4 * _nbytes((tm, d), _F32) + 2 * _nbytes((tm, d), _BF16)
            + _nbytes((kp + kr, d), _BF16) + _nbytes((_DOT_ROWS, d), _F32))
    return pl.pallas_call(
        _outproj_kernel,
        grid=(t // tm,),
        in_specs=[
            pl.BlockSpec((tm, kp), lambda i: (i, 0)),
            pl.BlockSpec((tm, kr), lambda i: (i, 0)),
            pl.BlockSpec((tm, d), lambda i: (i, 0)),
            pl.BlockSpec((1, _N_MOD, d), lambda i: (i // per_batch, 0, 0)),
            pl.BlockSpec((1, d), lambda i: (0, 0)),
            pl.BlockSpec((kp + kr, d), lambda i: (0, 0), pipeline_mode=pl.Buffered(1)),
        ],
        out_specs=[pl.BlockSpec((tm, d), lambda i: (i, 0)),
                   pl.BlockSpec((tm, d), lambda i: (i, 0))],
        out_shape=[jax.ShapeDtypeStruct((t, d), _F32),
                   jax.ShapeDtypeStruct((t, d), _BF16)],
        compiler_params=pltpu.CompilerParams(
            dimension_semantics=("arbitrary",), vmem_limit_bytes=_vmem_limit(vmem)),
        name="outproj",
    )(pool_out, ret_out, x2d, mod3, gain, w)


def _ffn_kernel(h2_ref, wg_ref, wu_ref, wd_ref, x1_hbm, mod_ref, gain_ref, o_ref, x1_buf, sem,
                *, final_norm):
    i, f = pl.program_id(0), pl.program_id(1)
    tm, d = o_ref.shape

    def x1_copy():
        return pltpu.make_async_copy(x1_hbm.at[pl.ds(pl.multiple_of(i * tm, tm), tm), :], x1_buf, sem.at[0])

    last = pl.num_programs(1) - 1

    def hidden():
        h2 = h2_ref[...]
        return (_silu(_dot(h2, wg_ref[...])) * _dot(h2, wu_ref[...])).astype(_BF16)

    def finish_rows(r0):
        rows = slice(r0, r0 + _NORM_ROWS)
        gate = mod_ref[0, 5:6, :]
        gain = gain_ref[...]
        ss = None
        for cols in _col_blocks(d):
            x2 = x1_buf[rows, cols] + gate[:, cols] * o_ref[rows, cols]
            o_ref[rows, cols] = x2
            ss = x2 * x2 if ss is None else ss + x2 * x2
        if final_norm:
            rs = lax.rsqrt(jnp.sum(ss, axis=-1, keepdims=True) / d + _EPS)
            for cols in _col_blocks(d):
                o_ref[rows, cols] = o_ref[rows, cols] * rs * gain[:, cols]

    @pl.when(f == 0)
    def _():
        x1_copy().start()
        o_ref[...] = _dot(hidden(), wd_ref[...])

    @pl.when((f > 0) & (f < last))
    def _():
        o_ref[...] += _dot(hidden(), wd_ref[...])

    @pl.when(f == last)
    def _():
        x1_copy().wait()
        hid = hidden()
        for r0 in range(0, tm, _DOT_ROWS):
            rows = slice(r0, r0 + _DOT_ROWS)
            o_ref[rows, :] += _dot(hid[rows, :], wd_ref[...])
            for s0 in range(r0, r0 + _DOT_ROWS, _NORM_ROWS):
                finish_rows(s0)


def _ffn(h2, x1, mod3, wg, wu, wd, final_gain, seq, *, final_norm, tm=1024, tf=512):
    t, d = x1.shape
    dff = wg.shape[1]
    assert dff // tf >= 2
    per_batch = seq // tm
    vmem = (2 * _nbytes((tm, d), _BF16) + 6 * _nbytes((d, tf), _BF16) + 3 * _nbytes((tm, d), _F32)
            + 3 * _nbytes((tm, tf), _F32))
    return pl.pallas_call(
        functools.partial(_ffn_kernel, final_norm=final_norm),
        grid=(t // tm, dff // tf),
        in_specs=[
            pl.BlockSpec((tm, d), lambda i, f: (i, 0)),
            pl.BlockSpec((d, tf), lambda i, f: (0, f)),
            pl.BlockSpec((d, tf), lambda i, f: (0, f)),
            pl.BlockSpec((tf, d), lambda i, f: (f, 0)),
            pl.BlockSpec(memory_space=pl.ANY),
            pl.BlockSpec((1, _N_MOD, d), lambda i, f: (i // per_batch, 0, 0)),
            pl.BlockSpec((1, d), lambda i, f: (0, 0)),
        ],
        out_specs=pl.BlockSpec((tm, d), lambda i, f: (i, 0)),
        out_shape=jax.ShapeDtypeStruct((t, d), _F32),
        scratch_shapes=[pltpu.VMEM((tm, d), _F32), pltpu.SemaphoreType.DMA((1,))],
        compiler_params=pltpu.CompilerParams(
            dimension_semantics=("arbitrary", "arbitrary"), vmem_limit_bytes=_vmem_limit(vmem)),
        name="ffn",
    )(h2, wg, wu, wd, x1, mod3, final_gain)


def kernel(x, c, w_ada, b_ada, norm1_g, w_in, pool_w, pool_scale, ret_decay_fwd, ret_decay_bwd,
           w_out, norm2_g, w_gate, w_up, w_down, final_g):
    bsz, seq, d = x.shape
    depth = w_ada.shape[0]
    heads = _RET_HEADS
    pool_width = pool_scale.shape[1]
    dh = (w_in.shape[2] - pool_width) // (4 * heads)
    assert pool_width == len(_POOL_WINDOWS) * dh and seq % _RET_CHUNK == 0 and dh == _RET_CHUNK
    assert all(w & (w - 1) == 0 and w // 2 <= _POOL_EDGE_ROWS for w in _POOL_WINDOWS)

    inv = 1.0 / (_ROPE_BASE ** np.linspace(0.0, 1.0, dh // 2))
    ang = np.arange(seq)[:, None] * inv[None, :]
    cos, sin = jnp.asarray(np.cos(ang), _F32), jnp.asarray(np.sin(ang), _F32)

    xf = x.reshape(bsz * seq, d)
    for l in range(depth):
        mod3 = _adaln(c, w_ada[l], b_ada[l]).reshape(bsz, _N_MOD, d)
        proj, wo, wg, wu, wd = _inproj(xf, mod3, norm1_g[l].reshape(1, d), cos, sin, w_in[l].astype(_BF16), seq,
                                       pool_width, [w_out[l], w_gate[l], w_up[l], w_down[l]])
        spread = lambda p: jnp.broadcast_to(p.astype(_F32)[:, None, None], (heads, 1, _RET_CHUNK))
        pool_out, ret_out = _mixer(
            proj, pool_w[l].astype(_BF16), pool_scale[l].reshape(len(_POOL_WINDOWS), 1, dh),
            spread(ret_decay_fwd[l]), spread(ret_decay_bwd[l]), bsz, seq)
        x1, h2 = _outproj(pool_out, ret_out, xf, mod3, norm2_g[l].reshape(1, d), wo, seq)
        xf = _ffn(h2, x1, mod3, wg, wu, wd, final_g.reshape(1, d), seq, final_norm=(l == depth - 1))
    return xf.reshape(bsz, seq, d)
```

```python
import functools

import jax
import jax.numpy as jnp
import numpy as np
from jax import lax
from jax.experimental import pallas as pl
from jax.experimental.pallas import tpu as pltpu

_EPS = 1e-6
_POOL_WINDOWS = (2, 4, 8, 16)
_RET_HEADS = 4
_ROPE_BASE = 10000.0
_N_MOD = 6

_V7X_VMEM_BYTES = 64 * 1024 * 1024
_LANES = 128
_BF16_ROWS = 16
_RET_CHUNK = 256
_POOL_HALO = 128
_POOL_EDGE_ROWS = 8
_NORM_ROWS = 128
_DOT_ROWS = 256

_F32 = jnp.float32
_BF16 = jnp.bfloat16


def _vmem_limit(nbytes):
    return int(min(_V7X_VMEM_BYTES - (2 << 20), nbytes + (8 << 20)))


def _nbytes(shape, dtype):
    n = 1
    for s in shape:
        n *= s
    return n * jnp.dtype(dtype).itemsize


def _dot(a, b):
    return jnp.dot(a, b, preferred_element_type=_F32)


def _silu(x):
    return x * jax.nn.sigmoid(x)


def _rms_rows(x):
    return x * lax.rsqrt(jnp.mean(x * x, axis=-1, keepdims=True) + _EPS)


def _col_blocks(d):
    return [slice(c0, c0 + _LANES) for c0 in range(0, d, _LANES)]


def _rmsnorm_rows(src_ref, dst_ref, row0, nrows, gain, shift=None, *, dst_row0=None):
    d = src_ref.shape[1]
    dst_shift = 0 if dst_row0 is None else dst_row0 - row0
    for r0 in range(row0, row0 + nrows, _NORM_ROWS):
        rows = slice(r0, r0 + _NORM_ROWS)
        dst_rows = slice(r0 + dst_shift, r0 + dst_shift + _NORM_ROWS)
        ss = None
        for cols in _col_blocks(d):
            xc = src_ref[rows, cols]
            ss = xc * xc if ss is None else ss + xc * xc
        rs = lax.rsqrt(jnp.sum(ss, axis=-1, keepdims=True) / d + _EPS)
        for cols in _col_blocks(d):
            y = src_ref[rows, cols] * rs * gain[:, cols]
            if shift is not None:
                y = y + shift[:, cols]
            dst_ref[dst_rows, cols] = y.astype(dst_ref.dtype)


def _adaln_kernel(c_ref, w_ref, b_ref, o_ref):
    c = c_ref[...]
    o_ref[...] = _dot(_silu(c).astype(_BF16), w_ref[...].astype(_BF16)) + b_ref[...]


def _adaln(c, w, b, *, tn=1024):
    bsz, d = c.shape
    n = w.shape[1]
    vmem = 2 * _nbytes((d, tn), _F32) + _nbytes((d, tn), _BF16) + 4 * _nbytes((bsz, d), _F32)
    return pl.pallas_call(
        _adaln_kernel,
        grid=(n // tn,),
        in_specs=[
            pl.BlockSpec((bsz, d), lambda j: (0, 0)),
            pl.BlockSpec((d, tn), lambda j: (0, j)),
            pl.BlockSpec((1, tn), lambda j: (0, j)),
        ],
        out_specs=pl.BlockSpec((bsz, tn), lambda j: (0, j)),
        out_shape=jax.ShapeDtypeStruct((bsz, n), _F32),
        compiler_params=pltpu.CompilerParams(
            dimension_semantics=("arbitrary",), vmem_limit_bytes=_vmem_limit(vmem)),
        name="adaln",
    )(c, w, b.reshape(1, n))


def _slab_specs(weights, steps, index_map):
    specs, shapes, nbytes = [], [], 0
    for w in weights:
        rows, cols = w.shape
        assert rows % steps == 0 and (rows // steps) % _BF16_ROWS == 0
        specs.append(pl.BlockSpec((rows // steps, cols), index_map))
        shapes.append(jax.ShapeDtypeStruct(w.shape, _BF16))
        nbytes += 2 * _nbytes((rows // steps, cols), _F32) + 2 * _nbytes((rows // steps, cols), _BF16)
    return specs, shapes, nbytes


def _cast_slabs(src_refs, dst_refs):
    for src, dst in zip(src_refs, dst_refs):
        dst[...] = src[...].astype(dst.dtype)


def _cast_gate_up(wg_ref, wu_ref, wgu_ref):
    for g in range(wg_ref.shape[1] // _LANES):
        cols = slice(g * _LANES, (g + 1) * _LANES)
        wgu_ref[:, 2 * g * _LANES:(2 * g + 1) * _LANES] = wg_ref[:, cols].astype(wgu_ref.dtype)
        wgu_ref[:, (2 * g + 1) * _LANES:(2 * g + 2) * _LANES] = wu_ref[:, cols].astype(wgu_ref.dtype)


def _inproj_kernel(x_ref, mod_ref, gain_ref, cos_ref, sin_ref, w_ref, *rest, n_cast, pool_width, heads):
    cast_in, (wg_ref, wu_ref) = rest[:n_cast], rest[n_cast:n_cast + 2]
    o_ref, *cast_out, wgu_ref, h_ref = rest[n_cast + 2:]
    width = (o_ref.shape[1] - pool_width) // 4
    dh = width // heads
    half = dh // 2
    scale = gain_ref[...] * (1.0 + mod_ref[0, 1:2, :])
    shift = mod_ref[0, 0:1, :]
    for r0 in range(0, x_ref.shape[0], _DOT_ROWS):
        rows = slice(r0, r0 + _DOT_ROWS)
        _rmsnorm_rows(x_ref, h_ref, r0, _DOT_ROWS, scale, shift)
        h = h_ref[rows, :]
        c0 = pool_width + 3 * width
        o_ref[rows, c0:c0 + width] = _silu(_dot(h, w_ref[:, c0:c0 + width])).astype(o_ref.dtype)
        cos, sin = cos_ref[rows, :], sin_ref[rows, :]
        for part in range(2):
            c0 = pool_width + part * width
            t = _dot(h, w_ref[:, c0:c0 + width])
            for hd in range(heads):
                t1 = t[:, hd * dh:hd * dh + half]
                t2 = t[:, hd * dh + half:(hd + 1) * dh]
                o_ref[rows, c0 + hd * dh:c0 + hd * dh + half] = (t1 * cos - t2 * sin).astype(o_ref.dtype)
                o_ref[rows, c0 + hd * dh + half:c0 + (hd + 1) * dh] = (t1 * sin + t2 * cos).astype(o_ref.dtype)
        if r0 == 0:
            _cast_slabs(cast_in, cast_out)
            _cast_gate_up(wg_ref, wu_ref, wgu_ref)
        c0 = pool_width + 2 * width
        o_ref[rows, c0:c0 + width] = _dot(h, w_ref[:, c0:c0 + width]).astype(o_ref.dtype)
        o_ref[rows, 0:pool_width] = _dot(h, w_ref[:, 0:pool_width]).astype(o_ref.dtype)


def _inproj(x2d, mod3, gain, cos, sin, w, seq, pool_width, cast_weights, gate_up, *, tm=512):
    t, d = x2d.shape
    n = w.shape[1]
    per_batch = seq // tm
    steps = t // tm
    cast_specs, cast_shapes, cast_bytes = _slab_specs(cast_weights, steps, lambda i: (i, 0))
    gu_specs, _, gu_bytes = _slab_specs(gate_up, steps, lambda i: (i, 0))
    gu_rows, dff = gate_up[0].shape
    assert gate_up[1].shape == (gu_rows, dff) and dff % _LANES == 0
    vmem = (2 * _nbytes((tm, d), _F32) + _nbytes((d, n), _BF16) + 2 * _nbytes((tm, n), _BF16)
            + _nbytes((tm, d), _BF16) + _nbytes((_DOT_ROWS, n), _F32) + 4 * _nbytes(cos.shape, _F32) // per_batch
            + cast_bytes + gu_bytes)
    return pl.pallas_call(
        functools.partial(_inproj_kernel, n_cast=len(cast_weights), pool_width=pool_width, heads=_RET_HEADS),
        grid=(t // tm,),
        in_specs=[
            pl.BlockSpec((tm, d), lambda i: (i, 0)),
            pl.BlockSpec((1, _N_MOD, d), lambda i: (i // per_batch, 0, 0)),
            pl.BlockSpec((1, d), lambda i: (0, 0)),
            pl.BlockSpec((tm, cos.shape[1]), lambda i: (i % per_batch, 0)),
            pl.BlockSpec((tm, sin.shape[1]), lambda i: (i % per_batch, 0)),
            pl.BlockSpec((d, n), lambda i: (0, 0), pipeline_mode=pl.Buffered(1)),
        ] + cast_specs + gu_specs,
        out_specs=([pl.BlockSpec((tm, n), lambda i: (i, 0))] + cast_specs
                   + [pl.BlockSpec((gu_rows // steps, 2 * dff), lambda i: (i, 0))]),
        out_shape=([jax.ShapeDtypeStruct((t, n), _BF16)] + cast_shapes
                   + [jax.ShapeDtypeStruct((gu_rows, 2 * dff), _BF16)]),
        scratch_shapes=[pltpu.VMEM((tm, d), _BF16)],
        compiler_params=pltpu.CompilerParams(
            dimension_semantics=("arbitrary",), vmem_limit_bytes=_vmem_limit(vmem)),
        name="inproj",
    )(x2d, mod3, gain, cos, sin, w, *cast_weights, *gate_up)


def _pool_chunks(u_ref, pad_ref, group, seq):
    c = _RET_CHUNK
    n = seq // c
    halo = _POOL_HALO
    half = jnp.int32(0)
    inv_w = jnp.float32(0)
    for gi, window in enumerate(_POOL_WINDOWS):
        half = jnp.where(group == gi, jnp.int32(window // 2), half)
        inv_w = jnp.where(group == gi, jnp.float32(1.0 / window), inv_w)

    zeros = jnp.zeros((halo, pad_ref.shape[1]), pad_ref.dtype)
    pad_ref[0:halo, :] = zeros
    pad_ref[halo + seq:, :] = zeros
    pad_ref[halo:halo + seq, :] = u_ref[...]

    rel = (lax.broadcasted_iota(jnp.int32, (c, c + 2 * halo), 1) - halo
           - lax.broadcasted_iota(jnp.int32, (c, c + 2 * halo), 0))
    band = (jnp.where((rel >= -half) & (rel < half), inv_w, 0.0)
            - jnp.where(rel == 0, 1.0, 0.0)).astype(_BF16)

    edge = _POOL_EDGE_ROWS
    t_lo = lax.broadcasted_iota(jnp.int32, (edge, pad_ref.shape[1]), 0)
    t_hi = t_lo + (seq - edge)
    width = (2 * half).astype(_F32)
    ratio_lo = width / (t_lo + half - jnp.maximum(t_lo - half, 0)).astype(_F32)
    ratio_hi = width / (jnp.minimum(t_hi + half, seq) - (t_hi - half)).astype(_F32)

    def pooled_chunk(ci):
        pooled = _dot(band, pad_ref[ci * c:ci * c + c + 2 * halo, :])
        if ci == 0:
            u_lo = u_ref[0:edge, :].astype(_F32)
            pooled = jnp.concatenate([(pooled[0:edge] + u_lo) * ratio_lo - u_lo, pooled[edge:]], axis=0)
        if ci == n - 1:
            u_hi = u_ref[seq - edge:seq, :].astype(_F32)
            pooled = jnp.concatenate([pooled[:c - edge], (pooled[c - edge:] + u_hi) * ratio_hi - u_hi], axis=0)
        return pooled.astype(_BF16)

    return pooled_chunk


def _mixer_kernel(u_ref, q_ref, k_ref, v_ref, gate_ref, pw_ref, ps_ref, df_ref, db_ref, pool_ref, ret_ref,
                  pad_ref, pooled_ref, kb_ref, acc_ref, dec_ref, sf_ref, sb_ref):
    seq, dh = q_ref.shape
    c = _RET_CHUNK
    n = seq // c

    pooled_chunk = _pool_chunks(u_ref, pad_ref, pl.program_id(1), seq)

    lg_f = -jnp.exp(df_ref[0])
    lg_b = -jnp.exp(db_ref[0])
    ri = lax.broadcasted_iota(jnp.int32, (c, c), 0).astype(_F32)
    cj = lax.broadcasted_iota(jnp.int32, (c, c), 1).astype(_F32)
    diff = ri - cj
    qk_scale = dh ** -0.5
    dec_ref[0] = jnp.exp(jnp.where(diff >= 0, diff * lg_f, -diff * lg_b)) * qk_scale
    dec_ref[1] = jnp.exp((ri + 1.0) * lg_f) * qk_scale
    dec_ref[2] = jnp.exp((c - 1.0 - ri) * lg_f)
    dec_ref[3] = jnp.exp((c - ri) * lg_b) * qk_scale
    dec_ref[4] = jnp.exp(ri * lg_b)
    chunk_f = jnp.exp(c * lg_f)
    chunk_b = jnp.exp(c * lg_b)

    def tn_dot(a, b):
        return lax.dot_general(a, b, (((0,), (0,)), ((), ())), preferred_element_type=_F32)

    def nt_dot(a, b):
        return lax.dot_general(a, b, (((1,), (1,)), ((), ())), preferred_element_type=_F32)

    for ci in range(n):
        rows = slice(ci * c, (ci + 1) * c)
        q = q_ref[rows, :]
        k = k_ref[rows, :]
        k32 = k.astype(_F32)
        if ci > 0:
            kb_ref[rows, :] = (k32 * dec_ref[4]).astype(_BF16)
        v = v_ref[rows, :]
        scores = nt_dot(q, k) * dec_ref[0]
        out = _dot(scores.astype(_BF16), v)
        if ci > 0:
            out = out + _dot(q, sf_ref[...].astype(_BF16)) * dec_ref[1]
        acc_ref[rows, :] = out
        if ci < n - 1:
            upd = tn_dot((k32 * dec_ref[2]).astype(_BF16), v)
            sf_ref[...] = upd if ci == 0 else sf_ref[...] * chunk_f + upd
        pooled_ref[rows, :] = pooled_chunk(ci)

    pool_ref[...] = (_dot(pooled_ref[...], pw_ref[0]) * ps_ref[0]).astype(pool_ref.dtype)

    for ci in reversed(range(n)):
        rows = slice(ci * c, (ci + 1) * c)
        out = acc_ref[rows, :]
        if ci < n - 1:
            out = out + _dot(q_ref[rows, :], sb_ref[...].astype(_BF16)) * dec_ref[3]
        ret = _rms_rows(out)
        ret_ref[rows, :] = (gate_ref[rows, :].astype(_F32) * ret).astype(ret_ref.dtype)
        if ci > 0:
            upd = tn_dot(kb_ref[rows, :], v_ref[rows, :])
            sb_ref[...] = upd if ci == n - 1 else sb_ref[...] * chunk_b + upd


def _mixer(proj, pool_w, pool_scale, dec_f, dec_b, bsz, seq):
    t = proj.shape[0]
    heads = _RET_HEADS
    dh = pool_w.shape[-1]
    pool_blocks = len(_POOL_WINDOWS)
    col = lambda first: (lambda b, h: (b, first + h))
    seq_block = lambda first: pl.BlockSpec((seq, dh), col(first))
    per_head = lambda shape: pl.BlockSpec((1,) + shape, lambda b, h: (h, 0, 0))
    scratch = [
        pltpu.VMEM((seq + 2 * _POOL_HALO, dh), _BF16),
        pltpu.VMEM((seq, dh), _BF16),
        pltpu.VMEM((seq, dh), _BF16),
        pltpu.VMEM((seq, dh), _F32),
        pltpu.VMEM((5, _RET_CHUNK, _RET_CHUNK), _F32),
        pltpu.VMEM((dh, dh), _F32),
        pltpu.VMEM((dh, dh), _F32),
    ]
    vmem = 2 * 7 * _nbytes((seq, dh), _BF16) + 4 * _nbytes((seq, dh), _F32) + 8 * _nbytes((dh, dh), _F32)
    return pl.pallas_call(
        _mixer_kernel,
        grid=(bsz, heads),
        in_specs=[
            seq_block(0),
            seq_block(pool_blocks),
            seq_block(pool_blocks + heads),
            seq_block(pool_blocks + 2 * heads),
            seq_block(pool_blocks + 3 * heads),
            per_head((dh, dh)),
            per_head((1, dh)),
            per_head((1, _RET_CHUNK)), per_head((1, _RET_CHUNK)),
        ],
        out_specs=[pl.BlockSpec((seq, dh), lambda b, h: (b, h)),
                   pl.BlockSpec((seq, dh), lambda b, h: (b, h))],
        out_shape=[jax.ShapeDtypeStruct((t, pool_blocks * dh), _BF16),
                   jax.ShapeDtypeStruct((t, heads * dh), _BF16)],
        scratch_shapes=scratch,
        compiler_params=pltpu.CompilerParams(
            dimension_semantics=("arbitrary", "arbitrary"), vmem_limit_bytes=_vmem_limit(vmem)),
        name="mixer",
    )(proj, proj, proj, proj, proj, pool_w, pool_scale, dec_f, dec_b)


def _outproj_kernel(p_ref, r_ref, x_ref, mod_ref, gain_ref, w_ref, x1_ref, h2_ref):
    kp = p_ref.shape[1]
    gate = mod_ref[0, 2:3, :]
    scale = gain_ref[...] * (1.0 + mod_ref[0, 4:5, :])
    shift = mod_ref[0, 3:4, :]
    for r0 in range(0, x_ref.shape[0], _DOT_ROWS):
        rows = slice(r0, r0 + _DOT_ROWS)
        mixed = _dot(p_ref[rows, :], w_ref[0:kp, :]) + _dot(r_ref[rows, :], w_ref[kp:, :])
        x1_ref[rows, :] = x_ref[rows, :] + gate * mixed
    _rmsnorm_rows(x1_ref, h2_ref, 0, x_ref.shape[0], scale, shift)


def _outproj(pool_out, ret_out, x2d, mod3, gain, w, seq, *, tm=512):
    t, d = x2d.shape
    kp, kr = pool_out.shape[1], ret_out.shape[1]
    per_batch = seq // tm
    vmem = (2 * _nbytes((tm, kp + kr), _BF16) + 4 * _nbytes((tm, d), _F32) + 2 * _nbytes((tm, d), _BF16)
            + _nbytes((kp + kr, d), _BF16) + _nbytes((_DOT_ROWS, d), _F32))
    return pl.pallas_call(
        _outproj_kernel,
        grid=(t // tm,),
        in_specs=[
            pl.BlockSpec((tm, kp), lambda i: (i, 0)),
            pl.BlockSpec((tm, kr), lambda i: (i, 0)),
            pl.BlockSpec((tm, d), lambda i: (i, 0)),
            pl.BlockSpec((1, _N_MOD, d), lambda i: (i // per_batch, 0, 0)),
            pl.BlockSpec((1, d), lambda i: (0, 0)),
            pl.BlockSpec((kp + kr, d), lambda i: (0, 0), pipeline_mode=pl.Buffered(1)),
        ],
        out_specs=[pl.BlockSpec((tm, d), lambda i: (i, 0)),
                   pl.BlockSpec((tm, d), lambda i: (i, 0))],
        out_shape=[jax.ShapeDtypeStruct((t, d), _F32),
                   jax.ShapeDtypeStruct((t, d), _BF16)],
        compiler_params=pltpu.CompilerParams(
            dimension_semantics=("arbitrary",), vmem_limit_bytes=_vmem_limit(vmem)),
        name="outproj",
    )(pool_out, ret_out, x2d, mod3, gain, w)


def _ffn_kernel(h2_ref, wgu_ref, wd_ref, x1_hbm, mod_ref, gain_ref, o_ref, x1_buf, sem, *, final_norm):
    i, f = pl.program_id(0), pl.program_id(1)
    tm, d = o_ref.shape

    def x1_copy():
        return pltpu.make_async_copy(x1_hbm.at[pl.ds(pl.multiple_of(i * tm, tm), tm), :], x1_buf, sem.at[0])

    last = pl.num_programs(1) - 1

    def hidden():
        gu = _dot(h2_ref[...], wgu_ref[...])
        parts = [_silu(gu[:, c0:c0 + _LANES]) * gu[:, c0 + _LANES:c0 + 2 * _LANES]
                 for c0 in range(0, gu.shape[1], 2 * _LANES)]
        return jnp.concatenate(parts, axis=1).astype(_BF16)

    def finish_rows(r0):
        rows = slice(r0, r0 + _NORM_ROWS)
        gate = mod_ref[0, 5:6, :]
        gain = gain_ref[...]
        ss = None
        for cols in _col_blocks(d):
            x2 = x1_buf[rows, cols] + gate[:, cols] * o_ref[rows, cols]
            o_ref[rows, cols] = x2
            ss = x2 * x2 if ss is None else ss + x2 * x2
        if final_norm:
            rs = lax.rsqrt(jnp.sum(ss, axis=-1, keepdims=True) / d + _EPS)
            for cols in _col_blocks(d):
                o_ref[rows, cols] = o_ref[rows, cols] * rs * gain[:, cols]

    @pl.when(f == 0)
    def _():
        x1_copy().start()
        o_ref[...] = _dot(hidden(), wd_ref[...])

    @pl.when((f > 0) & (f < last))
    def _():
        o_ref[...] += _dot(hidden(), wd_ref[...])

    @pl.when(f == last)
    def _():
        x1_copy().wait()
        hid = hidden()
        for r0 in range(0, tm, _DOT_ROWS):
            rows = slice(r0, r0 + _DOT_ROWS)
            o_ref[rows, :] += _dot(hid[rows, :], wd_ref[...])
            for s0 in range(r0, r0 + _DOT_ROWS, _NORM_ROWS):
                finish_rows(s0)


def _ffn(h2, x1, mod3, wgu, wd, final_gain, seq, *, final_norm, tm=1024, tf=512):
    t, d = x1.shape
    dff = wd.shape[0]
    assert wgu.shape == (d, 2 * dff) and tf % _LANES == 0
    assert dff // tf >= 2
    per_batch = seq // tm
    vmem = (2 * _nbytes((tm, d), _BF16) + 6 * _nbytes((d, tf), _BF16) + 3 * _nbytes((tm, d), _F32)
            + 3 * _nbytes((tm, tf), _F32))
    return pl.pallas_call(
        functools.partial(_ffn_kernel, final_norm=final_norm),
        grid=(t // tm, dff // tf),
        in_specs=[
            pl.BlockSpec((tm, d), lambda i, f: (i, 0)),
            pl.BlockSpec((d, 2 * tf), lambda i, f: (0, f)),
            pl.BlockSpec((tf, d), lambda i, f: (f, 0)),
            pl.BlockSpec(memory_space=pl.ANY),
            pl.BlockSpec((1, _N_MOD, d), lambda i, f: (i // per_batch, 0, 0)),
            pl.BlockSpec((1, d), lambda i, f: (0, 0)),
        ],
        out_specs=pl.BlockSpec((tm, d), lambda i, f: (i, 0)),
        out_shape=jax.ShapeDtypeStruct((t, d), _F32),
        scratch_shapes=[pltpu.VMEM((tm, d), _F32), pltpu.SemaphoreType.DMA((1,))],
        compiler_params=pltpu.CompilerParams(
            dimension_semantics=("arbitrary", "arbitrary"), vmem_limit_bytes=_vmem_limit(vmem)),
        name="ffn",
    )(h2, wgu, wd, x1, mod3, final_gain)


def kernel(x, c, w_ada, b_ada, norm1_g, w_in, pool_w, pool_scale, ret_decay_fwd, ret_decay_bwd,
           w_out, norm2_g, w_gate, w_up, w_down, final_g):
    bsz, seq, d = x.shape
    depth = w_ada.shape[0]
    heads = _RET_HEADS
    pool_width = pool_scale.shape[1]
    dh = (w_in.shape[2] - pool_width) // (4 * heads)
    assert pool_width == len(_POOL_WINDOWS) * dh and seq % _RET_CHUNK == 0 and dh == _RET_CHUNK
    assert all(w & (w - 1) == 0 and w // 2 <= _POOL_EDGE_ROWS for w in _POOL_WINDOWS)

    inv = 1.0 / (_ROPE_BASE ** np.linspace(0.0, 1.0, dh // 2))
    ang = np.arange(seq)[:, None] * inv[None, :]
    cos, sin = jnp.asarray(np.cos(ang), _F32), jnp.asarray(np.sin(ang), _F32)

    xf = x.reshape(bsz * seq, d)
    for l in range(depth):
        mod3 = _adaln(c, w_ada[l], b_ada[l]).reshape(bsz, _N_MOD, d)
        proj, wo, wd, wgu = _inproj(xf, mod3, norm1_g[l].reshape(1, d), cos, sin, w_in[l].astype(_BF16), seq,
                                    pool_width, [w_out[l], w_down[l]], [w_gate[l], w_up[l]])
        spread = lambda p: jnp.broadcast_to(p.astype(_F32)[:, None, None], (heads, 1, _RET_CHUNK))
        pool_out, ret_out = _mixer(
            proj, pool_w[l].astype(_BF16), pool_scale[l].reshape(len(_POOL_WINDOWS), 1, dh),
            spread(ret_decay_fwd[l]), spread(ret_decay_bwd[l]), bsz, seq)
        x1, h2 = _outproj(pool_out, ret_out, xf, mod3, norm2_g[l].reshape(1, d), wo, seq)
        xf = _ffn(h2, x1, mod3, wgu, wd, final_g.reshape(1, d), seq, final_norm=(l == depth - 1))
    return xf.reshape(bsz, seq, d)
```

```python
import functools

import jax
import jax.numpy as jnp
import numpy as np
from jax import lax
from jax.experimental import pallas as pl
from jax.experimental.pallas import tpu as pltpu

_EPS = 1e-6
_POOL_WINDOWS = (2, 4, 8, 16)
_RET_HEADS = 4
_ROPE_BASE = 10000.0
_N_MOD = 6

_V7X_VMEM_BYTES = 64 * 1024 * 1024
_LANES = 128
_BF16_ROWS = 16
_RET_CHUNK = 256
_POOL_HALO = 128
_POOL_EDGE_ROWS = 8
_NORM_ROWS = 128
_DOT_ROWS = 256
_MIXER_HEADS_PER_STEP = 2

_F32 = jnp.float32
_BF16 = jnp.bfloat16


def _vmem_limit(nbytes):
    return int(min(_V7X_VMEM_BYTES - (2 << 20), nbytes + (8 << 20)))


def _nbytes(shape, dtype):
    n = 1
    for s in shape:
        n *= s
    return n * jnp.dtype(dtype).itemsize


def _dot(a, b):
    return jnp.dot(a, b, preferred_element_type=_F32)


def _silu(x):
    return x * jax.nn.sigmoid(x)


def _rms_rows(x):
    return x * lax.rsqrt(jnp.mean(x * x, axis=-1, keepdims=True) + _EPS)


def _col_blocks(d):
    return [slice(c0, c0 + _LANES) for c0 in range(0, d, _LANES)]


def _rmsnorm_rows(src_ref, dst_ref, row0, nrows, gain, shift=None):
    d = src_ref.shape[1]
    for r0 in range(row0, row0 + nrows, _NORM_ROWS):
        rows = slice(r0, r0 + _NORM_ROWS)
        ss = None
        for cols in _col_blocks(d):
            xc = src_ref[rows, cols]
            ss = xc * xc if ss is None else ss + xc * xc
        rs = lax.rsqrt(jnp.sum(ss, axis=-1, keepdims=True) / d + _EPS)
        for cols in _col_blocks(d):
            y = src_ref[rows, cols] * rs * gain[:, cols]
            if shift is not None:
                y = y + shift[:, cols]
            dst_ref[rows, cols] = y.astype(dst_ref.dtype)


def _adaln_kernel(c_ref, w_ref, b_ref, o_ref):
    c = c_ref[...]
    o_ref[...] = _dot(_silu(c).astype(_BF16), w_ref[...].astype(_BF16)) + b_ref[...]


def _adaln(c, w, b, *, tn=1024):
    bsz, d = c.shape
    n = w.shape[1]
    vmem = 2 * _nbytes((d, tn), _F32) + _nbytes((d, tn), _BF16) + 4 * _nbytes((bsz, d), _F32)
    return pl.pallas_call(
        _adaln_kernel,
        grid=(n // tn,),
        in_specs=[
            pl.BlockSpec((bsz, d), lambda j: (0, 0)),
            pl.BlockSpec((d, tn), lambda j: (0, j)),
            pl.BlockSpec((1, tn), lambda j: (0, j)),
        ],
        out_specs=pl.BlockSpec((bsz, tn), lambda j: (0, j)),
        out_shape=jax.ShapeDtypeStruct((bsz, n), _F32),
        compiler_params=pltpu.CompilerParams(
            dimension_semantics=("arbitrary",), vmem_limit_bytes=_vmem_limit(vmem)),
        name="adaln",
    )(c, w, b.reshape(1, n))


def _slab_specs(weights, steps, index_map):
    specs, shapes, nbytes = [], [], 0
    for w in weights:
        rows, cols = w.shape
        assert rows % steps == 0 and (rows // steps) % _BF16_ROWS == 0
        specs.append(pl.BlockSpec((rows // steps, cols), index_map))
        shapes.append(jax.ShapeDtypeStruct(w.shape, _BF16))
        nbytes += 2 * _nbytes((rows // steps, cols), _F32) + 2 * _nbytes((rows // steps, cols), _BF16)
    return specs, shapes, nbytes


def _cast_slabs(src_refs, dst_refs):
    for src, dst in zip(src_refs, dst_refs):
        dst[...] = src[...].astype(dst.dtype)


def _cast_gate_up(wg_ref, wu_ref, wgu_ref):
    for g in range(wg_ref.shape[1] // _LANES):
        cols = slice(g * _LANES, (g + 1) * _LANES)
        wgu_ref[:, 2 * g * _LANES:(2 * g + 1) * _LANES] = wg_ref[:, cols].astype(wgu_ref.dtype)
        wgu_ref[:, (2 * g + 1) * _LANES:(2 * g + 2) * _LANES] = wu_ref[:, cols].astype(wgu_ref.dtype)


def _inproj_kernel(x_ref, mod_ref, gain_ref, cos_ref, sin_ref, w_ref, *rest, n_cast, pool_width, heads):
    cast_in, (wg_ref, wu_ref) = rest[:n_cast], rest[n_cast:n_cast + 2]
    o_ref, *cast_out, wgu_ref, h_ref = rest[n_cast + 2:]
    width = (o_ref.shape[1] - pool_width) // 4
    dh = width // heads
    half = dh // 2
    scale = gain_ref[...] * (1.0 + mod_ref[0, 1:2, :])
    shift = mod_ref[0, 0:1, :]
    for r0 in range(0, x_ref.shape[0], _DOT_ROWS):
        rows = slice(r0, r0 + _DOT_ROWS)
        _rmsnorm_rows(x_ref, h_ref, r0, _DOT_ROWS, scale, shift)
        h = h_ref[rows, :]
        c0 = pool_width + 3 * width
        o_ref[rows, c0:c0 + width] = _silu(_dot(h, w_ref[:, c0:c0 + width])).astype(o_ref.dtype)
        cos, sin = cos_ref[rows, :], sin_ref[rows, :]
        for part in range(2):
            c0 = pool_width + part * width
            t = _dot(h, w_ref[:, c0:c0 + width])
            for hd in range(heads):
                t1 = t[:, hd * dh:hd * dh + half]
                t2 = t[:, hd * dh + half:(hd + 1) * dh]
                o_ref[rows, c0 + hd * dh:c0 + hd * dh + half] = (t1 * cos - t2 * sin).astype(o_ref.dtype)
                o_ref[rows, c0 + hd * dh + half:c0 + (hd + 1) * dh] = (t1 * sin + t2 * cos).astype(o_ref.dtype)
        if r0 == 0:
            _cast_slabs(cast_in, cast_out)
            _cast_gate_up(wg_ref, wu_ref, wgu_ref)
        c0 = pool_width + 2 * width
        o_ref[rows, c0:c0 + width] = _dot(h, w_ref[:, c0:c0 + width]).astype(o_ref.dtype)
        o_ref[rows, 0:pool_width] = _dot(h, w_ref[:, 0:pool_width]).astype(o_ref.dtype)


def _inproj(x2d, mod3, gain, cos, sin, w, seq, pool_width, cast_weights, gate_up, *, tm=512):
    t, d = x2d.shape
    n = w.shape[1]
    per_batch = seq // tm
    steps = t // tm
    cast_specs, cast_shapes, cast_bytes = _slab_specs(cast_weights, steps, lambda i: (i, 0))
    gu_specs, _, gu_bytes = _slab_specs(gate_up, steps, lambda i: (i, 0))
    gu_rows, dff = gate_up[0].shape
    assert gate_up[1].shape == (gu_rows, dff) and dff % _LANES == 0
    vmem = (2 * _nbytes((tm, d), _F32) + _nbytes((d, n), _BF16) + 2 * _nbytes((tm, n), _BF16)
            + _nbytes((tm, d), _BF16) + _nbytes((_DOT_ROWS, n), _F32) + 4 * _nbytes(cos.shape, _F32) // per_batch
            + cast_bytes + gu_bytes)
    return pl.pallas_call(
        functools.partial(_inproj_kernel, n_cast=len(cast_weights), pool_width=pool_width, heads=_RET_HEADS),
        grid=(t // tm,),
        in_specs=[
            pl.BlockSpec((tm, d), lambda i: (i, 0)),
            pl.BlockSpec((1, _N_MOD, d), lambda i: (i // per_batch, 0, 0)),
            pl.BlockSpec((1, d), lambda i: (0, 0)),
            pl.BlockSpec((tm, cos.shape[1]), lambda i: (i % per_batch, 0)),
            pl.BlockSpec((tm, sin.shape[1]), lambda i: (i % per_batch, 0)),
            pl.BlockSpec((d, n), lambda i: (0, 0), pipeline_mode=pl.Buffered(1)),
        ] + cast_specs + gu_specs,
        out_specs=([pl.BlockSpec((tm, n), lambda i: (i, 0))] + cast_specs
                   + [pl.BlockSpec((gu_rows // steps, 2 * dff), lambda i: (i, 0))]),
        out_shape=([jax.ShapeDtypeStruct((t, n), _BF16)] + cast_shapes
                   + [jax.ShapeDtypeStruct((gu_rows, 2 * dff), _BF16)]),
        scratch_shapes=[pltpu.VMEM((tm, d), _BF16)],
        compiler_params=pltpu.CompilerParams(
            dimension_semantics=("arbitrary",), vmem_limit_bytes=_vmem_limit(vmem)),
        name="inproj",
    )(x2d, mod3, gain, cos, sin, w, *cast_weights, *gate_up)


def _pool_chunks(u_ref, pad_ref, cols, group, seq):
    c = _RET_CHUNK
    n = seq // c
    halo = _POOL_HALO
    half = jnp.int32(0)
    inv_w = jnp.float32(0)
    for gi, window in enumerate(_POOL_WINDOWS):
        half = jnp.where(group == gi, jnp.int32(window // 2), half)
        inv_w = jnp.where(group == gi, jnp.float32(1.0 / window), inv_w)

    width_c = cols.stop - cols.start
    zeros = jnp.zeros((halo, width_c), pad_ref.dtype)
    pad_ref[0:halo, cols] = zeros
    pad_ref[halo + seq:, cols] = zeros
    pad_ref[halo:halo + seq, cols] = u_ref[:, cols]

    rel = (lax.broadcasted_iota(jnp.int32, (c, c + 2 * halo), 1) - halo
           - lax.broadcasted_iota(jnp.int32, (c, c + 2 * halo), 0))
    band = (jnp.where((rel >= -half) & (rel < half), inv_w, 0.0)
            - jnp.where(rel == 0, 1.0, 0.0)).astype(_BF16)

    edge = _POOL_EDGE_ROWS
    t_lo = lax.broadcasted_iota(jnp.int32, (edge, width_c), 0)
    t_hi = t_lo + (seq - edge)
    width = (2 * half).astype(_F32)
    ratio_lo = width / (t_lo + half - jnp.maximum(t_lo - half, 0)).astype(_F32)
    ratio_hi = width / (jnp.minimum(t_hi + half, seq) - (t_hi - half)).astype(_F32)

    def pooled_chunk(ci):
        pooled = _dot(band, pad_ref[ci * c:ci * c + c + 2 * halo, cols])
        if ci == 0:
            u_lo = u_ref[0:edge, cols].astype(_F32)
            pooled = jnp.concatenate([(pooled[0:edge] + u_lo) * ratio_lo - u_lo, pooled[edge:]], axis=0)
        if ci == n - 1:
            u_hi = u_ref[seq - edge:seq, cols].astype(_F32)
            pooled = jnp.concatenate([pooled[:c - edge], (pooled[c - edge:] + u_hi) * ratio_hi - u_hi], axis=0)
        return pooled.astype(_BF16)

    return pooled_chunk


def _mixer_kernel(u_ref, q_ref, k_ref, v_ref, gate_ref, pw_ref, ps_ref, df_ref, db_ref, pool_ref, ret_ref,
                  pad_ref, pooled_ref, kb_ref, acc_ref, dec_ref, sf_ref, sb_ref):
    seq = q_ref.shape[0]
    nh, dh = pw_ref.shape[0], pw_ref.shape[-1]
    c = _RET_CHUNK
    n = seq // c
    head_cols = [slice(hd * dh, (hd + 1) * dh) for hd in range(nh)]

    def tn_dot(a, b):
        return lax.dot_general(a, b, (((0,), (0,)), ((), ())), preferred_element_type=_F32)

    def nt_dot(a, b):
        return lax.dot_general(a, b, (((1,), (1,)), ((), ())), preferred_element_type=_F32)

    ri = lax.broadcasted_iota(jnp.int32, (c, c), 0).astype(_F32)
    cj = lax.broadcasted_iota(jnp.int32, (c, c), 1).astype(_F32)
    diff = ri - cj
    qk_scale = dh ** -0.5
    pooled_chunk, chunk_f, chunk_b = [], [], []
    for hd, cols in enumerate(head_cols):
        pooled_chunk.append(_pool_chunks(u_ref, pad_ref, cols, pl.program_id(1) * nh + hd, seq))
        lg_f = -jnp.exp(df_ref[hd])
        lg_b = -jnp.exp(db_ref[hd])
        dec_ref[hd, 0] = jnp.exp(jnp.where(diff >= 0, diff * lg_f, -diff * lg_b)) * qk_scale
        dec_ref[hd, 1] = jnp.exp((ri + 1.0) * lg_f) * qk_scale
        dec_ref[hd, 2] = jnp.exp((c - 1.0 - ri) * lg_f)
        dec_ref[hd, 3] = jnp.exp((c - ri) * lg_b) * qk_scale
        dec_ref[hd, 4] = jnp.exp(ri * lg_b)
        chunk_f.append(jnp.exp(c * lg_f))
        chunk_b.append(jnp.exp(c * lg_b))

    for ci in range(n):
        rows = slice(ci * c, (ci + 1) * c)
        for hd, cols in enumerate(head_cols):
            q = q_ref[rows, cols]
            k = k_ref[rows, cols]
            k32 = k.astype(_F32)
            if ci > 0:
                kb_ref[rows, cols] = (k32 * dec_ref[hd, 4]).astype(_BF16)
            v = v_ref[rows, cols]
            scores = nt_dot(q, k) * dec_ref[hd, 0]
            out = _dot(scores.astype(_BF16), v)
            if ci > 0:
                out = out + _dot(q, sf_ref[hd].astype(_BF16)) * dec_ref[hd, 1]
            acc_ref[rows, cols] = out
            if ci < n - 1:
                upd = tn_dot((k32 * dec_ref[hd, 2]).astype(_BF16), v)
                sf_ref[hd] = upd if ci == 0 else sf_ref[hd] * chunk_f[hd] + upd
            pooled_ref[rows, cols] = pooled_chunk[hd](ci)

    for hd, cols in enumerate(head_cols):
        pool_ref[:, cols] = (_dot(pooled_ref[:, cols], pw_ref[hd]) * ps_ref[hd]).astype(pool_ref.dtype)

    for ci in reversed(range(n)):
        rows = slice(ci * c, (ci + 1) * c)
        for hd, cols in enumerate(head_cols):
            out = acc_ref[rows, cols]
            if ci < n - 1:
                out = out + _dot(q_ref[rows, cols], sb_ref[hd].astype(_BF16)) * dec_ref[hd, 3]
            ret = _rms_rows(out)
            ret_ref[rows, cols] = (gate_ref[rows, cols].astype(_F32) * ret).astype(ret_ref.dtype)
            if ci > 0:
                upd = tn_dot(kb_ref[rows, cols], v_ref[rows, cols])
                sb_ref[hd] = upd if ci == n - 1 else sb_ref[hd] * chunk_b[hd] + upd


def _mixer(proj, pool_w, pool_scale, dec_f, dec_b, bsz, seq):
    t = proj.shape[0]
    heads = _RET_HEADS
    dh = pool_w.shape[-1]
    pool_blocks = len(_POOL_WINDOWS)
    nh = _MIXER_HEADS_PER_STEP
    assert heads % nh == 0 and pool_blocks == heads
    wide = nh * dh
    seq_block = lambda first: pl.BlockSpec((seq, wide), lambda b, j: (b, first // nh + j))
    per_head = lambda shape: pl.BlockSpec((nh,) + shape, lambda b, j: (j, 0, 0))
    scratch = [
        pltpu.VMEM((seq + 2 * _POOL_HALO, wide), _BF16),
        pltpu.VMEM((seq, wide), _BF16),
        pltpu.VMEM((seq, wide), _BF16),
        pltpu.VMEM((seq, wide), _F32),
        pltpu.VMEM((nh, 5, _RET_CHUNK, _RET_CHUNK), _F32),
        pltpu.VMEM((nh, dh, dh), _F32),
        pltpu.VMEM((nh, dh, dh), _F32),
    ]
    vmem = (2 * 7 * _nbytes((seq, wide), _BF16) + 4 * _nbytes((seq, wide), _F32)
            + 8 * nh * _nbytes((dh, dh), _F32))
    return pl.pallas_call(
        _mixer_kernel,
        grid=(bsz, heads // nh),
        in_specs=[
            seq_block(0),
            seq_block(pool_blocks),
            seq_block(pool_blocks + heads),
            seq_block(pool_blocks + 2 * heads),
            seq_block(pool_blocks + 3 * heads),
            per_head((dh, dh)),
            per_head((1, dh)),
            per_head((1, _RET_CHUNK)), per_head((1, _RET_CHUNK)),
        ],
        out_specs=[pl.BlockSpec((seq, wide), lambda b, j: (b, j)),
                   pl.BlockSpec((seq, wide), lambda b, j: (b, j))],
        out_shape=[jax.ShapeDtypeStruct((t, pool_blocks * dh), _BF16),
                   jax.ShapeDtypeStruct((t, heads * dh), _BF16)],
        scratch_shapes=scratch,
        compiler_params=pltpu.CompilerParams(
            dimension_semantics=("arbitrary", "arbitrary"), vmem_limit_bytes=_vmem_limit(vmem)),
        name="mixer",
    )(proj, proj, proj, proj, proj, pool_w, pool_scale, dec_f, dec_b)


def _outproj_kernel(p_ref, r_ref, x_ref, mod_ref, gain_ref, w_ref, x1_ref, h2_ref):
    kp = p_ref.shape[1]
    gate = mod_ref[0, 2:3, :]
    scale = gain_ref[...] * (1.0 + mod_ref[0, 4:5, :])
    shift = mod_ref[0, 3:4, :]
    for r0 in range(0, x_ref.shape[0], _DOT_ROWS):
        rows = slice(r0, r0 + _DOT_ROWS)
        mixed = _dot(p_ref[rows, :], w_ref[0:kp, :]) + _dot(r_ref[rows, :], w_ref[kp:, :])
        x1_ref[rows, :] = x_ref[rows, :] + gate * mixed
        _rmsnorm_rows(x1_ref, h2_ref, r0, _DOT_ROWS, scale, shift)


def _outproj(pool_out, ret_out, x2d, mod3, gain, w, seq, *, tm=512):
    t, d = x2d.shape
    kp, kr = pool_out.shape[1], ret_out.shape[1]
    per_batch = seq // tm
    vmem = (2 * _nbytes((tm, kp + kr), _BF16) + 4 * _nbytes((tm, d), _F32) + 2 * _nbytes((tm, d), _BF16)
            + _nbytes((kp + kr, d), _BF16) + _nbytes((_DOT_ROWS, d), _F32))
    return pl.pallas_call(
        _outproj_kernel,
        grid=(t // tm,),
        in_specs=[
            pl.BlockSpec((tm, kp), lambda i: (i, 0)),
            pl.BlockSpec((tm, kr), lambda i: (i, 0)),
            pl.BlockSpec((tm, d), lambda i: (i, 0)),
            pl.BlockSpec((1, _N_MOD, d), lambda i: (i // per_batch, 0, 0)),
            pl.BlockSpec((1, d), lambda i: (0, 0)),
            pl.BlockSpec((kp + kr, d), lambda i: (0, 0), pipeline_mode=pl.Buffered(1)),
        ],
        out_specs=[pl.BlockSpec((tm, d), lambda i: (i, 0)),
                   pl.BlockSpec((tm, d), lambda i: (i, 0))],
        out_shape=[jax.ShapeDtypeStruct((t, d), _F32),
                   jax.ShapeDtypeStruct((t, d), _BF16)],
        compiler_params=pltpu.CompilerParams(
            dimension_semantics=("arbitrary",), vmem_limit_bytes=_vmem_limit(vmem)),
        name="outproj",
    )(pool_out, ret_out, x2d, mod3, gain, w)


def _ffn_kernel(h2_ref, wgu_ref, wd_ref, x1_hbm, mod_ref, gain_ref, o_ref, x1_buf, sem, *, final_norm):
    i, f = pl.program_id(0), pl.program_id(1)
    tm, d = o_ref.shape

    def x1_copy():
        return pltpu.make_async_copy(x1_hbm.at[pl.ds(pl.multiple_of(i * tm, tm), tm), :], x1_buf, sem.at[0])

    last = pl.num_programs(1) - 1

    def hidden():
        gu = _dot(h2_ref[...], wgu_ref[...])
        parts = [_silu(gu[:, c0:c0 + _LANES]) * gu[:, c0 + _LANES:c0 + 2 * _LANES]
                 for c0 in range(0, gu.shape[1], 2 * _LANES)]
        return jnp.concatenate(parts, axis=1).astype(_BF16)

    def finish_rows(r0):
        rows = slice(r0, r0 + _NORM_ROWS)
        gate = mod_ref[0, 5:6, :]
        gain = gain_ref[...]
        ss = None
        for cols in _col_blocks(d):
            x2 = x1_buf[rows, cols] + gate[:, cols] * o_ref[rows, cols]
            o_ref[rows, cols] = x2
            ss = x2 * x2 if ss is None else ss + x2 * x2
        if final_norm:
            rs = lax.rsqrt(jnp.sum(ss, axis=-1, keepdims=True) / d + _EPS)
            for cols in _col_blocks(d):
                o_ref[rows, cols] = o_ref[rows, cols] * rs * gain[:, cols]

    @pl.when(f == 0)
    def _():
        x1_copy().start()
        o_ref[...] = _dot(hidden(), wd_ref[...])

    @pl.when((f > 0) & (f < last))
    def _():
        o_ref[...] += _dot(hidden(), wd_ref[...])

    @pl.when(f == last)
    def _():
        x1_copy().wait()
        hid = hidden()
        for r0 in range(0, tm, _DOT_ROWS):
            rows = slice(r0, r0 + _DOT_ROWS)
            o_ref[rows, :] += _dot(hid[rows, :], wd_ref[...])
            for s0 in range(r0, r0 + _DOT_ROWS, _NORM_ROWS):
                finish_rows(s0)


def _ffn(h2, x1, mod3, wgu, wd, final_gain, seq, *, final_norm, tm=1024, tf=512):
    t, d = x1.shape
    dff = wd.shape[0]
    assert wgu.shape == (d, 2 * dff) and tf % _LANES == 0
    assert dff // tf >= 2
    per_batch = seq // tm
    vmem = (2 * _nbytes((tm, d), _BF16) + 6 * _nbytes((d, tf), _BF16) + 3 * _nbytes((tm, d), _F32)
            + 3 * _nbytes((tm, tf), _F32))
    return pl.pallas_call(
        functools.partial(_ffn_kernel, final_norm=final_norm),
        grid=(t // tm, dff // tf),
        in_specs=[
            pl.BlockSpec((tm, d), lambda i, f: (i, 0)),
            pl.BlockSpec((d, 2 * tf), lambda i, f: (0, f)),
            pl.BlockSpec((tf, d), lambda i, f: (f, 0)),
            pl.BlockSpec(memory_space=pl.ANY),
            pl.BlockSpec((1, _N_MOD, d), lambda i, f: (i // per_batch, 0, 0)),
            pl.BlockSpec((1, d), lambda i, f: (0, 0)),
        ],
        out_specs=pl.BlockSpec((tm, d), lambda i, f: (i, 0)),
        out_shape=jax.ShapeDtypeStruct((t, d), _F32),
        scratch_shapes=[pltpu.VMEM((tm, d), _F32), pltpu.SemaphoreType.DMA((1,))],
        compiler_params=pltpu.CompilerParams(
            dimension_semantics=("arbitrary", "arbitrary"), vmem_limit_bytes=_vmem_limit(vmem)),
        name="ffn",
    )(h2, wgu, wd, x1, mod3, final_gain)


def kernel(x, c, w_ada, b_ada, norm1_g, w_in, pool_w, pool_scale, ret_decay_fwd, ret_decay_bwd,
           w_out, norm2_g, w_gate, w_up, w_down, final_g):
    bsz, seq, d = x.shape
    depth = w_ada.shape[0]
    heads = _RET_HEADS
    pool_width = pool_scale.shape[1]
    dh = (w_in.shape[2] - pool_width) // (4 * heads)
    assert pool_width == len(_POOL_WINDOWS) * dh and seq % _RET_CHUNK == 0 and dh == _RET_CHUNK
    assert all(w & (w - 1) == 0 and w // 2 <= _POOL_EDGE_ROWS for w in _POOL_WINDOWS)

    inv = 1.0 / (_ROPE_BASE ** np.linspace(0.0, 1.0, dh // 2))
    ang = np.arange(seq)[:, None] * inv[None, :]
    cos, sin = jnp.asarray(np.cos(ang), _F32), jnp.asarray(np.sin(ang), _F32)

    xf = x.reshape(bsz * seq, d)
    for l in range(depth):
        mod3 = _adaln(c, w_ada[l], b_ada[l]).reshape(bsz, _N_MOD, d)
        proj, wo, wd, wgu = _inproj(xf, mod3, norm1_g[l].reshape(1, d), cos, sin, w_in[l].astype(_BF16), seq,
                                    pool_width, [w_out[l], w_down[l]], [w_gate[l], w_up[l]])
        spread = lambda p: jnp.broadcast_to(p.astype(_F32)[:, None, None], (heads, 1, _RET_CHUNK))
        pool_out, ret_out = _mixer(
            proj, pool_w[l].astype(_BF16), pool_scale[l].reshape(len(_POOL_WINDOWS), 1, dh),
            spread(ret_decay_fwd[l]), spread(ret_decay_bwd[l]), bsz, seq)
        x1, h2 = _outproj(pool_out, ret_out, xf, mod3, norm2_g[l].reshape(1, d), wo, seq)
        xf = _ffn(h2, x1, mod3, wgu, wd, final_g.reshape(1, d), seq, final_norm=(l == depth - 1))
    return xf.reshape(bsz, seq, d)
```

```python
import functools

import jax
import jax.numpy as jnp
import numpy as np
from jax import lax
from jax.experimental import pallas as pl
from jax.experimental.pallas import tpu as pltpu

_EPS = 1e-6
_POOL_WINDOWS = (2, 4, 8, 16)
_RET_HEADS = 4
_ROPE_BASE = 10000.0
_N_MOD = 6

_V7X_VMEM_BYTES = 64 * 1024 * 1024
_VMEM_UNCOUNTED_BYTES = 8 * 1024 * 1024
_VMEM_RESERVED_BYTES = 2 * 1024 * 1024
_LANES = 128
_BF16_ROWS = 16
_RET_CHUNK = 256
_POOL_HALO = 128
_POOL_EDGE_ROWS = 8
_NORM_ROWS = 128
_DOT_ROWS = 256
_MIXER_HEADS_PER_STEP = 2

_F32 = jnp.float32
_BF16 = jnp.bfloat16


def _vmem_limit(nbytes):
    return int(min(_V7X_VMEM_BYTES - _VMEM_RESERVED_BYTES, nbytes + _VMEM_UNCOUNTED_BYTES))


def _nbytes(shape, dtype):
    n = 1
    for s in shape:
        n *= s
    return n * jnp.dtype(dtype).itemsize


def _dot(a, b):
    return jnp.dot(a, b, preferred_element_type=_F32)


def _silu(x):
    return x * jax.nn.sigmoid(x)


def _rms_rows(x):
    return x * lax.rsqrt(jnp.mean(x * x, axis=-1, keepdims=True) + _EPS)


def _col_blocks(d):
    return [slice(c0, c0 + _LANES) for c0 in range(0, d, _LANES)]


def _rmsnorm_rows(src_ref, dst_ref, row0, nrows, gain, shift=None):
    d = src_ref.shape[1]
    for r0 in range(row0, row0 + nrows, _NORM_ROWS):
        rows = slice(r0, r0 + _NORM_ROWS)
        ss = None
        for cols in _col_blocks(d):
            xc = src_ref[rows, cols]
            ss = xc * xc if ss is None else ss + xc * xc
        rs = lax.rsqrt(jnp.sum(ss, axis=-1, keepdims=True) / d + _EPS)
        for cols in _col_blocks(d):
            y = src_ref[rows, cols] * rs * gain[:, cols]
            if shift is not None:
                y = y + shift[:, cols]
            dst_ref[rows, cols] = y.astype(dst_ref.dtype)


def _adaln_kernel(c_ref, w_ref, b_ref, o_ref):
    c = c_ref[...]
    o_ref[...] = _dot(_silu(c).astype(_BF16), w_ref[...].astype(_BF16)) + b_ref[...]


def _adaln(c, w, b, *, tn=2048):
    bsz, d = c.shape
    n = w.shape[1]
    vmem = 2 * _nbytes((d, tn), _F32) + _nbytes((d, tn), _BF16) + 4 * _nbytes((bsz, d), _F32)
    return pl.pallas_call(
        _adaln_kernel,
        grid=(n // tn,),
        in_specs=[
            pl.BlockSpec((bsz, d), lambda j: (0, 0)),
            pl.BlockSpec((d, tn), lambda j: (0, j)),
            pl.BlockSpec((1, tn), lambda j: (0, j)),
        ],
        out_specs=pl.BlockSpec((bsz, tn), lambda j: (0, j)),
        out_shape=jax.ShapeDtypeStruct((bsz, n), _F32),
        compiler_params=pltpu.CompilerParams(
            dimension_semantics=("arbitrary",), vmem_limit_bytes=_vmem_limit(vmem)),
        name="adaln",
    )(c, w, b.reshape(1, n))


def _slab_specs(weights, steps, index_map):
    specs, shapes, nbytes = [], [], 0
    for w in weights:
        rows, cols = w.shape
        assert rows % steps == 0 and (rows // steps) % _BF16_ROWS == 0
        specs.append(pl.BlockSpec((rows // steps, cols), index_map))
        shapes.append(jax.ShapeDtypeStruct(w.shape, _BF16))
        nbytes += 2 * _nbytes((rows // steps, cols), _F32) + 2 * _nbytes((rows // steps, cols), _BF16)
    return specs, shapes, nbytes


def _cast_slabs(src_refs, dst_refs):
    for src, dst in zip(src_refs, dst_refs):
        dst[...] = src[...].astype(dst.dtype)


def _cast_gate_up(wg_ref, wu_ref, wgu_ref):
    for g in range(wg_ref.shape[1] // _LANES):
        cols = slice(g * _LANES, (g + 1) * _LANES)
        wgu_ref[:, 2 * g * _LANES:(2 * g + 1) * _LANES] = wg_ref[:, cols].astype(wgu_ref.dtype)
        wgu_ref[:, (2 * g + 1) * _LANES:(2 * g + 2) * _LANES] = wu_ref[:, cols].astype(wgu_ref.dtype)


def _inproj_kernel(x_ref, mod_ref, gain_ref, cos_ref, sin_ref, w_ref, *rest, n_cast, pool_width, heads):
    cast_in, (wg_ref, wu_ref) = rest[:n_cast], rest[n_cast:n_cast + 2]
    o_ref, *cast_out, wgu_ref, h_ref = rest[n_cast + 2:]
    width = (o_ref.shape[1] - pool_width) // 4
    dh = width // heads
    half = dh // 2
    scale = gain_ref[...] * (1.0 + mod_ref[0, 1:2, :])
    shift = mod_ref[0, 0:1, :]
    for r0 in range(0, x_ref.shape[0], _DOT_ROWS):
        rows = slice(r0, r0 + _DOT_ROWS)
        _rmsnorm_rows(x_ref, h_ref, r0, _DOT_ROWS, scale, shift)
        h = h_ref[rows, :]
        c0 = pool_width + 3 * width
        o_ref[rows, c0:c0 + width] = _silu(_dot(h, w_ref[:, c0:c0 + width])).astype(o_ref.dtype)
        cos, sin = cos_ref[rows, :], sin_ref[rows, :]
        for part in range(2):
            c0 = pool_width + part * width
            t = _dot(h, w_ref[:, c0:c0 + width])
            for hd in range(heads):
                t1 = t[:, hd * dh:hd * dh + half]
                t2 = t[:, hd * dh + half:(hd + 1) * dh]
                o_ref[rows, c0 + hd * dh:c0 + hd * dh + half] = (t1 * cos - t2 * sin).astype(o_ref.dtype)
                o_ref[rows, c0 + hd * dh + half:c0 + (hd + 1) * dh] = (t1 * sin + t2 * cos).astype(o_ref.dtype)
        if r0 == 0:
            _cast_slabs(cast_in, cast_out)
            _cast_gate_up(wg_ref, wu_ref, wgu_ref)
        c0 = pool_width + 2 * width
        o_ref[rows, c0:c0 + width] = _dot(h, w_ref[:, c0:c0 + width]).astype(o_ref.dtype)
        o_ref[rows, 0:pool_width] = _dot(h, w_ref[:, 0:pool_width]).astype(o_ref.dtype)


def _inproj(x2d, mod3, gain, cos, sin, w, seq, pool_width, cast_weights, gate_up, *, tm=512):
    t, d = x2d.shape
    n = w.shape[1]
    per_batch = seq // tm
    steps = t // tm
    cast_specs, cast_shapes, cast_bytes = _slab_specs(cast_weights, steps, lambda i: (i, 0))
    gu_specs, _, gu_bytes = _slab_specs(gate_up, steps, lambda i: (i, 0))
    gu_rows, dff = gate_up[0].shape
    assert gate_up[1].shape == (gu_rows, dff) and dff % _LANES == 0
    vmem = (2 * _nbytes((tm, d), _F32) + _nbytes((d, n), _BF16) + 2 * _nbytes((tm, n), _BF16)
            + _nbytes((tm, d), _BF16) + _nbytes((_DOT_ROWS, n), _F32) + 4 * _nbytes(cos.shape, _F32) // per_batch
            + cast_bytes + gu_bytes)
    return pl.pallas_call(
        functools.partial(_inproj_kernel, n_cast=len(cast_weights), pool_width=pool_width, heads=_RET_HEADS),
        grid=(t // tm,),
        in_specs=[
            pl.BlockSpec((tm, d), lambda i: (i, 0)),
            pl.BlockSpec((1, _N_MOD, d), lambda i: (i // per_batch, 0, 0)),
            pl.BlockSpec((1, d), lambda i: (0, 0)),
            pl.BlockSpec((tm, cos.shape[1]), lambda i: (i % per_batch, 0)),
            pl.BlockSpec((tm, sin.shape[1]), lambda i: (i % per_batch, 0)),
            pl.BlockSpec((d, n), lambda i: (0, 0), pipeline_mode=pl.Buffered(1)),
        ] + cast_specs + gu_specs,
        out_specs=([pl.BlockSpec((tm, n), lambda i: (i, 0))] + cast_specs
                   + [pl.BlockSpec((gu_rows // steps, 2 * dff), lambda i: (i, 0))]),
        out_shape=([jax.ShapeDtypeStruct((t, n), _BF16)] + cast_shapes
                   + [jax.ShapeDtypeStruct((gu_rows, 2 * dff), _BF16)]),
        scratch_shapes=[pltpu.VMEM((tm, d), _BF16)],
        compiler_params=pltpu.CompilerParams(
            dimension_semantics=("arbitrary",), vmem_limit_bytes=_vmem_limit(vmem)),
        name="inproj",
    )(x2d, mod3, gain, cos, sin, w, *cast_weights, *gate_up)


def _pool_chunks(u_ref, pad_ref, cols, group, seq):
    c = _RET_CHUNK
    n = seq // c
    halo = _POOL_HALO
    half = jnp.int32(0)
    inv_w = jnp.float32(0)
    for gi, window in enumerate(_POOL_WINDOWS):
        half = jnp.where(group == gi, jnp.int32(window // 2), half)
        inv_w = jnp.where(group == gi, jnp.float32(1.0 / window), inv_w)

    width_c = cols.stop - cols.start
    zeros = jnp.zeros((halo, width_c), pad_ref.dtype)
    pad_ref[0:halo, cols] = zeros
    pad_ref[halo + seq:, cols] = zeros
    pad_ref[halo:halo + seq, cols] = u_ref[:, cols]

    rel = (lax.broadcasted_iota(jnp.int32, (c, c + 2 * halo), 1) - halo
           - lax.broadcasted_iota(jnp.int32, (c, c + 2 * halo), 0))
    band = (jnp.where((rel >= -half) & (rel < half), inv_w, 0.0)
            - jnp.where(rel == 0, 1.0, 0.0)).astype(_BF16)

    edge = _POOL_EDGE_ROWS
    t_lo = lax.broadcasted_iota(jnp.int32, (edge, width_c), 0)
    t_hi = t_lo + (seq - edge)
    width = (2 * half).astype(_F32)
    ratio_lo = width / (t_lo + half - jnp.maximum(t_lo - half, 0)).astype(_F32)
    ratio_hi = width / (jnp.minimum(t_hi + half, seq) - (t_hi - half)).astype(_F32)

    def pooled_chunk(ci):
        pooled = _dot(band, pad_ref[ci * c:ci * c + c + 2 * halo, cols])
        if ci == 0:
            u_lo = u_ref[0:edge, cols].astype(_F32)
            pooled = jnp.concatenate([(pooled[0:edge] + u_lo) * ratio_lo - u_lo, pooled[edge:]], axis=0)
        if ci == n - 1:
            u_hi = u_ref[seq - edge:seq, cols].astype(_F32)
            pooled = jnp.concatenate([pooled[:c - edge], (pooled[c - edge:] + u_hi) * ratio_hi - u_hi], axis=0)
        return pooled.astype(_BF16)

    return pooled_chunk


def _mixer_kernel(u_ref, q_ref, k_ref, v_ref, gate_ref, pw_ref, ps_ref, df_ref, db_ref, pool_ref, ret_ref,
                  pad_ref, pooled_ref, kb_ref, acc_ref, dec_ref, sf_ref, sb_ref):
    seq = q_ref.shape[0]
    nh, dh = pw_ref.shape[0], pw_ref.shape[-1]
    c = _RET_CHUNK
    n = seq // c
    head_cols = [slice(hd * dh, (hd + 1) * dh) for hd in range(nh)]

    def tn_dot(a, b):
        return lax.dot_general(a, b, (((0,), (0,)), ((), ())), preferred_element_type=_F32)

    def nt_dot(a, b):
        return lax.dot_general(a, b, (((1,), (1,)), ((), ())), preferred_element_type=_F32)

    ri = lax.broadcasted_iota(jnp.int32, (c, c), 0).astype(_F32)
    cj = lax.broadcasted_iota(jnp.int32, (c, c), 1).astype(_F32)
    diff = ri - cj
    qk_scale = dh ** -0.5
    pooled_chunk, chunk_f, chunk_b = [], [], []
    for hd, cols in enumerate(head_cols):
        pooled_chunk.append(_pool_chunks(u_ref, pad_ref, cols, pl.program_id(1) * nh + hd, seq))
        head = pl.program_id(1) * nh + hd
        lg_f = -jnp.exp(jnp.full((1, c), df_ref[head], _F32))
        lg_b = -jnp.exp(jnp.full((1, c), db_ref[head], _F32))
        dec_ref[hd, 0] = jnp.exp(jnp.where(diff >= 0, diff * lg_f, -diff * lg_b)) * qk_scale
        dec_ref[hd, 1] = jnp.exp((ri + 1.0) * lg_f) * qk_scale
        dec_ref[hd, 2] = jnp.exp((c - 1.0 - ri) * lg_f)
        dec_ref[hd, 3] = jnp.exp((c - ri) * lg_b) * qk_scale
        dec_ref[hd, 4] = jnp.exp(ri * lg_b)
        chunk_f.append(jnp.exp(c * lg_f))
        chunk_b.append(jnp.exp(c * lg_b))

    for ci in range(n):
        rows = slice(ci * c, (ci + 1) * c)
        for hd, cols in enumerate(head_cols):
            q = q_ref[rows, cols]
            k = k_ref[rows, cols]
            k32 = k.astype(_F32)
            if ci > 0:
                kb_ref[rows, cols] = (k32 * dec_ref[hd, 4]).astype(_BF16)
            v = v_ref[rows, cols]
            scores = nt_dot(q, k) * dec_ref[hd, 0]
            out = _dot(scores.astype(_BF16), v)
            if ci > 0:
                out = out + _dot(q, sf_ref[hd].astype(_BF16)) * dec_ref[hd, 1]
            acc_ref[rows, cols] = out
            if ci < n - 1:
                upd = tn_dot((k32 * dec_ref[hd, 2]).astype(_BF16), v)
                sf_ref[hd] = upd if ci == 0 else sf_ref[hd] * chunk_f[hd] + upd
            pooled_ref[rows, cols] = pooled_chunk[hd](ci)

    for hd, cols in enumerate(head_cols):
        pool_ref[:, cols] = (_dot(pooled_ref[:, cols], pw_ref[hd]) * ps_ref[hd]).astype(pool_ref.dtype)

    for ci in reversed(range(n)):
        rows = slice(ci * c, (ci + 1) * c)
        for hd, cols in enumerate(head_cols):
            out = acc_ref[rows, cols]
            if ci < n - 1:
                out = out + _dot(q_ref[rows, cols], sb_ref[hd].astype(_BF16)) * dec_ref[hd, 3]
            ret = _rms_rows(out)
            ret_ref[rows, cols] = (gate_ref[rows, cols].astype(_F32) * ret).astype(ret_ref.dtype)
            if ci > 0:
                upd = tn_dot(kb_ref[rows, cols], v_ref[rows, cols])
                sb_ref[hd] = upd if ci == n - 1 else sb_ref[hd] * chunk_b[hd] + upd


def _mixer(proj, pool_w, pool_scale, dec_f, dec_b, bsz, seq):
    t = proj.shape[0]
    heads = _RET_HEADS
    dh = pool_w.shape[-1]
    pool_blocks = len(_POOL_WINDOWS)
    nh = _MIXER_HEADS_PER_STEP
    assert heads % nh == 0 and pool_blocks == heads
    wide = nh * dh
    seq_block = lambda first: pl.BlockSpec((seq, wide), lambda b, j: (b, first // nh + j))
    per_head = lambda shape: pl.BlockSpec((nh,) + shape, lambda b, j: (j, 0, 0))
    scratch = [
        pltpu.VMEM((seq + 2 * _POOL_HALO, wide), _BF16),
        pltpu.VMEM((seq, wide), _BF16),
        pltpu.VMEM((seq, wide), _BF16),
        pltpu.VMEM((seq, wide), _F32),
        pltpu.VMEM((nh, 5, _RET_CHUNK, _RET_CHUNK), _F32),
        pltpu.VMEM((nh, dh, dh), _F32),
        pltpu.VMEM((nh, dh, dh), _F32),
    ]
    vmem = (2 * 7 * _nbytes((seq, wide), _BF16) + 4 * _nbytes((seq, wide), _F32)
            + 8 * nh * _nbytes((dh, dh), _F32))
    return pl.pallas_call(
        _mixer_kernel,
        grid=(bsz, heads // nh),
        in_specs=[
            seq_block(0),
            seq_block(pool_blocks),
            seq_block(pool_blocks + heads),
            seq_block(pool_blocks + 2 * heads),
            seq_block(pool_blocks + 3 * heads),
            per_head((dh, dh)),
            per_head((1, dh)),
            pl.BlockSpec(memory_space=pltpu.SMEM),
            pl.BlockSpec(memory_space=pltpu.SMEM),
        ],
        out_specs=[pl.BlockSpec((seq, wide), lambda b, j: (b, j)),
                   pl.BlockSpec((seq, wide), lambda b, j: (b, j))],
        out_shape=[jax.ShapeDtypeStruct((t, pool_blocks * dh), _BF16),
                   jax.ShapeDtypeStruct((t, heads * dh), _BF16)],
        scratch_shapes=scratch,
        compiler_params=pltpu.CompilerParams(
            dimension_semantics=("arbitrary", "arbitrary"), vmem_limit_bytes=_vmem_limit(vmem)),
        name="mixer",
    )(proj, proj, proj, proj, proj, pool_w, pool_scale, dec_f, dec_b)


def _outproj_kernel(p_ref, r_ref, x_ref, mod_ref, gain_ref, w_ref, x1_ref, h2_ref):
    kp = p_ref.shape[1]
    gate = mod_ref[0, 2:3, :]
    scale = gain_ref[...] * (1.0 + mod_ref[0, 4:5, :])
    shift = mod_ref[0, 3:4, :]
    for r0 in range(0, x_ref.shape[0], _DOT_ROWS):
        rows = slice(r0, r0 + _DOT_ROWS)
        mixed = _dot(p_ref[rows, :], w_ref[0:kp, :]) + _dot(r_ref[rows, :], w_ref[kp:, :])
        x1_ref[rows, :] = x_ref[rows, :] + gate * mixed
        _rmsnorm_rows(x1_ref, h2_ref, r0, _DOT_ROWS, scale, shift)


def _outproj(pool_out, ret_out, x2d, mod3, gain, w, seq, *, tm=512):
    t, d = x2d.shape
    kp, kr = pool_out.shape[1], ret_out.shape[1]
    per_batch = seq // tm
    vmem = (2 * _nbytes((tm, kp + kr), _BF16) + 4 * _nbytes((tm, d), _F32) + 2 * _nbytes((tm, d), _BF16)
            + _nbytes((kp + kr, d), _BF16) + _nbytes((_DOT_ROWS, d), _F32))
    return pl.pallas_call(
        _outproj_kernel,
        grid=(t // tm,),
        in_specs=[
            pl.BlockSpec((tm, kp), lambda i: (i, 0)),
            pl.BlockSpec((tm, kr), lambda i: (i, 0)),
            pl.BlockSpec((tm, d), lambda i: (i, 0)),
            pl.BlockSpec((1, _N_MOD, d), lambda i: (i // per_batch, 0, 0)),
            pl.BlockSpec((1, d), lambda i: (0, 0)),
            pl.BlockSpec((kp + kr, d), lambda i: (0, 0), pipeline_mode=pl.Buffered(1)),
        ],
        out_specs=[pl.BlockSpec((tm, d), lambda i: (i, 0)),
                   pl.BlockSpec((tm, d), lambda i: (i, 0))],
        out_shape=[jax.ShapeDtypeStruct((t, d), _F32),
                   jax.ShapeDtypeStruct((t, d), _BF16)],
        compiler_params=pltpu.CompilerParams(
            dimension_semantics=("arbitrary",), vmem_limit_bytes=_vmem_limit(vmem)),
        name="outproj",
    )(pool_out, ret_out, x2d, mod3, gain, w)


def _ffn_kernel(h2_ref, wgu_ref, wd_ref, x1_hbm, mod_ref, gain_ref, o_ref, x1_buf, sem, *, final_norm):
    i, f = pl.program_id(0), pl.program_id(1)
    tm, d = o_ref.shape

    def x1_copy():
        return pltpu.make_async_copy(x1_hbm.at[pl.ds(pl.multiple_of(i * tm, tm), tm), :], x1_buf, sem.at[0])

    last = pl.num_programs(1) - 1

    def hidden():
        gu = _dot(h2_ref[...], wgu_ref[...])
        parts = [_silu(gu[:, c0:c0 + _LANES]) * gu[:, c0 + _LANES:c0 + 2 * _LANES]
                 for c0 in range(0, gu.shape[1], 2 * _LANES)]
        return jnp.concatenate(parts, axis=1).astype(_BF16)

    def finish_rows(r0):
        rows = slice(r0, r0 + _NORM_ROWS)
        gate = mod_ref[0, 5:6, :]
        gain = gain_ref[...]
        ss = None
        for cols in _col_blocks(d):
            x2 = x1_buf[rows, cols] + gate[:, cols] * o_ref[rows, cols]
            o_ref[rows, cols] = x2
            ss = x2 * x2 if ss is None else ss + x2 * x2
        if final_norm:
            rs = lax.rsqrt(jnp.sum(ss, axis=-1, keepdims=True) / d + _EPS)
            for cols in _col_blocks(d):
                o_ref[rows, cols] = o_ref[rows, cols] * rs * gain[:, cols]

    @pl.when(f == 0)
    def _():
        x1_copy().start()
        o_ref[...] = _dot(hidden(), wd_ref[...])

    @pl.when((f > 0) & (f < last))
    def _():
        o_ref[...] += _dot(hidden(), wd_ref[...])

    @pl.when(f == last)
    def _():
        x1_copy().wait()
        hid = hidden()
        for r0 in range(0, tm, _DOT_ROWS):
            rows = slice(r0, r0 + _DOT_ROWS)
            o_ref[rows, :] += _dot(hid[rows, :], wd_ref[...])
            for s0 in range(r0, r0 + _DOT_ROWS, _NORM_ROWS):
                finish_rows(s0)


def _ffn(h2, x1, mod3, wgu, wd, final_gain, seq, *, final_norm, tm=1024, tf=512):
    t, d = x1.shape
    dff = wd.shape[0]
    assert wgu.shape == (d, 2 * dff) and tf % _LANES == 0
    assert dff // tf >= 2
    per_batch = seq // tm
    vmem = (2 * _nbytes((tm, d), _BF16) + 6 * _nbytes((d, tf), _BF16) + 3 * _nbytes((tm, d), _F32)
            + 3 * _nbytes((tm, tf), _F32))
    return pl.pallas_call(
        functools.partial(_ffn_kernel, final_norm=final_norm),
        grid=(t // tm, dff // tf),
        in_specs=[
            pl.BlockSpec((tm, d), lambda i, f: (i, 0)),
            pl.BlockSpec((d, 2 * tf), lambda i, f: (0, f)),
            pl.BlockSpec((tf, d), lambda i, f: (f, 0)),
            pl.BlockSpec(memory_space=pl.ANY),
            pl.BlockSpec((1, _N_MOD, d), lambda i, f: (i // per_batch, 0, 0)),
            pl.BlockSpec((1, d), lambda i, f: (0, 0)),
        ],
        out_specs=pl.BlockSpec((tm, d), lambda i, f: (i, 0)),
        out_shape=jax.ShapeDtypeStruct((t, d), _F32),
        scratch_shapes=[pltpu.VMEM((tm, d), _F32), pltpu.SemaphoreType.DMA((1,))],
        compiler_params=pltpu.CompilerParams(
            dimension_semantics=("arbitrary", "arbitrary"), vmem_limit_bytes=_vmem_limit(vmem)),
        name="ffn",
    )(h2, wgu, wd, x1, mod3, final_gain)


def kernel(x, c, w_ada, b_ada, norm1_g, w_in, pool_w, pool_scale, ret_decay_fwd, ret_decay_bwd,
           w_out, norm2_g, w_gate, w_up, w_down, final_g):
    bsz, seq, d = x.shape
    depth = w_ada.shape[0]
    heads = _RET_HEADS
    pool_width = pool_scale.shape[1]
    dh = (w_in.shape[2] - pool_width) // (4 * heads)
    assert pool_width == len(_POOL_WINDOWS) * dh and seq % _RET_CHUNK == 0 and dh == _RET_CHUNK
    assert all(w & (w - 1) == 0 and w // 2 <= _POOL_EDGE_ROWS for w in _POOL_WINDOWS)

    inv = 1.0 / (_ROPE_BASE ** np.linspace(0.0, 1.0, dh // 2))
    ang = np.arange(seq)[:, None] * inv[None, :]
    cos, sin = jnp.asarray(np.cos(ang), _F32), jnp.asarray(np.sin(ang), _F32)

    xf = x.reshape(bsz * seq, d)
    for l in range(depth):
        mod3 = _adaln(c, w_ada[l], b_ada[l]).reshape(bsz, _N_MOD, d)
        proj, wo, wd, wgu = _inproj(xf, mod3, norm1_g[l].reshape(1, d), cos, sin, w_in[l].astype(_BF16), seq,
                                    pool_width, [w_out[l], w_down[l]], [w_gate[l], w_up[l]])
        pool_out, ret_out = _mixer(
            proj, pool_w[l].astype(_BF16), pool_scale[l].reshape(len(_POOL_WINDOWS), 1, dh),
            ret_decay_fwd[l].astype(_F32), ret_decay_bwd[l].astype(_F32), bsz, seq)
        x1, h2 = _outproj(pool_out, ret_out, xf, mod3, norm2_g[l].reshape(1, d), wo, seq)
        xf = _ffn(h2, x1, mod3, wgu, wd, final_g.reshape(1, d), seq, final_norm=(l == depth - 1))
    return xf.reshape(bsz, seq, d)
```

```python
import functools

import jax
import jax.numpy as jnp
import numpy as np
from jax import lax
from jax.experimental import pallas as pl
from jax.experimental.pallas import tpu as pltpu

_EPS = 1e-6
_POOL_WINDOWS = (2, 4, 8, 16)
_RET_HEADS = 4
_ROPE_BASE = 10000.0
_N_MOD = 6

_V7X_VMEM_BYTES = 64 * 1024 * 1024
_VMEM_UNCOUNTED_BYTES = 8 * 1024 * 1024
_VMEM_RESERVED_BYTES = 2 * 1024 * 1024
_LANES = 128
_BF16_ROWS = 16
_RET_CHUNK = 256
_POOL_HALO = 128
_POOL_EDGE_ROWS = 8
_NORM_ROWS = 128
_DOT_ROWS = 256
_MIXER_HEADS_PER_STEP = 2

_F32 = jnp.float32
_BF16 = jnp.bfloat16


def _vmem_limit(nbytes):
    return int(min(_V7X_VMEM_BYTES - _VMEM_RESERVED_BYTES, nbytes + _VMEM_UNCOUNTED_BYTES))


def _nbytes(shape, dtype):
    n = 1
    for s in shape:
        n *= s
    return n * jnp.dtype(dtype).itemsize


def _dot(a, b):
    return jnp.dot(a, b, preferred_element_type=_F32)


def _silu(x):
    return x * jax.nn.sigmoid(x)


def _rms_rows(x):
    return x * lax.rsqrt(jnp.mean(x * x, axis=-1, keepdims=True) + _EPS)


def _col_blocks(d):
    return [slice(c0, c0 + _LANES) for c0 in range(0, d, _LANES)]


def _rmsnorm_rows(src_ref, dst_ref, row0, nrows, gain, shift=None):
    d = src_ref.shape[1]
    for r0 in range(row0, row0 + nrows, _NORM_ROWS):
        rows = slice(r0, r0 + _NORM_ROWS)
        ss = None
        for cols in _col_blocks(d):
            xc = src_ref[rows, cols]
            ss = xc * xc if ss is None else ss + xc * xc
        rs = lax.rsqrt(jnp.sum(ss, axis=-1, keepdims=True) / d + _EPS)
        for cols in _col_blocks(d):
            y = src_ref[rows, cols] * rs * gain[:, cols]
            if shift is not None:
                y = y + shift[:, cols]
            dst_ref[rows, cols] = y.astype(dst_ref.dtype)


def _adaln_kernel(c_ref, w_ref, b_ref, o_ref):
    c = c_ref[...]
    o_ref[...] = _dot(_silu(c).astype(_BF16), w_ref[...].astype(_BF16)) + b_ref[...]


def _adaln(c, w, b, *, tn=1024):
    bsz, d = c.shape
    n = w.shape[1]
    vmem = 2 * _nbytes((d, tn), _F32) + _nbytes((d, tn), _BF16) + 4 * _nbytes((bsz, d), _F32)
    return pl.pallas_call(
        _adaln_kernel,
        grid=(n // tn,),
        in_specs=[
            pl.BlockSpec((bsz, d), lambda j: (0, 0)),
            pl.BlockSpec((d, tn), lambda j: (0, j)),
            pl.BlockSpec((1, tn), lambda j: (0, j)),
        ],
        out_specs=pl.BlockSpec((bsz, tn), lambda j: (0, j)),
        out_shape=jax.ShapeDtypeStruct((bsz, n), _F32),
        compiler_params=pltpu.CompilerParams(
            dimension_semantics=("arbitrary",), vmem_limit_bytes=_vmem_limit(vmem)),
        name="adaln",
    )(c, w, b.reshape(1, n))


def _slab_specs(weights, steps, index_map):
    specs, shapes, nbytes = [], [], 0
    for w in weights:
        rows, cols = w.shape
        assert rows % steps == 0 and (rows // steps) % _BF16_ROWS == 0
        specs.append(pl.BlockSpec((rows // steps, cols), index_map))
        shapes.append(jax.ShapeDtypeStruct(w.shape, _BF16))
        nbytes += 2 * _nbytes((rows // steps, cols), _F32) + 2 * _nbytes((rows // steps, cols), _BF16)
    return specs, shapes, nbytes


def _cast_slabs(src_refs, dst_refs):
    for src, dst in zip(src_refs, dst_refs):
        dst[...] = src[...].astype(dst.dtype)


def _cast_gate_up(wg_ref, wu_ref, wgu_ref):
    for g in range(wg_ref.shape[1] // _LANES):
        cols = slice(g * _LANES, (g + 1) * _LANES)
        wgu_ref[:, 2 * g * _LANES:(2 * g + 1) * _LANES] = wg_ref[:, cols].astype(wgu_ref.dtype)
        wgu_ref[:, (2 * g + 1) * _LANES:(2 * g + 2) * _LANES] = wu_ref[:, cols].astype(wgu_ref.dtype)


def _inproj_kernel(x_ref, mod_ref, gain_ref, cos_ref, sin_ref, w_ref, *rest, n_cast, pool_width, heads):
    cast_in, (wg_ref, wu_ref) = rest[:n_cast], rest[n_cast:n_cast + 2]
    o_ref, *cast_out, wgu_ref, h_ref = rest[n_cast + 2:]
    width = (o_ref.shape[1] - pool_width) // 4
    dh = width // heads
    half = dh // 2
    scale = gain_ref[...] * (1.0 + mod_ref[0, 1:2, :])
    shift = mod_ref[0, 0:1, :]
    for r0 in range(0, x_ref.shape[0], _DOT_ROWS):
        rows = slice(r0, r0 + _DOT_ROWS)
        _rmsnorm_rows(x_ref, h_ref, r0, _DOT_ROWS, scale, shift)
        h = h_ref[rows, :]
        c0 = pool_width + 3 * width
        o_ref[rows, c0:c0 + width] = _silu(_dot(h, w_ref[:, c0:c0 + width])).astype(o_ref.dtype)
        cos, sin = cos_ref[rows, :], sin_ref[rows, :]
        for part in range(2):
            c0 = pool_width + part * width
            t = _dot(h, w_ref[:, c0:c0 + width])
            for hd in range(heads):
                t1 = t[:, hd * dh:hd * dh + half]
                t2 = t[:, hd * dh + half:(hd + 1) * dh]
                o_ref[rows, c0 + hd * dh:c0 + hd * dh + half] = (t1 * cos - t2 * sin).astype(o_ref.dtype)
                o_ref[rows, c0 + hd * dh + half:c0 + (hd + 1) * dh] = (t1 * sin + t2 * cos).astype(o_ref.dtype)
        if r0 == 0:
            _cast_slabs(cast_in, cast_out)
            _cast_gate_up(wg_ref, wu_ref, wgu_ref)
        c0 = pool_width + 2 * width
        o_ref[rows, c0:c0 + width] = _dot(h, w_ref[:, c0:c0 + width]).astype(o_ref.dtype)
        o_ref[rows, 0:pool_width] = _dot(h, w_ref[:, 0:pool_width]).astype(o_ref.dtype)


def _inproj(x2d, mod3, gain, cos, sin, w, seq, pool_width, cast_weights, gate_up, *, tm=512):
    t, d = x2d.shape
    n = w.shape[1]
    per_batch = seq // tm
    steps = t // tm
    cast_specs, cast_shapes, cast_bytes = _slab_specs(cast_weights, steps, lambda i: (i, 0))
    gu_specs, _, gu_bytes = _slab_specs(gate_up, steps, lambda i: (i, 0))
    gu_rows, dff = gate_up[0].shape
    assert gate_up[1].shape == (gu_rows, dff) and dff % _LANES == 0
    vmem = (2 * _nbytes((tm, d), _F32) + _nbytes((d, n), _BF16) + 2 * _nbytes((tm, n), _BF16)
            + _nbytes((tm, d), _BF16) + _nbytes((_DOT_ROWS, n), _F32) + 4 * _nbytes(cos.shape, _F32) // per_batch
            + cast_bytes + gu_bytes)
    return pl.pallas_call(
        functools.partial(_inproj_kernel, n_cast=len(cast_weights), pool_width=pool_width, heads=_RET_HEADS),
        grid=(t // tm,),
        in_specs=[
            pl.BlockSpec((tm, d), lambda i: (i, 0)),
            pl.BlockSpec((1, _N_MOD, d), lambda i: (i // per_batch, 0, 0)),
            pl.BlockSpec((1, d), lambda i: (0, 0)),
            pl.BlockSpec((tm, cos.shape[1]), lambda i: (i % per_batch, 0)),
            pl.BlockSpec((tm, sin.shape[1]), lambda i: (i % per_batch, 0)),
            pl.BlockSpec((d, n), lambda i: (0, 0), pipeline_mode=pl.Buffered(1)),
        ] + cast_specs + gu_specs,
        out_specs=([pl.BlockSpec((tm, n), lambda i: (i, 0))] + cast_specs
                   + [pl.BlockSpec((gu_rows // steps, 2 * dff), lambda i: (i, 0))]),
        out_shape=([jax.ShapeDtypeStruct((t, n), _BF16)] + cast_shapes
                   + [jax.ShapeDtypeStruct((gu_rows, 2 * dff), _BF16)]),
        scratch_shapes=[pltpu.VMEM((tm, d), _BF16)],
        compiler_params=pltpu.CompilerParams(
            dimension_semantics=("arbitrary",), vmem_limit_bytes=_vmem_limit(vmem)),
        name="inproj",
    )(x2d, mod3, gain, cos, sin, w, *cast_weights, *gate_up)


def _pool_chunks(u_ref, pad_ref, cols, group, seq):
    c = _RET_CHUNK
    n = seq // c
    halo = _POOL_HALO
    half = jnp.int32(0)
    inv_w = jnp.float32(0)
    for gi, window in enumerate(_POOL_WINDOWS):
        half = jnp.where(group == gi, jnp.int32(window // 2), half)
        inv_w = jnp.where(group == gi, jnp.float32(1.0 / window), inv_w)

    width_c = cols.stop - cols.start
    zeros = jnp.zeros((halo, width_c), pad_ref.dtype)
    pad_ref[0:halo, cols] = zeros
    pad_ref[halo + seq:, cols] = zeros
    pad_ref[halo:halo + seq, cols] = u_ref[:, cols]

    rel = (lax.broadcasted_iota(jnp.int32, (c, c + 2 * halo), 1) - halo
           - lax.broadcasted_iota(jnp.int32, (c, c + 2 * halo), 0))
    band = (jnp.where((rel >= -half) & (rel < half), inv_w, 0.0)
            - jnp.where(rel == 0, 1.0, 0.0)).astype(_BF16)

    edge = _POOL_EDGE_ROWS
    t_lo = lax.broadcasted_iota(jnp.int32, (edge, width_c), 0)
    t_hi = t_lo + (seq - edge)
    width = (2 * half).astype(_F32)
    ratio_lo = width / (t_lo + half - jnp.maximum(t_lo - half, 0)).astype(_F32)
    ratio_hi = width / (jnp.minimum(t_hi + half, seq) - (t_hi - half)).astype(_F32)

    def pooled_chunk(ci):
        pooled = _dot(band, pad_ref[ci * c:ci * c + c + 2 * halo, cols])
        if ci == 0:
            u_lo = u_ref[0:edge, cols].astype(_F32)
            pooled = jnp.concatenate([(pooled[0:edge] + u_lo) * ratio_lo - u_lo, pooled[edge:]], axis=0)
        if ci == n - 1:
            u_hi = u_ref[seq - edge:seq, cols].astype(_F32)
            pooled = jnp.concatenate([pooled[:c - edge], (pooled[c - edge:] + u_hi) * ratio_hi - u_hi], axis=0)
        return pooled.astype(_BF16)

    return pooled_chunk


def _mixer_kernel(u_ref, q_ref, k_ref, v_ref, gate_ref, pw_ref, ps_ref, df_ref, db_ref, pool_ref, ret_ref,
                  pad_ref, pooled_ref, kb_ref, acc_ref, dec_ref, sf_ref, sb_ref):
    seq = q_ref.shape[0]
    nh, dh = pw_ref.shape[0], pw_ref.shape[-1]
    c = _RET_CHUNK
    n = seq // c
    head_cols = [slice(hd * dh, (hd + 1) * dh) for hd in range(nh)]

    def tn_dot(a, b):
        return lax.dot_general(a, b, (((0,), (0,)), ((), ())), preferred_element_type=_F32)

    def nt_dot(a, b):
        return lax.dot_general(a, b, (((1,), (1,)), ((), ())), preferred_element_type=_F32)

    ri = lax.broadcasted_iota(jnp.int32, (c, c), 0).astype(_F32)
    cj = lax.broadcasted_iota(jnp.int32, (c, c), 1).astype(_F32)
    diff = ri - cj
    qk_scale = dh ** -0.5
    pooled_chunk, chunk_f, chunk_b = [], [], []
    for hd, cols in enumerate(head_cols):
        pooled_chunk.append(_pool_chunks(u_ref, pad_ref, cols, pl.program_id(1) * nh + hd, seq))
        head = pl.program_id(1) * nh + hd
        lg_f = -jnp.exp(jnp.full((1, c), df_ref[head], _F32))
        lg_b = -jnp.exp(jnp.full((1, c), db_ref[head], _F32))
        dec_ref[hd, 0] = jnp.exp(jnp.where(diff >= 0, diff * lg_f, -diff * lg_b)) * qk_scale
        dec_ref[hd, 1] = jnp.exp((ri + 1.0) * lg_f) * qk_scale
        dec_ref[hd, 2] = jnp.exp((c - 1.0 - ri) * lg_f)
        dec_ref[hd, 3] = jnp.exp((c - ri) * lg_b) * qk_scale
        dec_ref[hd, 4] = jnp.exp(ri * lg_b)
        chunk_f.append(jnp.exp(c * lg_f))
        chunk_b.append(jnp.exp(c * lg_b))

    for ci in range(n):
        rows = slice(ci * c, (ci + 1) * c)
        for hd, cols in enumerate(head_cols):
            q = q_ref[rows, cols]
            k = k_ref[rows, cols]
            k32 = k.astype(_F32)
            if ci > 0:
                kb_ref[rows, cols] = (k32 * dec_ref[hd, 4]).astype(_BF16)
            v = v_ref[rows, cols]
            scores = nt_dot(q, k) * dec_ref[hd, 0]
            out = _dot(scores.astype(_BF16), v)
            if ci > 0:
                out = out + _dot(q, sf_ref[hd].astype(_BF16)) * dec_ref[hd, 1]
            acc_ref[rows, cols] = out
            if ci < n - 1:
                upd = tn_dot((k32 * dec_ref[hd, 2]).astype(_BF16), v)
                sf_ref[hd] = upd if ci == 0 else sf_ref[hd] * chunk_f[hd] + upd
            pooled_ref[rows, cols] = pooled_chunk[hd](ci)

    for hd, cols in enumerate(head_cols):
        pool_ref[:, cols] = (_dot(pooled_ref[:, cols], pw_ref[hd]) * ps_ref[hd]).astype(pool_ref.dtype)

    for ci in reversed(range(n)):
        rows = slice(ci * c, (ci + 1) * c)
        for hd, cols in enumerate(head_cols):
            out = acc_ref[rows, cols]
            if ci < n - 1:
                out = out + _dot(q_ref[rows, cols], sb_ref[hd].astype(_BF16)) * dec_ref[hd, 3]
            ret = _rms_rows(out)
            ret_ref[rows, cols] = (gate_ref[rows, cols].astype(_F32) * ret).astype(ret_ref.dtype)
            if ci > 0:
                upd = tn_dot(kb_ref[rows, cols], v_ref[rows, cols])
                sb_ref[hd] = upd if ci == n - 1 else sb_ref[hd] * chunk_b[hd] + upd


def _mixer(proj, pool_w, pool_scale, dec_f, dec_b, bsz, seq):
    t = proj.shape[0]
    heads = _RET_HEADS
    dh = pool_w.shape[-1]
    pool_blocks = len(_POOL_WINDOWS)
    nh = _MIXER_HEADS_PER_STEP
    assert heads % nh == 0 and pool_blocks == heads
    wide = nh * dh
    seq_block = lambda first: pl.BlockSpec((seq, wide), lambda b, j: (b, first // nh + j))
    per_head = lambda shape: pl.BlockSpec((nh,) + shape, lambda b, j: (j, 0, 0))
    scratch = [
        pltpu.VMEM((seq + 2 * _POOL_HALO, wide), _BF16),
        pltpu.VMEM((seq, wide), _BF16),
        pltpu.VMEM((seq, wide), _BF16),
        pltpu.VMEM((seq, wide), _F32),
        pltpu.VMEM((nh, 5, _RET_CHUNK, _RET_CHUNK), _F32),
        pltpu.VMEM((nh, dh, dh), _F32),
        pltpu.VMEM((nh, dh, dh), _F32),
    ]
    vmem = (2 * 7 * _nbytes((seq, wide), _BF16) + 4 * _nbytes((seq, wide), _F32)
            + 8 * nh * _nbytes((dh, dh), _F32))
    return pl.pallas_call(
        _mixer_kernel,
        grid=(bsz, heads // nh),
        in_specs=[
            seq_block(0),
            seq_block(pool_blocks),
            seq_block(pool_blocks + heads),
            seq_block(pool_blocks + 2 * heads),
            seq_block(pool_blocks + 3 * heads),
            per_head((dh, dh)),
            per_head((1, dh)),
            pl.BlockSpec(memory_space=pltpu.SMEM),
            pl.BlockSpec(memory_space=pltpu.SMEM),
        ],
        out_specs=[pl.BlockSpec((seq, wide), lambda b, j: (b, j)),
                   pl.BlockSpec((seq, wide), lambda b, j: (b, j))],
        out_shape=[jax.ShapeDtypeStruct((t, pool_blocks * dh), _BF16),
                   jax.ShapeDtypeStruct((t, heads * dh), _BF16)],
        scratch_shapes=scratch,
        compiler_params=pltpu.CompilerParams(
            dimension_semantics=("arbitrary", "arbitrary"), vmem_limit_bytes=_vmem_limit(vmem)),
        name="mixer",
    )(proj, proj, proj, proj, proj, pool_w, pool_scale, dec_f, dec_b)


def _outproj_kernel(p_ref, r_ref, x_ref, mod_ref, gain_ref, w_ref, x1_ref, h2_ref):
    kp = p_ref.shape[1]
    gate = mod_ref[0, 2:3, :]
    scale = gain_ref[...] * (1.0 + mod_ref[0, 4:5, :])
    shift = mod_ref[0, 3:4, :]
    for r0 in range(0, x_ref.shape[0], _DOT_ROWS):
        rows = slice(r0, r0 + _DOT_ROWS)
        mixed = _dot(p_ref[rows, :], w_ref[0:kp, :]) + _dot(r_ref[rows, :], w_ref[kp:, :])
        x1_ref[rows, :] = x_ref[rows, :] + gate * mixed
        _rmsnorm_rows(x1_ref, h2_ref, r0, _DOT_ROWS, scale, shift)


def _outproj(pool_out, ret_out, x2d, mod3, gain, w, seq, *, tm=512):
    t, d = x2d.shape
    kp, kr = pool_out.shape[1], ret_out.shape[1]
    per_batch = seq // tm
    vmem = (2 * _nbytes((tm, kp + kr), _BF16) + 4 * _nbytes((tm, d), _F32) + 2 * _nbytes((tm, d), _BF16)
            + _nbytes((kp + kr, d), _BF16) + _nbytes((_DOT_ROWS, d), _F32))
    return pl.pallas_call(
        _outproj_kernel,
        grid=(t // tm,),
        in_specs=[
            pl.BlockSpec((tm, kp), lambda i: (i, 0)),
            pl.BlockSpec((tm, kr), lambda i: (i, 0)),
            pl.BlockSpec((tm, d), lambda i: (i, 0)),
            pl.BlockSpec((1, _N_MOD, d), lambda i: (i // per_batch, 0, 0)),
            pl.BlockSpec((1, d), lambda i: (0, 0)),
            pl.BlockSpec((kp + kr, d), lambda i: (0, 0), pipeline_mode=pl.Buffered(1)),
        ],
        out_specs=[pl.BlockSpec((tm, d), lambda i: (i, 0)),
                   pl.BlockSpec((tm, d), lambda i: (i, 0))],
        out_shape=[jax.ShapeDtypeStruct((t, d), _F32),
                   jax.ShapeDtypeStruct((t, d), _BF16)],
        compiler_params=pltpu.CompilerParams(
            dimension_semantics=("arbitrary",), vmem_limit_bytes=_vmem_limit(vmem)),
        name="outproj",
    )(pool_out, ret_out, x2d, mod3, gain, w)


def _ffn_kernel(h2_ref, wgu_ref, wd_ref, x1_hbm, mod_ref, gain_ref, o_ref, x1_buf, sem, *, final_norm):
    i, f = pl.program_id(0), pl.program_id(1)
    tm, d = o_ref.shape

    def x1_copy(tile):
        return pltpu.make_async_copy(x1_hbm.at[pl.ds(pl.multiple_of(tile * tm, tm), tm), :], x1_buf, sem.at[0])

    last = pl.num_programs(1) - 1
    gate = mod_ref[0, 5:6, :]

    def hidden():
        gu = _dot(h2_ref[...], wgu_ref[...])
        parts = [_silu(gu[:, c0:c0 + _LANES]) * gu[:, c0 + _LANES:c0 + 2 * _LANES]
                 for c0 in range(0, gu.shape[1], 2 * _LANES)]
        return jnp.concatenate(parts, axis=1).astype(_BF16)

    @pl.when(f == 0)
    def _():
        @pl.when(i == 0)
        def _():
            x1_copy(0).start()

        x1_copy(i).wait()
        o_ref[...] = x1_buf[...] + gate * _dot(hidden(), wd_ref[...])

    @pl.when((f == 1) & (i + 1 < pl.num_programs(0)))
    def _():
        x1_copy(i + 1).start()

    @pl.when((f > 0) & (f < last))
    def _():
        o_ref[...] += gate * _dot(hidden(), wd_ref[...])

    @pl.when(f == last)
    def _():
        hid = hidden()
        for r0 in range(0, tm, _DOT_ROWS):
            rows = slice(r0, r0 + _DOT_ROWS)
            o_ref[rows, :] += gate * _dot(hid[rows, :], wd_ref[...])
            if final_norm:
                _rmsnorm_rows(o_ref, o_ref, r0, _DOT_ROWS, gain_ref[...])


def _ffn(h2, x1, mod3, wgu, wd, final_gain, seq, *, final_norm, tm=1024, tf=512):
    t, d = x1.shape
    dff = wd.shape[0]
    assert wgu.shape == (d, 2 * dff) and tf % _LANES == 0
    assert dff // tf >= 2
    per_batch = seq // tm
    vmem = (2 * _nbytes((tm, d), _BF16) + 6 * _nbytes((d, tf), _BF16) + 3 * _nbytes((tm, d), _F32)
            + 3 * _nbytes((tm, tf), _F32))
    return pl.pallas_call(
        functools.partial(_ffn_kernel, final_norm=final_norm),
        grid=(t // tm, dff // tf),
        in_specs=[
            pl.BlockSpec((tm, d), lambda i, f: (i, 0)),
            pl.BlockSpec((d, 2 * tf), lambda i, f: (0, f)),
            pl.BlockSpec((tf, d), lambda i, f: (f, 0)),
            pl.BlockSpec(memory_space=pl.ANY),
            pl.BlockSpec((1, _N_MOD, d), lambda i, f: (i // per_batch, 0, 0)),
            pl.BlockSpec((1, d), lambda i, f: (0, 0)),
        ],
        out_specs=pl.BlockSpec((tm, d), lambda i, f: (i, 0)),
        out_shape=jax.ShapeDtypeStruct((t, d), _F32),
        scratch_shapes=[pltpu.VMEM((tm, d), _F32), pltpu.SemaphoreType.DMA((1,))],
        compiler_params=pltpu.CompilerParams(
            dimension_semantics=("arbitrary", "arbitrary"), vmem_limit_bytes=_vmem_limit(vmem)),
        name="ffn",
    )(h2, wgu, wd, x1, mod3, final_gain)


def kernel(x, c, w_ada, b_ada, norm1_g, w_in, pool_w, pool_scale, ret_decay_fwd, ret_decay_bwd,
           w_out, norm2_g, w_gate, w_up, w_down, final_g):
    bsz, seq, d = x.shape
    depth = w_ada.shape[0]
    heads = _RET_HEADS
    pool_width = pool_scale.shape[1]
    dh = (w_in.shape[2] - pool_width) // (4 * heads)
    assert pool_width == len(_POOL_WINDOWS) * dh and seq % _RET_CHUNK == 0 and dh == _RET_CHUNK
    assert all(w & (w - 1) == 0 and w // 2 <= _POOL_EDGE_ROWS for w in _POOL_WINDOWS)

    inv = 1.0 / (_ROPE_BASE ** np.linspace(0.0, 1.0, dh // 2))
    ang = np.arange(seq)[:, None] * inv[None, :]
    cos, sin = jnp.asarray(np.cos(ang), _F32), jnp.asarray(np.sin(ang), _F32)

    xf = x.reshape(bsz * seq, d)
    for l in range(depth):
        mod3 = _adaln(c, w_ada[l], b_ada[l]).reshape(bsz, _N_MOD, d)
        proj, wo, wd, wgu = _inproj(xf, mod3, norm1_g[l].reshape(1, d), cos, sin, w_in[l].astype(_BF16), seq,
                                    pool_width, [w_out[l], w_down[l]], [w_gate[l], w_up[l]])
        pool_out, ret_out = _mixer(
            proj, pool_w[l].astype(_BF16), pool_scale[l].reshape(len(_POOL_WINDOWS), 1, dh),
            ret_decay_fwd[l].astype(_F32), ret_decay_bwd[l].astype(_F32), bsz, seq)
        x1, h2 = _outproj(pool_out, ret_out, xf, mod3, norm2_g[l].reshape(1, d), wo, seq)
        xf = _ffn(h2, x1, mod3, wgu, wd, final_g.reshape(1, d), seq, final_norm=(l == depth - 1))
    return xf.reshape(bsz, seq, d)
```

```python
import functools

import jax
import jax.numpy as jnp
import numpy as np
from jax import lax
from jax.experimental import pallas as pl
from jax.experimental.pallas import tpu as pltpu

_EPS = 1e-6
_POOL_WINDOWS = (2, 4, 8, 16)
_RET_HEADS = 4
_ROPE_BASE = 10000.0
_N_MOD = 6

_V7X_VMEM_BYTES = 64 * 1024 * 1024
_VMEM_UNCOUNTED_BYTES = 8 * 1024 * 1024
_VMEM_RESERVED_BYTES = 2 * 1024 * 1024
_LANES = 128
_BF16_ROWS = 16
_RET_CHUNK = 256
_POOL_HALO = 128
_POOL_EDGE_ROWS = 8
_NORM_ROWS = 128
_DOT_ROWS = 256
_MIXER_HEADS_PER_STEP = 2

_F32 = jnp.float32
_BF16 = jnp.bfloat16


def _vmem_limit(nbytes):
    return int(min(_V7X_VMEM_BYTES - _VMEM_RESERVED_BYTES, nbytes + _VMEM_UNCOUNTED_BYTES))


def _nbytes(shape, dtype):
    n = 1
    for s in shape:
        n *= s
    return n * jnp.dtype(dtype).itemsize


def _dot(a, b):
    return jnp.dot(a, b, preferred_element_type=_F32)


def _silu(x):
    return x * jax.nn.sigmoid(x)


def _rms_rows(x):
    return x * lax.rsqrt(jnp.mean(x * x, axis=-1, keepdims=True) + _EPS)


def _col_blocks(d):
    return [slice(c0, c0 + _LANES) for c0 in range(0, d, _LANES)]


def _rmsnorm_rows(src_ref, dst_ref, row0, nrows, gain, shift=None):
    d = src_ref.shape[1]
    for r0 in range(row0, row0 + nrows, _NORM_ROWS):
        rows = slice(r0, r0 + _NORM_ROWS)
        ss = None
        for cols in _col_blocks(d):
            xc = src_ref[rows, cols]
            ss = xc * xc if ss is None else ss + xc * xc
        rs = lax.rsqrt(jnp.sum(ss, axis=-1, keepdims=True) / d + _EPS)
        for cols in _col_blocks(d):
            y = src_ref[rows, cols] * rs * gain[:, cols]
            if shift is not None:
                y = y + shift[:, cols]
            dst_ref[rows, cols] = y.astype(dst_ref.dtype)


def _adaln_kernel(c_ref, w_ref, b_ref, o_ref):
    c = c_ref[...]
    o_ref[...] = _dot(_silu(c).astype(_BF16), w_ref[...].astype(_BF16)) + b_ref[...]


def _adaln(c, w, b, *, tn=2048):
    bsz, d = c.shape
    n = w.shape[1]
    vmem = 2 * _nbytes((d, tn), _F32) + _nbytes((d, tn), _BF16) + 4 * _nbytes((bsz, d), _F32)
    return pl.pallas_call(
        _adaln_kernel,
        grid=(n // tn,),
        in_specs=[
            pl.BlockSpec((bsz, d), lambda j: (0, 0)),
            pl.BlockSpec((d, tn), lambda j: (0, j)),
            pl.BlockSpec((1, tn), lambda j: (0, j)),
        ],
        out_specs=pl.BlockSpec((bsz, tn), lambda j: (0, j)),
        out_shape=jax.ShapeDtypeStruct((bsz, n), _F32),
        compiler_params=pltpu.CompilerParams(
            dimension_semantics=("arbitrary",), vmem_limit_bytes=_vmem_limit(vmem)),
        name="adaln",
    )(c, w, b.reshape(1, n))


def _slab_specs(weights, steps, index_map):
    specs, shapes, nbytes = [], [], 0
    for w in weights:
        rows, cols = w.shape
        assert rows % steps == 0 and (rows // steps) % _BF16_ROWS == 0
        specs.append(pl.BlockSpec((rows // steps, cols), index_map))
        shapes.append(jax.ShapeDtypeStruct(w.shape, _BF16))
        nbytes += 2 * _nbytes((rows // steps, cols), _F32) + 2 * _nbytes((rows // steps, cols), _BF16)
    return specs, shapes, nbytes


def _cast_slabs(src_refs, dst_refs):
    for src, dst in zip(src_refs, dst_refs):
        dst[...] = src[...].astype(dst.dtype)


def _cast_gate_up(wg_ref, wu_ref, wgu_ref):
    for g in range(wg_ref.shape[1] // _LANES):
        cols = slice(g * _LANES, (g + 1) * _LANES)
        wgu_ref[:, 2 * g * _LANES:(2 * g + 1) * _LANES] = wg_ref[:, cols].astype(wgu_ref.dtype)
        wgu_ref[:, (2 * g + 1) * _LANES:(2 * g + 2) * _LANES] = wu_ref[:, cols].astype(wgu_ref.dtype)


def _inproj_kernel(x_ref, mod_ref, gain_ref, cos_ref, sin_ref, w_ref, *rest, n_cast, pool_width, heads):
    cast_in, (wg_ref, wu_ref) = rest[:n_cast], rest[n_cast:n_cast + 2]
    u_ref, q_ref, k_ref, v_ref, g_ref, *cast_out, wgu_ref, h_ref = rest[n_cast + 2:]
    groups, _, wide = q_ref.shape
    width = groups * wide
    dh = width // heads
    half = dh // 2
    scale = gain_ref[...] * (1.0 + mod_ref[0, 1:2, :])
    shift = mod_ref[0, 0:1, :]

    def store(dst_ref, rows, val):
        for j in range(dst_ref.shape[0]):
            dst_ref[j, rows, :] = val[:, j * wide:(j + 1) * wide].astype(dst_ref.dtype)

    for r0 in range(0, x_ref.shape[0], _DOT_ROWS):
        rows = slice(r0, r0 + _DOT_ROWS)
        _rmsnorm_rows(x_ref, h_ref, r0, _DOT_ROWS, scale, shift)
        h = h_ref[rows, :]
        c0 = pool_width + 3 * width
        store(g_ref, rows, _silu(_dot(h, w_ref[:, c0:c0 + width])))
        cos, sin = cos_ref[rows, :], sin_ref[rows, :]
        for part, dst_ref in enumerate((q_ref, k_ref)):
            c0 = pool_width + part * width
            t = _dot(h, w_ref[:, c0:c0 + width])
            for hd in range(heads):
                t1 = t[:, hd * dh:hd * dh + half]
                t2 = t[:, hd * dh + half:(hd + 1) * dh]
                j, c1 = (hd * dh) // wide, (hd * dh) % wide
                dst_ref[j, rows, c1:c1 + half] = (t1 * cos - t2 * sin).astype(dst_ref.dtype)
                dst_ref[j, rows, c1 + half:c1 + dh] = (t1 * sin + t2 * cos).astype(dst_ref.dtype)
        if r0 == 0:
            _cast_slabs(cast_in, cast_out)
            _cast_gate_up(wg_ref, wu_ref, wgu_ref)
        c0 = pool_width + 2 * width
        store(v_ref, rows, _dot(h, w_ref[:, c0:c0 + width]))
        store(u_ref, rows, _dot(h, w_ref[:, 0:pool_width]))


def _inproj(x2d, mod3, gain, cos, sin, w, seq, pool_width, cast_weights, gate_up, *, tm=512):
    t, d = x2d.shape
    n = w.shape[1]
    per_batch = seq // tm
    steps = t // tm
    wide = _MIXER_HEADS_PER_STEP * (n - pool_width) // (4 * _RET_HEADS)
    groups = _RET_HEADS // _MIXER_HEADS_PER_STEP
    assert pool_width == groups * wide
    part_spec = pl.BlockSpec((groups, tm, wide), lambda i: (0, i, 0))
    part_shape = jax.ShapeDtypeStruct((groups, t, wide), _BF16)
    cast_specs, cast_shapes, cast_bytes = _slab_specs(cast_weights, steps, lambda i: (i, 0))
    gu_specs, _, gu_bytes = _slab_specs(gate_up, steps, lambda i: (i, 0))
    gu_rows, dff = gate_up[0].shape
    assert gate_up[1].shape == (gu_rows, dff) and dff % _LANES == 0
    vmem = (2 * _nbytes((tm, d), _F32) + _nbytes((d, n), _BF16) + 2 * _nbytes((tm, n), _BF16)
            + _nbytes((tm, d), _BF16) + _nbytes((_DOT_ROWS, n), _F32) + 4 * _nbytes(cos.shape, _F32) // per_batch
            + cast_bytes + gu_bytes)
    return pl.pallas_call(
        functools.partial(_inproj_kernel, n_cast=len(cast_weights), pool_width=pool_width, heads=_RET_HEADS),
        grid=(t // tm,),
        in_specs=[
            pl.BlockSpec((tm, d), lambda i: (i, 0)),
            pl.BlockSpec((1, _N_MOD, d), lambda i: (i // per_batch, 0, 0)),
            pl.BlockSpec((1, d), lambda i: (0, 0)),
            pl.BlockSpec((tm, cos.shape[1]), lambda i: (i % per_batch, 0)),
            pl.BlockSpec((tm, sin.shape[1]), lambda i: (i % per_batch, 0)),
            pl.BlockSpec((d, n), lambda i: (0, 0), pipeline_mode=pl.Buffered(1)),
        ] + cast_specs + gu_specs,
        out_specs=([part_spec] * 5 + cast_specs
                   + [pl.BlockSpec((gu_rows // steps, 2 * dff), lambda i: (i, 0))]),
        out_shape=([part_shape] * 5 + cast_shapes
                   + [jax.ShapeDtypeStruct((gu_rows, 2 * dff), _BF16)]),
        scratch_shapes=[pltpu.VMEM((tm, d), _BF16)],
        compiler_params=pltpu.CompilerParams(
            dimension_semantics=("arbitrary",), vmem_limit_bytes=_vmem_limit(vmem)),
        name="inproj",
    )(x2d, mod3, gain, cos, sin, w, *cast_weights, *gate_up)


def _pool_chunks(u_ref, pad_ref, cols, group, seq):
    c = _RET_CHUNK
    n = seq // c
    halo = _POOL_HALO
    half = jnp.int32(0)
    inv_w = jnp.float32(0)
    for gi, window in enumerate(_POOL_WINDOWS):
        half = jnp.where(group == gi, jnp.int32(window // 2), half)
        inv_w = jnp.where(group == gi, jnp.float32(1.0 / window), inv_w)

    width_c = cols.stop - cols.start
    zeros = jnp.zeros((halo, width_c), pad_ref.dtype)
    pad_ref[0:halo, cols] = zeros
    pad_ref[halo + seq:, cols] = zeros
    pad_ref[halo:halo + seq, cols] = u_ref[:, cols]

    rel = (lax.broadcasted_iota(jnp.int32, (c, c + 2 * halo), 1) - halo
           - lax.broadcasted_iota(jnp.int32, (c, c + 2 * halo), 0))
    band = (jnp.where((rel >= -half) & (rel < half), inv_w, 0.0)
            - jnp.where(rel == 0, 1.0, 0.0)).astype(_BF16)

    edge = _POOL_EDGE_ROWS
    t_lo = lax.broadcasted_iota(jnp.int32, (edge, width_c), 0)
    t_hi = t_lo + (seq - edge)
    width = (2 * half).astype(_F32)
    ratio_lo = width / (t_lo + half - jnp.maximum(t_lo - half, 0)).astype(_F32)
    ratio_hi = width / (jnp.minimum(t_hi + half, seq) - (t_hi - half)).astype(_F32)

    def pooled_chunk(ci):
        pooled = _dot(band, pad_ref[ci * c:ci * c + c + 2 * halo, cols])
        if ci == 0:
            u_lo = u_ref[0:edge, cols].astype(_F32)
            pooled = jnp.concatenate([(pooled[0:edge] + u_lo) * ratio_lo - u_lo, pooled[edge:]], axis=0)
        if ci == n - 1:
            u_hi = u_ref[seq - edge:seq, cols].astype(_F32)
            pooled = jnp.concatenate([pooled[:c - edge], (pooled[c - edge:] + u_hi) * ratio_hi - u_hi], axis=0)
        return pooled.astype(_BF16)

    return pooled_chunk


def _mixer_kernel(u_ref, q_ref, k_ref, v_ref, gate_ref, pw_ref, ps_ref, df_ref, db_ref, pool_ref, ret_ref,
                  pad_ref, pooled_ref, kb_ref, acc_ref, dec_ref, sf_ref, sb_ref):
    u_ref, q_ref, k_ref, v_ref, gate_ref, pool_ref, ret_ref = (
        r.at[0] for r in (u_ref, q_ref, k_ref, v_ref, gate_ref, pool_ref, ret_ref))
    seq = q_ref.shape[0]
    nh, dh = pw_ref.shape[0], pw_ref.shape[-1]
    c = _RET_CHUNK
    n = seq // c
    head_cols = [slice(hd * dh, (hd + 1) * dh) for hd in range(nh)]

    def tn_dot(a, b):
        return lax.dot_general(a, b, (((0,), (0,)), ((), ())), preferred_element_type=_F32)

    def nt_dot(a, b):
        return lax.dot_general(a, b, (((1,), (1,)), ((), ())), preferred_element_type=_F32)

    ri = lax.broadcasted_iota(jnp.int32, (c, c), 0).astype(_F32)
    cj = lax.broadcasted_iota(jnp.int32, (c, c), 1).astype(_F32)
    diff = ri - cj
    qk_scale = dh ** -0.5
    pooled_chunk, chunk_f, chunk_b = [], [], []
    for hd, cols in enumerate(head_cols):
        pooled_chunk.append(_pool_chunks(u_ref, pad_ref, cols, pl.program_id(1) * nh + hd, seq))
        head = pl.program_id(1) * nh + hd
        lg_f = -jnp.exp(jnp.full((1, c), df_ref[head], _F32))
        lg_b = -jnp.exp(jnp.full((1, c), db_ref[head], _F32))
        dec_ref[hd, 0] = jnp.exp(jnp.where(diff >= 0, diff * lg_f, -diff * lg_b)) * qk_scale
        dec_ref[hd, 1] = jnp.exp((ri + 1.0) * lg_f) * qk_scale
        dec_ref[hd, 2] = jnp.exp((c - 1.0 - ri) * lg_f)
        dec_ref[hd, 3] = jnp.exp((c - ri) * lg_b) * qk_scale
        dec_ref[hd, 4] = jnp.exp(ri * lg_b)
        chunk_f.append(jnp.exp(c * lg_f))
        chunk_b.append(jnp.exp(c * lg_b))

    for ci in range(n):
        rows = slice(ci * c, (ci + 1) * c)
        for hd, cols in enumerate(head_cols):
            q = q_ref[rows, cols]
            k = k_ref[rows, cols]
            k32 = k.astype(_F32)
            if ci > 0:
                kb_ref[rows, cols] = (k32 * dec_ref[hd, 4]).astype(_BF16)
            v = v_ref[rows, cols]
            scores = nt_dot(q, k) * dec_ref[hd, 0]
            out = _dot(scores.astype(_BF16), v)
            if ci > 0:
                out = out + _dot(q, sf_ref[hd].astype(_BF16)) * dec_ref[hd, 1]
            acc_ref[rows, cols] = out
            if ci < n - 1:
                upd = tn_dot((k32 * dec_ref[hd, 2]).astype(_BF16), v)
                sf_ref[hd] = upd if ci == 0 else sf_ref[hd] * chunk_f[hd] + upd
            pooled_ref[rows, cols] = pooled_chunk[hd](ci)

    for hd, cols in enumerate(head_cols):
        pool_ref[:, cols] = (_dot(pooled_ref[:, cols], pw_ref[hd]) * ps_ref[hd]).astype(pool_ref.dtype)

    for ci in reversed(range(n)):
        rows = slice(ci * c, (ci + 1) * c)
        for hd, cols in enumerate(head_cols):
            out = acc_ref[rows, cols]
            if ci < n - 1:
                out = out + _dot(q_ref[rows, cols], sb_ref[hd].astype(_BF16)) * dec_ref[hd, 3]
            ret = _rms_rows(out)
            ret_ref[rows, cols] = (gate_ref[rows, cols].astype(_F32) * ret).astype(ret_ref.dtype)
            if ci > 0:
                upd = tn_dot(kb_ref[rows, cols], v_ref[rows, cols])
                sb_ref[hd] = upd if ci == n - 1 else sb_ref[hd] * chunk_b[hd] + upd


def _mixer(parts, pool_w, pool_scale, dec_f, dec_b, bsz, seq):
    groups, t, wide = parts[0].shape
    heads = _RET_HEADS
    dh = pool_w.shape[-1]
    nh = _MIXER_HEADS_PER_STEP
    assert heads == groups * nh and len(_POOL_WINDOWS) == heads and wide == nh * dh
    seq_block = pl.BlockSpec((1, seq, wide), lambda b, j: (j, b, 0))
    per_head = lambda shape: pl.BlockSpec((nh,) + shape, lambda b, j: (j, 0, 0))
    scratch = [
        pltpu.VMEM((seq + 2 * _POOL_HALO, wide), _BF16),
        pltpu.VMEM((seq, wide), _BF16),
        pltpu.VMEM((seq, wide), _BF16),
        pltpu.VMEM((seq, wide), _F32),
        pltpu.VMEM((nh, 5, _RET_CHUNK, _RET_CHUNK), _F32),
        pltpu.VMEM((nh, dh, dh), _F32),
        pltpu.VMEM((nh, dh, dh), _F32),
    ]
    vmem = (2 * 7 * _nbytes((seq, wide), _BF16) + 4 * _nbytes((seq, wide), _F32)
            + 8 * nh * _nbytes((dh, dh), _F32))
    return pl.pallas_call(
        _mixer_kernel,
        grid=(bsz, heads // nh),
        in_specs=[
            seq_block, seq_block, seq_block, seq_block, seq_block,
            per_head((dh, dh)),
            per_head((1, dh)),
            pl.BlockSpec(memory_space=pltpu.SMEM),
            pl.BlockSpec(memory_space=pltpu.SMEM),
        ],
        out_specs=[seq_block, seq_block],
        out_shape=[jax.ShapeDtypeStruct((groups, t, wide), _BF16),
                   jax.ShapeDtypeStruct((groups, t, wide), _BF16)],
        scratch_shapes=scratch,
        compiler_params=pltpu.CompilerParams(
            dimension_semantics=("arbitrary", "arbitrary"), vmem_limit_bytes=_vmem_limit(vmem)),
        name="mixer",
    )(*parts, pool_w, pool_scale, dec_f, dec_b)


def _outproj_kernel(p_ref, r_ref, x_ref, mod_ref, gain_ref, w_ref, x1_ref, h2_ref):
    groups, _, wide = p_ref.shape
    gate = mod_ref[0, 2:3, :]
    scale = gain_ref[...] * (1.0 + mod_ref[0, 4:5, :])
    shift = mod_ref[0, 3:4, :]
    for r0 in range(0, x_ref.shape[0], _DOT_ROWS):
        rows = slice(r0, r0 + _DOT_ROWS)
        mixed = None
        for part, src_ref in enumerate((p_ref, r_ref)):
            for j in range(groups):
                k0 = (part * groups + j) * wide
                term = _dot(src_ref[j, rows, :], w_ref[k0:k0 + wide, :])
                mixed = term if mixed is None else mixed + term
        x1_ref[rows, :] = x_ref[rows, :] + gate * mixed
        _rmsnorm_rows(x1_ref, h2_ref, r0, _DOT_ROWS, scale, shift)


def _outproj(pool_out, ret_out, x2d, mod3, gain, w, seq, *, tm=512):
    t, d = x2d.shape
    groups, _, wide = pool_out.shape
    kp = kr = groups * wide
    assert ret_out.shape == pool_out.shape and w.shape[0] == kp + kr
    per_batch = seq // tm
    vmem = (2 * _nbytes((tm, kp + kr), _BF16) + 4 * _nbytes((tm, d), _F32) + 2 * _nbytes((tm, d), _BF16)
            + _nbytes((kp + kr, d), _BF16) + _nbytes((_DOT_ROWS, d), _F32))
    return pl.pallas_call(
        _outproj_kernel,
        grid=(t // tm,),
        in_specs=[
            pl.BlockSpec((groups, tm, wide), lambda i: (0, i, 0)),
            pl.BlockSpec((groups, tm, wide), lambda i: (0, i, 0)),
            pl.BlockSpec((tm, d), lambda i: (i, 0)),
            pl.BlockSpec((1, _N_MOD, d), lambda i: (i // per_batch, 0, 0)),
            pl.BlockSpec((1, d), lambda i: (0, 0)),
            pl.BlockSpec((kp + kr, d), lambda i: (0, 0), pipeline_mode=pl.Buffered(1)),
        ],
        out_specs=[pl.BlockSpec((tm, d), lambda i: (i, 0)),
                   pl.BlockSpec((tm, d), lambda i: (i, 0))],
        out_shape=[jax.ShapeDtypeStruct((t, d), _F32),
                   jax.ShapeDtypeStruct((t, d), _BF16)],
        compiler_params=pltpu.CompilerParams(
            dimension_semantics=("arbitrary",), vmem_limit_bytes=_vmem_limit(vmem)),
        name="outproj",
    )(pool_out, ret_out, x2d, mod3, gain, w)


def _ffn_kernel(h2_ref, wgu_ref, wd_ref, x1_hbm, mod_ref, gain_ref, o_ref, x1_buf, sem, *, final_norm):
    i, f = pl.program_id(0), pl.program_id(1)
    tm, d = o_ref.shape

    def x1_copy():
        return pltpu.make_async_copy(x1_hbm.at[pl.ds(pl.multiple_of(i * tm, tm), tm), :], x1_buf, sem.at[0])

    last = pl.num_programs(1) - 1

    def hidden():
        gu = _dot(h2_ref[...], wgu_ref[...])
        parts = [_silu(gu[:, c0:c0 + _LANES]) * gu[:, c0 + _LANES:c0 + 2 * _LANES]
                 for c0 in range(0, gu.shape[1], 2 * _LANES)]
        return jnp.concatenate(parts, axis=1).astype(_BF16)

    def finish_rows(r0):
        rows = slice(r0, r0 + _NORM_ROWS)
        gate = mod_ref[0, 5:6, :]
        gain = gain_ref[...]
        ss = None
        for cols in _col_blocks(d):
            x2 = x1_buf[rows, cols] + gate[:, cols] * o_ref[rows, cols]
            o_ref[rows, cols] = x2
            ss = x2 * x2 if ss is None else ss + x2 * x2
        if final_norm:
            rs = lax.rsqrt(jnp.sum(ss, axis=-1, keepdims=True) / d + _EPS)
            for cols in _col_blocks(d):
                o_ref[rows, cols] = o_ref[rows, cols] * rs * gain[:, cols]

    @pl.when(f == 0)
    def _():
        x1_copy().start()
        o_ref[...] = _dot(hidden(), wd_ref[...])

    @pl.when((f > 0) & (f < last))
    def _():
        o_ref[...] += _dot(hidden(), wd_ref[...])

    @pl.when(f == last)
    def _():
        x1_copy().wait()
        hid = hidden()
        for r0 in range(0, tm, _DOT_ROWS):
            rows = slice(r0, r0 + _DOT_ROWS)
            o_ref[rows, :] += _dot(hid[rows, :], wd_ref[...])
            for s0 in range(r0, r0 + _DOT_ROWS, _NORM_ROWS):
                finish_rows(s0)


def _ffn(h2, x1, mod3, wgu, wd, final_gain, seq, *, final_norm, tm=1024, tf=512):
    t, d = x1.shape
    dff = wd.shape[0]
    assert wgu.shape == (d, 2 * dff) and tf % _LANES == 0
    assert dff // tf >= 2
    per_batch = seq // tm
    vmem = (2 * _nbytes((tm, d), _BF16) + 6 * _nbytes((d, tf), _BF16) + 3 * _nbytes((tm, d), _F32)
            + 3 * _nbytes((tm, tf), _F32))
    return pl.pallas_call(
        functools.partial(_ffn_kernel, final_norm=final_norm),
        grid=(t // tm, dff // tf),
        in_specs=[
            pl.BlockSpec((tm, d), lambda i, f: (i, 0)),
            pl.BlockSpec((d, 2 * tf), lambda i, f: (0, f)),
            pl.BlockSpec((tf, d), lambda i, f: (f, 0)),
            pl.BlockSpec(memory_space=pl.ANY),
            pl.BlockSpec((1, _N_MOD, d), lambda i, f: (i // per_batch, 0, 0)),
            pl.BlockSpec((1, d), lambda i, f: (0, 0)),
        ],
        out_specs=pl.BlockSpec((tm, d), lambda i, f: (i, 0)),
        out_shape=jax.ShapeDtypeStruct((t, d), _F32),
        scratch_shapes=[pltpu.VMEM((tm, d), _F32), pltpu.SemaphoreType.DMA((1,))],
        compiler_params=pltpu.CompilerParams(
            dimension_semantics=("arbitrary", "arbitrary"), vmem_limit_bytes=_vmem_limit(vmem)),
        name="ffn",
    )(h2, wgu, wd, x1, mod3, final_gain)


def kernel(x, c, w_ada, b_ada, norm1_g, w_in, pool_w, pool_scale, ret_decay_fwd, ret_decay_bwd,
           w_out, norm2_g, w_gate, w_up, w_down, final_g):
    bsz, seq, d = x.shape
    depth = w_ada.shape[0]
    heads = _RET_HEADS
    pool_width = pool_scale.shape[1]
    dh = (w_in.shape[2] - pool_width) // (4 * heads)
    assert pool_width == len(_POOL_WINDOWS) * dh and seq % _RET_CHUNK == 0 and dh == _RET_CHUNK
    assert all(w & (w - 1) == 0 and w // 2 <= _POOL_EDGE_ROWS for w in _POOL_WINDOWS)

    inv = 1.0 / (_ROPE_BASE ** np.linspace(0.0, 1.0, dh // 2))
    ang = np.arange(seq)[:, None] * inv[None, :]
    cos, sin = jnp.asarray(np.cos(ang), _F32), jnp.asarray(np.sin(ang), _F32)

    xf = x.reshape(bsz * seq, d)
    for l in range(depth):
        mod3 = _adaln(c, w_ada[l], b_ada[l]).reshape(bsz, _N_MOD, d)
        *parts, wo, wd, wgu = _inproj(xf, mod3, norm1_g[l].reshape(1, d), cos, sin, w_in[l].astype(_BF16), seq,
                                      pool_width, [w_out[l], w_down[l]], [w_gate[l], w_up[l]])
        pool_out, ret_out = _mixer(
            parts, pool_w[l].astype(_BF16), pool_scale[l].reshape(len(_POOL_WINDOWS), 1, dh),
            ret_decay_fwd[l].astype(_F32), ret_decay_bwd[l].astype(_F32), bsz, seq)
        x1, h2 = _outproj(pool_out, ret_out, xf, mod3, norm2_g[l].reshape(1, d), wo, seq)
        xf = _ffn(h2, x1, mod3, wgu, wd, final_g.reshape(1, d), seq, final_norm=(l == depth - 1))
    return xf.reshape(bsz, seq, d)
```

```python
import functools

import jax
import jax.numpy as jnp
import numpy as np
from jax import lax
from jax.experimental import pallas as pl
from jax.experimental.pallas import tpu as pltpu

_EPS = 1e-6
_POOL_WINDOWS = (2, 4, 8, 16)
_RET_HEADS = 4
_ROPE_BASE = 10000.0
_N_MOD = 6

_V7X_VMEM_BYTES = 64 * 1024 * 1024
_VMEM_UNCOUNTED_BYTES = 8 * 1024 * 1024
_VMEM_RESERVED_BYTES = 2 * 1024 * 1024
_LANES = 128
_BF16_ROWS = 16
_RET_CHUNK = 256
_POOL_HALO = 16
_POOL_SRC_ROWS = 256
_POOL_EDGE_ROWS = 8
_NORM_ROWS = 128
_DOT_ROWS = 256
_MIXER_HEADS_PER_STEP = 2

_F32 = jnp.float32
_BF16 = jnp.bfloat16


def _vmem_limit(nbytes):
    return int(min(_V7X_VMEM_BYTES - _VMEM_RESERVED_BYTES, nbytes + _VMEM_UNCOUNTED_BYTES))


def _nbytes(shape, dtype):
    n = 1
    for s in shape:
        n *= s
    return n * jnp.dtype(dtype).itemsize


def _dot(a, b):
    return jnp.dot(a, b, preferred_element_type=_F32)


def _silu(x):
    return x * jax.nn.sigmoid(x)


def _rms_rows(x):
    return x * lax.rsqrt(jnp.mean(x * x, axis=-1, keepdims=True) + _EPS)


def _col_blocks(d):
    return [slice(c0, c0 + _LANES) for c0 in range(0, d, _LANES)]


def _rmsnorm_rows(src_ref, dst_ref, row0, nrows, gain, shift=None):
    d = src_ref.shape[1]
    for r0 in range(row0, row0 + nrows, _NORM_ROWS):
        rows = slice(r0, r0 + _NORM_ROWS)
        ss = None
        for cols in _col_blocks(d):
            xc = src_ref[rows, cols]
            ss = xc * xc if ss is None else ss + xc * xc
        rs = lax.rsqrt(jnp.sum(ss, axis=-1, keepdims=True) / d + _EPS)
        for cols in _col_blocks(d):
            y = src_ref[rows, cols] * rs * gain[:, cols]
            if shift is not None:
                y = y + shift[:, cols]
            dst_ref[rows, cols] = y.astype(dst_ref.dtype)


def _adaln_kernel(c_ref, w_ref, b_ref, o_ref):
    c = c_ref[...]
    o_ref[...] = _dot(_silu(c).astype(_BF16), w_ref[...].astype(_BF16)) + b_ref[...]


def _adaln(c, w, b, *, tn=1024):
    bsz, d = c.shape
    n = w.shape[1]
    vmem = 2 * _nbytes((d, tn), _F32) + _nbytes((d, tn), _BF16) + 4 * _nbytes((bsz, d), _F32)
    return pl.pallas_call(
        _adaln_kernel,
        grid=(n // tn,),
        in_specs=[
            pl.BlockSpec((bsz, d), lambda j: (0, 0)),
            pl.BlockSpec((d, tn), lambda j: (0, j)),
            pl.BlockSpec((1, tn), lambda j: (0, j)),
        ],
        out_specs=pl.BlockSpec((bsz, tn), lambda j: (0, j)),
        out_shape=jax.ShapeDtypeStruct((bsz, n), _F32),
        compiler_params=pltpu.CompilerParams(
            dimension_semantics=("arbitrary",), vmem_limit_bytes=_vmem_limit(vmem)),
        name="adaln",
    )(c, w, b.reshape(1, n))


def _slab_specs(weights, steps, index_map):
    specs, shapes, nbytes = [], [], 0
    for w in weights:
        rows, cols = w.shape
        assert rows % steps == 0 and (rows // steps) % _BF16_ROWS == 0
        specs.append(pl.BlockSpec((rows // steps, cols), index_map))
        shapes.append(jax.ShapeDtypeStruct(w.shape, _BF16))
        nbytes += 2 * _nbytes((rows // steps, cols), _F32) + 2 * _nbytes((rows // steps, cols), _BF16)
    return specs, shapes, nbytes


def _cast_slabs(src_refs, dst_refs):
    for src, dst in zip(src_refs, dst_refs):
        dst[...] = src[...].astype(dst.dtype)


def _cast_gate_up(wg_ref, wu_ref, wgu_ref):
    for g in range(wg_ref.shape[1] // _LANES):
        cols = slice(g * _LANES, (g + 1) * _LANES)
        wgu_ref[:, 2 * g * _LANES:(2 * g + 1) * _LANES] = wg_ref[:, cols].astype(wgu_ref.dtype)
        wgu_ref[:, (2 * g + 1) * _LANES:(2 * g + 2) * _LANES] = wu_ref[:, cols].astype(wgu_ref.dtype)


def _inproj_kernel(x_ref, mod_ref, gain_ref, cos_ref, sin_ref, w_ref, *rest, n_cast, pool_width, heads):
    cast_in, (wg_ref, wu_ref) = rest[:n_cast], rest[n_cast:n_cast + 2]
    u_ref, q_ref, k_ref, v_ref, g_ref, *cast_out, wgu_ref, h_ref = rest[n_cast + 2:]
    groups, _, wide = q_ref.shape
    width = groups * wide
    dh = width // heads
    half = dh // 2
    scale = gain_ref[...] * (1.0 + mod_ref[0, 1:2, :])
    shift = mod_ref[0, 0:1, :]

    def store(dst_ref, rows, val):
        for j in range(dst_ref.shape[0]):
            dst_ref[j, rows, :] = val[:, j * wide:(j + 1) * wide].astype(dst_ref.dtype)

    for r0 in range(0, x_ref.shape[0], _DOT_ROWS):
        rows = slice(r0, r0 + _DOT_ROWS)
        _rmsnorm_rows(x_ref, h_ref, r0, _DOT_ROWS, scale, shift)
        h = h_ref[rows, :]
        c0 = pool_width + 3 * width
        store(g_ref, rows, _silu(_dot(h, w_ref[:, c0:c0 + width])))
        cos, sin = cos_ref[rows, :], sin_ref[rows, :]
        for part, dst_ref in enumerate((q_ref, k_ref)):
            c0 = pool_width + part * width
            t = _dot(h, w_ref[:, c0:c0 + width])
            for hd in range(heads):
                t1 = t[:, hd * dh:hd * dh + half]
                t2 = t[:, hd * dh + half:(hd + 1) * dh]
                j, c1 = (hd * dh) // wide, (hd * dh) % wide
                dst_ref[j, rows, c1:c1 + half] = (t1 * cos - t2 * sin).astype(dst_ref.dtype)
                dst_ref[j, rows, c1 + half:c1 + dh] = (t1 * sin + t2 * cos).astype(dst_ref.dtype)
        if r0 == 0:
            _cast_slabs(cast_in, cast_out)
            _cast_gate_up(wg_ref, wu_ref, wgu_ref)
        c0 = pool_width + 2 * width
        store(v_ref, rows, _dot(h, w_ref[:, c0:c0 + width]))
        store(u_ref, rows, _dot(h, w_ref[:, 0:pool_width]))


def _inproj(x2d, mod3, gain, cos, sin, w, seq, pool_width, cast_weights, gate_up, *, tm=512):
    t, d = x2d.shape
    n = w.shape[1]
    per_batch = seq // tm
    steps = t // tm
    wide = _MIXER_HEADS_PER_STEP * (n - pool_width) // (4 * _RET_HEADS)
    groups = _RET_HEADS // _MIXER_HEADS_PER_STEP
    assert pool_width == groups * wide
    part_spec = pl.BlockSpec((groups, tm, wide), lambda i: (0, i, 0))
    part_shape = jax.ShapeDtypeStruct((groups, t, wide), _BF16)
    cast_specs, cast_shapes, cast_bytes = _slab_specs(cast_weights, steps, lambda i: (i, 0))
    gu_specs, _, gu_bytes = _slab_specs(gate_up, steps, lambda i: (i, 0))
    gu_rows, dff = gate_up[0].shape
    assert gate_up[1].shape == (gu_rows, dff) and dff % _LANES == 0
    vmem = (2 * _nbytes((tm, d), _F32) + _nbytes((d, n), _BF16) + 2 * _nbytes((tm, n), _BF16)
            + _nbytes((tm, d), _BF16) + _nbytes((_DOT_ROWS, n), _F32) + 4 * _nbytes(cos.shape, _F32) // per_batch
            + cast_bytes + gu_bytes)
    return pl.pallas_call(
        functools.partial(_inproj_kernel, n_cast=len(cast_weights), pool_width=pool_width, heads=_RET_HEADS),
        grid=(t // tm,),
        in_specs=[
            pl.BlockSpec((tm, d), lambda i: (i, 0)),
            pl.BlockSpec((1, _N_MOD, d), lambda i: (i // per_batch, 0, 0)),
            pl.BlockSpec((1, d), lambda i: (0, 0)),
            pl.BlockSpec((tm, cos.shape[1]), lambda i: (i % per_batch, 0)),
            pl.BlockSpec((tm, sin.shape[1]), lambda i: (i % per_batch, 0)),
            pl.BlockSpec((d, n), lambda i: (0, 0), pipeline_mode=pl.Buffered(1)),
        ] + cast_specs + gu_specs,
        out_specs=([part_spec] * 5 + cast_specs
                   + [pl.BlockSpec((gu_rows // steps, 2 * dff), lambda i: (i, 0))]),
        out_shape=([part_shape] * 5 + cast_shapes
                   + [jax.ShapeDtypeStruct((gu_rows, 2 * dff), _BF16)]),
        scratch_shapes=[pltpu.VMEM((tm, d), _BF16)],
        compiler_params=pltpu.CompilerParams(
            dimension_semantics=("arbitrary",), vmem_limit_bytes=_vmem_limit(vmem)),
        name="inproj",
    )(x2d, mod3, gain, cos, sin, w, *cast_weights, *gate_up)


def _pool_tiles(seq):
    step = _POOL_SRC_ROWS - 2 * _POOL_HALO
    return [(s, min(step, seq - s)) for s in range(0, seq, step)]


def _pool_pad_rows(seq):
    return _pool_tiles(seq)[-1][0] + _POOL_SRC_ROWS


def _pool_chunks(u_ref, pad_ref, pooled_ref, cols, group, seq):
    tiles = _pool_tiles(seq)
    halo, src_rows = _POOL_HALO, _POOL_SRC_ROWS
    out_rows = src_rows - 2 * halo
    edge = _POOL_EDGE_ROWS
    assert tiles[-1][1] >= edge
    half = jnp.int32(0)
    inv_w = jnp.float32(0)
    for gi, window in enumerate(_POOL_WINDOWS):
        half = jnp.where(group == gi, jnp.int32(window // 2), half)
        inv_w = jnp.where(group == gi, jnp.float32(1.0 / window), inv_w)

    width_c = cols.stop - cols.start
    pad_ref[0:halo, cols] = jnp.zeros((halo, width_c), pad_ref.dtype)
    pad_ref[halo + seq:, cols] = jnp.zeros((pad_ref.shape[0] - halo - seq, width_c), pad_ref.dtype)
    pad_ref[halo:halo + seq, cols] = u_ref[:, cols]

    rel = (lax.broadcasted_iota(jnp.int32, (out_rows, src_rows), 1) - halo
           - lax.broadcasted_iota(jnp.int32, (out_rows, src_rows), 0))
    band = (jnp.where((rel >= -half) & (rel < half), inv_w, 0.0)
            - jnp.where(rel == 0, 1.0, 0.0)).astype(_BF16)

    t_lo = lax.broadcasted_iota(jnp.int32, (edge, width_c), 0)
    t_hi = t_lo + (seq - edge)
    width = (2 * half).astype(_F32)
    ratio_lo = width / (t_lo + half - jnp.maximum(t_lo - half, 0)).astype(_F32)
    ratio_hi = width / (jnp.minimum(t_hi + half, seq) - (t_hi - half)).astype(_F32)

    def pool_tile(ti):
        s, rows = tiles[ti]
        pooled = _dot(band, pad_ref[s:s + src_rows, cols])
        if ti == 0:
            u_lo = u_ref[0:edge, cols].astype(_F32)
            pooled = jnp.concatenate([(pooled[0:edge] + u_lo) * ratio_lo - u_lo, pooled[edge:]], axis=0)
        if ti == len(tiles) - 1:
            lo = seq - edge - s
            u_hi = u_ref[seq - edge:seq, cols].astype(_F32)
            pooled = jnp.concatenate(
                [pooled[:lo], (pooled[lo:lo + edge] + u_hi) * ratio_hi - u_hi, pooled[lo + edge:]], axis=0)
        pooled_ref[s:s + rows, cols] = pooled[:rows].astype(pooled_ref.dtype)

    return pool_tile


def _mixer_kernel(u_ref, q_ref, k_ref, v_ref, gate_ref, pw_ref, ps_ref, df_ref, db_ref, pool_ref, ret_ref,
                  pad_ref, pooled_ref, kb_ref, acc_ref, dec_ref, sf_ref, sb_ref):
    u_ref, q_ref, k_ref, v_ref, gate_ref, pool_ref, ret_ref = (
        r.at[0] for r in (u_ref, q_ref, k_ref, v_ref, gate_ref, pool_ref, ret_ref))
    seq = q_ref.shape[0]
    nh, dh = pw_ref.shape[0], pw_ref.shape[-1]
    c = _RET_CHUNK
    n = seq // c
    head_cols = [slice(hd * dh, (hd + 1) * dh) for hd in range(nh)]

    def tn_dot(a, b):
        return lax.dot_general(a, b, (((0,), (0,)), ((), ())), preferred_element_type=_F32)

    def nt_dot(a, b):
        return lax.dot_general(a, b, (((1,), (1,)), ((), ())), preferred_element_type=_F32)

    ri = lax.broadcasted_iota(jnp.int32, (c, c), 0).astype(_F32)
    cj = lax.broadcasted_iota(jnp.int32, (c, c), 1).astype(_F32)
    diff = ri - cj
    qk_scale = dh ** -0.5
    pool_tile, chunk_f, chunk_b = [], [], []
    n_pool_tiles = len(_pool_tiles(seq))
    for hd, cols in enumerate(head_cols):
        pool_tile.append(_pool_chunks(u_ref, pad_ref, pooled_ref, cols, pl.program_id(1) * nh + hd, seq))
        head = pl.program_id(1) * nh + hd
        lg_f = -jnp.exp(jnp.full((1, c), df_ref[head], _F32))
        lg_b = -jnp.exp(jnp.full((1, c), db_ref[head], _F32))
        dec_ref[hd, 0] = jnp.exp(jnp.where(diff >= 0, diff * lg_f, -diff * lg_b)) * qk_scale
        dec_ref[hd, 1] = jnp.exp((ri + 1.0) * lg_f) * qk_scale
        dec_ref[hd, 2] = jnp.exp((c - 1.0 - ri) * lg_f)
        dec_ref[hd, 3] = jnp.exp((c - ri) * lg_b) * qk_scale
        dec_ref[hd, 4] = jnp.exp(ri * lg_b)
        chunk_f.append(jnp.exp(c * lg_f))
        chunk_b.append(jnp.exp(c * lg_b))

    for ci in range(n):
        rows = slice(ci * c, (ci + 1) * c)
        for hd, cols in enumerate(head_cols):
            q = q_ref[rows, cols]
            k = k_ref[rows, cols]
            k32 = k.astype(_F32)
            if ci > 0:
                kb_ref[rows, cols] = (k32 * dec_ref[hd, 4]).astype(_BF16)
            v = v_ref[rows, cols]
            scores = nt_dot(q, k) * dec_ref[hd, 0]
            out = _dot(scores.astype(_BF16), v)
            if ci > 0:
                out = out + _dot(q, sf_ref[hd].astype(_BF16)) * dec_ref[hd, 1]
            acc_ref[rows, cols] = out
            if ci < n - 1:
                upd = tn_dot((k32 * dec_ref[hd, 2]).astype(_BF16), v)
                sf_ref[hd] = upd if ci == 0 else sf_ref[hd] * chunk_f[hd] + upd
            for ti in range(ci, n_pool_tiles, n):
                pool_tile[hd](ti)

    for hd, cols in enumerate(head_cols):
        pool_ref[:, cols] = (_dot(pooled_ref[:, cols], pw_ref[hd]) * ps_ref[hd]).astype(pool_ref.dtype)

    for ci in reversed(range(n)):
        rows = slice(ci * c, (ci + 1) * c)
        for hd, cols in enumerate(head_cols):
            out = acc_ref[rows, cols]
            if ci < n - 1:
                out = out + _dot(q_ref[rows, cols], sb_ref[hd].astype(_BF16)) * dec_ref[hd, 3]
            ret = _rms_rows(out)
            ret_ref[rows, cols] = (gate_ref[rows, cols].astype(_F32) * ret).astype(ret_ref.dtype)
            if ci > 0:
                upd = tn_dot(kb_ref[rows, cols], v_ref[rows, cols])
                sb_ref[hd] = upd if ci == n - 1 else sb_ref[hd] * chunk_b[hd] + upd


def _mixer(parts, pool_w, pool_scale, dec_f, dec_b, bsz, seq):
    groups, t, wide = parts[0].shape
    heads = _RET_HEADS
    dh = pool_w.shape[-1]
    nh = _MIXER_HEADS_PER_STEP
    assert heads == groups * nh and len(_POOL_WINDOWS) == heads and wide == nh * dh
    seq_block = pl.BlockSpec((1, seq, wide), lambda b, j: (j, b, 0))
    per_head = lambda shape: pl.BlockSpec((nh,) + shape, lambda b, j: (j, 0, 0))
    scratch = [
        pltpu.VMEM((_pool_pad_rows(seq), wide), _BF16),
        pltpu.VMEM((seq, wide), _BF16),
        pltpu.VMEM((seq, wide), _BF16),
        pltpu.VMEM((seq, wide), _F32),
        pltpu.VMEM((nh, 5, _RET_CHUNK, _RET_CHUNK), _F32),
        pltpu.VMEM((nh, dh, dh), _F32),
        pltpu.VMEM((nh, dh, dh), _F32),
    ]
    vmem = (2 * 7 * _nbytes((seq, wide), _BF16) + 4 * _nbytes((seq, wide), _F32)
            + 8 * nh * _nbytes((dh, dh), _F32))
    return pl.pallas_call(
        _mixer_kernel,
        grid=(bsz, heads // nh),
        in_specs=[
            seq_block, seq_block, seq_block, seq_block, seq_block,
            per_head((dh, dh)),
            per_head((1, dh)),
            pl.BlockSpec(memory_space=pltpu.SMEM),
            pl.BlockSpec(memory_space=pltpu.SMEM),
        ],
        out_specs=[seq_block, seq_block],
        out_shape=[jax.ShapeDtypeStruct((groups, t, wide), _BF16),
                   jax.ShapeDtypeStruct((groups, t, wide), _BF16)],
        scratch_shapes=scratch,
        compiler_params=pltpu.CompilerParams(
            dimension_semantics=("arbitrary", "arbitrary"), vmem_limit_bytes=_vmem_limit(vmem)),
        name="mixer",
    )(*parts, pool_w, pool_scale, dec_f, dec_b)


def _outproj_kernel(p_ref, r_ref, x_ref, mod_ref, gain_ref, w_ref, x1_ref, h2_ref):
    groups, _, wide = p_ref.shape
    gate = mod_ref[0, 2:3, :]
    scale = gain_ref[...] * (1.0 + mod_ref[0, 4:5, :])
    shift = mod_ref[0, 3:4, :]
    for r0 in range(0, x_ref.shape[0], _DOT_ROWS):
        rows = slice(r0, r0 + _DOT_ROWS)
        mixed = None
        for part, src_ref in enumerate((p_ref, r_ref)):
            for j in range(groups):
                k0 = (part * groups + j) * wide
                term = _dot(src_ref[j, rows, :], w_ref[k0:k0 + wide, :])
                mixed = term if mixed is None else mixed + term
        x1_ref[rows, :] = x_ref[rows, :] + gate * mixed
        _rmsnorm_rows(x1_ref, h2_ref, r0, _DOT_ROWS, scale, shift)


def _outproj(pool_out, ret_out, x2d, mod3, gain, w, seq, *, tm=512):
    t, d = x2d.shape
    groups, _, wide = pool_out.shape
    kp = kr = groups * wide
    assert ret_out.shape == pool_out.shape and w.shape[0] == kp + kr
    per_batch = seq // tm
    vmem = (2 * _nbytes((tm, kp + kr), _BF16) + 4 * _nbytes((tm, d), _F32) + 2 * _nbytes((tm, d), _BF16)
            + _nbytes((kp + kr, d), _BF16) + _nbytes((_DOT_ROWS, d), _F32))
    return pl.pallas_call(
        _outproj_kernel,
        grid=(t // tm,),
        in_specs=[
            pl.BlockSpec((groups, tm, wide), lambda i: (0, i, 0)),
            pl.BlockSpec((groups, tm, wide), lambda i: (0, i, 0)),
            pl.BlockSpec((tm, d), lambda i: (i, 0)),
            pl.BlockSpec((1, _N_MOD, d), lambda i: (i // per_batch, 0, 0)),
            pl.BlockSpec((1, d), lambda i: (0, 0)),
            pl.BlockSpec((kp + kr, d), lambda i: (0, 0), pipeline_mode=pl.Buffered(1)),
        ],
        out_specs=[pl.BlockSpec((tm, d), lambda i: (i, 0)),
                   pl.BlockSpec((tm, d), lambda i: (i, 0))],
        out_shape=[jax.ShapeDtypeStruct((t, d), _F32),
                   jax.ShapeDtypeStruct((t, d), _BF16)],
        compiler_params=pltpu.CompilerParams(
            dimension_semantics=("arbitrary",), vmem_limit_bytes=_vmem_limit(vmem)),
        name="outproj",
    )(pool_out, ret_out, x2d, mod3, gain, w)


def _ffn_kernel(h2_ref, wgu_ref, wd_ref, x1_hbm, mod_ref, gain_ref, o_ref, x1_buf, sem, *, final_norm):
    i, f = pl.program_id(0), pl.program_id(1)
    tm, d = o_ref.shape

    def x1_copy():
        return pltpu.make_async_copy(x1_hbm.at[pl.ds(pl.multiple_of(i * tm, tm), tm), :], x1_buf, sem.at[0])

    last = pl.num_programs(1) - 1

    def hidden():
        gu = _dot(h2_ref[...], wgu_ref[...])
        parts = [_silu(gu[:, c0:c0 + _LANES]) * gu[:, c0 + _LANES:c0 + 2 * _LANES]
                 for c0 in range(0, gu.shape[1], 2 * _LANES)]
        return jnp.concatenate(parts, axis=1).astype(_BF16)

    def finish_rows(r0):
        rows = slice(r0, r0 + _NORM_ROWS)
        gate = mod_ref[0, 5:6, :]
        gain = gain_ref[...]
        ss = None
        for cols in _col_blocks(d):
            x2 = x1_buf[rows, cols] + gate[:, cols] * o_ref[rows, cols]
            o_ref[rows, cols] = x2
            ss = x2 * x2 if ss is None else ss + x2 * x2
        if final_norm:
            rs = lax.rsqrt(jnp.sum(ss, axis=-1, keepdims=True) / d + _EPS)
            for cols in _col_blocks(d):
                o_ref[rows, cols] = o_ref[rows, cols] * rs * gain[:, cols]

    @pl.when(f == 0)
    def _():
        x1_copy().start()
        o_ref[...] = _dot(hidden(), wd_ref[...])

    @pl.when((f > 0) & (f < last))
    def _():
        o_ref[...] += _dot(hidden(), wd_ref[...])

    @pl.when(f == last)
    def _():
        x1_copy().wait()
        hid = hidden()
        for r0 in range(0, tm, _DOT_ROWS):
            rows = slice(r0, r0 + _DOT_ROWS)
            o_ref[rows, :] += _dot(hid[rows, :], wd_ref[...])
            for s0 in range(r0, r0 + _DOT_ROWS, _NORM_ROWS):
                finish_rows(s0)


def _ffn(h2, x1, mod3, wgu, wd, final_gain, seq, *, final_norm, tm=1024, tf=512):
    t, d = x1.shape
    dff = wd.shape[0]
    assert wgu.shape == (d, 2 * dff) and tf % _LANES == 0
    assert dff // tf >= 2
    per_batch = seq // tm
    vmem = (2 * _nbytes((tm, d), _BF16) + 6 * _nbytes((d, tf), _BF16) + 3 * _nbytes((tm, d), _F32)
            + 3 * _nbytes((tm, tf), _F32))
    return pl.pallas_call(
        functools.partial(_ffn_kernel, final_norm=final_norm),
        grid=(t // tm, dff // tf),
        in_specs=[
            pl.BlockSpec((tm, d), lambda i, f: (i, 0)),
            pl.BlockSpec((d, 2 * tf), lambda i, f: (0, f)),
            pl.BlockSpec((tf, d), lambda i, f: (f, 0)),
            pl.BlockSpec(memory_space=pl.ANY),
            pl.BlockSpec((1, _N_MOD, d), lambda i, f: (i // per_batch, 0, 0)),
            pl.BlockSpec((1, d), lambda i, f: (0, 0)),
        ],
        out_specs=pl.BlockSpec((tm, d), lambda i, f: (i, 0)),
        out_shape=jax.ShapeDtypeStruct((t, d), _F32),
        scratch_shapes=[pltpu.VMEM((tm, d), _F32), pltpu.SemaphoreType.DMA((1,))],
        compiler_params=pltpu.CompilerParams(
            dimension_semantics=("arbitrary", "arbitrary"), vmem_limit_bytes=_vmem_limit(vmem)),
        name="ffn",
    )(h2, wgu, wd, x1, mod3, final_gain)


def kernel(x, c, w_ada, b_ada, norm1_g, w_in, pool_w, pool_scale, ret_decay_fwd, ret_decay_bwd,
           w_out, norm2_g, w_gate, w_up, w_down, final_g):
    bsz, seq, d = x.shape
    depth = w_ada.shape[0]
    heads = _RET_HEADS
    pool_width = pool_scale.shape[1]
    dh = (w_in.shape[2] - pool_width) // (4 * heads)
    assert pool_width == len(_POOL_WINDOWS) * dh and seq % _RET_CHUNK == 0 and dh == _RET_CHUNK
    assert all(w & (w - 1) == 0 and w // 2 <= _POOL_EDGE_ROWS for w in _POOL_WINDOWS)

    inv = 1.0 / (_ROPE_BASE ** np.linspace(0.0, 1.0, dh // 2))
    ang = np.arange(seq)[:, None] * inv[None, :]
    cos, sin = jnp.asarray(np.cos(ang), _F32), jnp.asarray(np.sin(ang), _F32)

    xf = x.reshape(bsz * seq, d)
    for l in range(depth):
        mod3 = _adaln(c, w_ada[l], b_ada[l]).reshape(bsz, _N_MOD, d)
        *parts, wo, wd, wgu = _inproj(xf, mod3, norm1_g[l].reshape(1, d), cos, sin, w_in[l].astype(_BF16), seq,
                                      pool_width, [w_out[l], w_down[l]], [w_gate[l], w_up[l]])
        pool_out, ret_out = _mixer(
            parts, pool_w[l].astype(_BF16), pool_scale[l].reshape(len(_POOL_WINDOWS), 1, dh),
            ret_decay_fwd[l].astype(_F32), ret_decay_bwd[l].astype(_F32), bsz, seq)
        x1, h2 = _outproj(pool_out, ret_out, xf, mod3, norm2_g[l].reshape(1, d), wo, seq)
        xf = _ffn(h2, x1, mod3, wgu, wd, final_g.reshape(1, d), seq, final_norm=(l == depth - 1))
    return xf.reshape(bsz, seq, d)
```

```python
import functools

import jax
import jax.numpy as jnp
import numpy as np
from jax import lax
from jax.experimental import pallas as pl
from jax.experimental.pallas import tpu as pltpu

_EPS = 1e-6
_POOL_WINDOWS = (2, 4, 8, 16)
_RET_HEADS = 4
_ROPE_BASE = 10000.0
_N_MOD = 6

_V7X_VMEM_BYTES = 64 * 1024 * 1024
_VMEM_UNCOUNTED_BYTES = 8 * 1024 * 1024
_VMEM_RESERVED_BYTES = 2 * 1024 * 1024
_LANES = 128
_BF16_ROWS = 16
_RET_CHUNK = 256
_POOL_HALO = 16
_POOL_SRC_ROWS = 256
_POOL_EDGE_ROWS = 8
_NORM_ROWS = 128
_DOT_ROWS = 256
_MIXER_HEADS_PER_STEP = 2

_F32 = jnp.float32
_BF16 = jnp.bfloat16


def _vmem_limit(nbytes):
    return int(min(_V7X_VMEM_BYTES - _VMEM_RESERVED_BYTES, nbytes + _VMEM_UNCOUNTED_BYTES))


def _nbytes(shape, dtype):
    n = 1
    for s in shape:
        n *= s
    return n * jnp.dtype(dtype).itemsize


def _dot(a, b):
    return jnp.dot(a, b, preferred_element_type=_F32)


def _silu(x):
    return x * jax.nn.sigmoid(x)


def _rms_rows(x):
    return x * lax.rsqrt(jnp.mean(x * x, axis=-1, keepdims=True) + _EPS)


def _col_blocks(d):
    return [slice(c0, c0 + _LANES) for c0 in range(0, d, _LANES)]


def _rmsnorm_rows(src_ref, dst_ref, row0, nrows, gain, shift=None):
    d = src_ref.shape[1]
    for r0 in range(row0, row0 + nrows, _NORM_ROWS):
        rows = slice(r0, r0 + _NORM_ROWS)
        ss = None
        for cols in _col_blocks(d):
            xc = src_ref[rows, cols]
            ss = xc * xc if ss is None else ss + xc * xc
        rs = lax.rsqrt(jnp.sum(ss, axis=-1, keepdims=True) / d + _EPS)
        for cols in _col_blocks(d):
            y = src_ref[rows, cols] * rs * gain[:, cols]
            if shift is not None:
                y = y + shift[:, cols]
            dst_ref[rows, cols] = y.astype(dst_ref.dtype)


def _adaln_kernel(c_ref, w_ref, b_ref, o_ref):
    c = c_ref[...]
    o_ref[...] = _dot(_silu(c).astype(_BF16), w_ref[...].astype(_BF16)) + b_ref[...]


def _adaln(c, w, b, *, tn=1024):
    bsz, d = c.shape
    n = w.shape[1]
    vmem = 2 * _nbytes((d, tn), _F32) + _nbytes((d, tn), _BF16) + 4 * _nbytes((bsz, d), _F32)
    return pl.pallas_call(
        _adaln_kernel,
        grid=(n // tn,),
        in_specs=[
            pl.BlockSpec((bsz, d), lambda j: (0, 0)),
            pl.BlockSpec((d, tn), lambda j: (0, j)),
            pl.BlockSpec((1, tn), lambda j: (0, j)),
        ],
        out_specs=pl.BlockSpec((bsz, tn), lambda j: (0, j)),
        out_shape=jax.ShapeDtypeStruct((bsz, n), _F32),
        compiler_params=pltpu.CompilerParams(
            dimension_semantics=("arbitrary",), vmem_limit_bytes=_vmem_limit(vmem)),
        name="adaln",
    )(c, w, b.reshape(1, n))


def _slab_specs(weights, steps, index_map):
    specs, shapes, nbytes = [], [], 0
    for w in weights:
        rows, cols = w.shape
        assert rows % steps == 0 and (rows // steps) % _BF16_ROWS == 0
        specs.append(pl.BlockSpec((rows // steps, cols), index_map))
        shapes.append(jax.ShapeDtypeStruct(w.shape, _BF16))
        nbytes += 2 * _nbytes((rows // steps, cols), _F32) + 2 * _nbytes((rows // steps, cols), _BF16)
    return specs, shapes, nbytes


def _cast_slabs(src_refs, dst_refs):
    for src, dst in zip(src_refs, dst_refs):
        dst[...] = src[...].astype(dst.dtype)


def _cast_gate_up(wg_ref, wu_ref, wgu_ref):
    for g in range(wg_ref.shape[1] // _LANES):
        cols = slice(g * _LANES, (g + 1) * _LANES)
        wgu_ref[:, 2 * g * _LANES:(2 * g + 1) * _LANES] = wg_ref[:, cols].astype(wgu_ref.dtype)
        wgu_ref[:, (2 * g + 1) * _LANES:(2 * g + 2) * _LANES] = wu_ref[:, cols].astype(wgu_ref.dtype)


def _inproj_kernel(x_ref, mod_ref, gain_ref, cos_ref, sin_ref, w_ref, *rest, n_cast, pool_width, heads):
    cast_in, (wg_ref, wu_ref) = rest[:n_cast], rest[n_cast:n_cast + 2]
    u_ref, q_ref, k_ref, v_ref, g_ref, *cast_out, wgu_ref, h_ref = rest[n_cast + 2:]
    groups, _, wide = q_ref.shape
    width = groups * wide
    dh = width // heads
    half = dh // 2
    scale = gain_ref[...] * (1.0 + mod_ref[0, 1:2, :])
    shift = mod_ref[0, 0:1, :]

    def store(dst_ref, rows, val):
        for j in range(dst_ref.shape[0]):
            dst_ref[j, rows, :] = val[:, j * wide:(j + 1) * wide].astype(dst_ref.dtype)

    for r0 in range(0, x_ref.shape[0], _DOT_ROWS):
        rows = slice(r0, r0 + _DOT_ROWS)
        _rmsnorm_rows(x_ref, h_ref, r0, _DOT_ROWS, scale, shift)
        h = h_ref[rows, :]
        c0 = pool_width + 3 * width
        store(g_ref, rows, _silu(_dot(h, w_ref[:, c0:c0 + width])))
        cos, sin = cos_ref[rows, :], sin_ref[rows, :]
        for part, dst_ref in enumerate((q_ref, k_ref)):
            c0 = pool_width + part * width
            t = _dot(h, w_ref[:, c0:c0 + width])
            for hd in range(heads):
                t1 = t[:, hd * dh:hd * dh + half]
                t2 = t[:, hd * dh + half:(hd + 1) * dh]
                j, c1 = (hd * dh) // wide, (hd * dh) % wide
                dst_ref[j, rows, c1:c1 + half] = (t1 * cos - t2 * sin).astype(dst_ref.dtype)
                dst_ref[j, rows, c1 + half:c1 + dh] = (t1 * sin + t2 * cos).astype(dst_ref.dtype)
        if r0 == 0:
            _cast_slabs(cast_in, cast_out)
            _cast_gate_up(wg_ref, wu_ref, wgu_ref)
        c0 = pool_width + 2 * width
        store(v_ref, rows, _dot(h, w_ref[:, c0:c0 + width]))
        store(u_ref, rows, _dot(h, w_ref[:, 0:pool_width]))


def _inproj(x2d, mod3, gain, cos, sin, w, seq, pool_width, cast_weights, gate_up, *, tm=512):
    t, d = x2d.shape
    n = w.shape[1]
    per_batch = seq // tm
    steps = t // tm
    wide = _MIXER_HEADS_PER_STEP * (n - pool_width) // (4 * _RET_HEADS)
    groups = _RET_HEADS // _MIXER_HEADS_PER_STEP
    assert pool_width == groups * wide
    part_spec = pl.BlockSpec((groups, tm, wide), lambda i: (0, i, 0))
    part_shape = jax.ShapeDtypeStruct((groups, t, wide), _BF16)
    cast_specs, cast_shapes, cast_bytes = _slab_specs(cast_weights, steps, lambda i: (i, 0))
    gu_specs, _, gu_bytes = _slab_specs(gate_up, steps, lambda i: (i, 0))
    gu_rows, dff = gate_up[0].shape
    assert gate_up[1].shape == (gu_rows, dff) and dff % _LANES == 0
    vmem = (2 * _nbytes((tm, d), _F32) + _nbytes((d, n), _BF16) + 2 * _nbytes((tm, n), _BF16)
            + _nbytes((tm, d), _BF16) + _nbytes((_DOT_ROWS, n), _F32) + 4 * _nbytes(cos.shape, _F32) // per_batch
            + cast_bytes + gu_bytes)
    return pl.pallas_call(
        functools.partial(_inproj_kernel, n_cast=len(cast_weights), pool_width=pool_width, heads=_RET_HEADS),
        grid=(t // tm,),
        in_specs=[
            pl.BlockSpec((tm, d), lambda i: (i, 0)),
            pl.BlockSpec((1, _N_MOD, d), lambda i: (i // per_batch, 0, 0)),
            pl.BlockSpec((1, d), lambda i: (0, 0)),
            pl.BlockSpec((tm, cos.shape[1]), lambda i: (i % per_batch, 0)),
            pl.BlockSpec((tm, sin.shape[1]), lambda i: (i % per_batch, 0)),
            pl.BlockSpec((d, n), lambda i: (0, 0), pipeline_mode=pl.Buffered(1)),
        ] + cast_specs + gu_specs,
        out_specs=([part_spec] * 5 + cast_specs
                   + [pl.BlockSpec((gu_rows // steps, 2 * dff), lambda i: (i, 0))]),
        out_shape=([part_shape] * 5 + cast_shapes
                   + [jax.ShapeDtypeStruct((gu_rows, 2 * dff), _BF16)]),
        scratch_shapes=[pltpu.VMEM((tm, d), _BF16)],
        compiler_params=pltpu.CompilerParams(
            dimension_semantics=("arbitrary",), vmem_limit_bytes=_vmem_limit(vmem)),
        name="inproj",
    )(x2d, mod3, gain, cos, sin, w, *cast_weights, *gate_up)


def _pool_tiles(seq):
    step = _POOL_SRC_ROWS - 2 * _POOL_HALO
    return [(s, min(step, seq - s)) for s in range(0, seq, step)]


def _pool_pad_rows(seq):
    return _pool_tiles(seq)[-1][0] + _POOL_SRC_ROWS


def _pool_chunks(u_ref, pad_ref, pooled_ref, cols, group, seq):
    tiles = _pool_tiles(seq)
    halo, src_rows = _POOL_HALO, _POOL_SRC_ROWS
    out_rows = src_rows - 2 * halo
    edge = _POOL_EDGE_ROWS
    assert tiles[-1][1] >= edge
    half = jnp.int32(0)
    inv_w = jnp.float32(0)
    for gi, window in enumerate(_POOL_WINDOWS):
        half = jnp.where(group == gi, jnp.int32(window // 2), half)
        inv_w = jnp.where(group == gi, jnp.float32(1.0 / window), inv_w)

    width_c = cols.stop - cols.start
    pad_ref[0:halo, cols] = jnp.zeros((halo, width_c), pad_ref.dtype)
    pad_ref[halo + seq:, cols] = jnp.zeros((pad_ref.shape[0] - halo - seq, width_c), pad_ref.dtype)
    pad_ref[halo:halo + seq, cols] = u_ref[:, cols]

    rel = (lax.broadcasted_iota(jnp.int32, (out_rows, src_rows), 1) - halo
           - lax.broadcasted_iota(jnp.int32, (out_rows, src_rows), 0))
    band = (jnp.where((rel >= -half) & (rel < half), inv_w, 0.0)
            - jnp.where(rel == 0, 1.0, 0.0)).astype(_BF16)

    t_lo = lax.broadcasted_iota(jnp.int32, (edge, width_c), 0)
    t_hi = t_lo + (seq - edge)
    width = (2 * half).astype(_F32)
    ratio_lo = width / (t_lo + half - jnp.maximum(t_lo - half, 0)).astype(_F32)
    ratio_hi = width / (jnp.minimum(t_hi + half, seq) - (t_hi - half)).astype(_F32)

    def pool_tile(ti):
        s, rows = tiles[ti]
        pooled = _dot(band, pad_ref[s:s + src_rows, cols])
        if ti == 0:
            u_lo = u_ref[0:edge, cols].astype(_F32)
            pooled = jnp.concatenate([(pooled[0:edge] + u_lo) * ratio_lo - u_lo, pooled[edge:]], axis=0)
        if ti == len(tiles) - 1:
            lo = seq - edge - s
            u_hi = u_ref[seq - edge:seq, cols].astype(_F32)
            pooled = jnp.concatenate(
                [pooled[:lo], (pooled[lo:lo + edge] + u_hi) * ratio_hi - u_hi, pooled[lo + edge:]], axis=0)
        pooled_ref[s:s + rows, cols] = pooled[:rows].astype(pooled_ref.dtype)

    return pool_tile


def _mixer_kernel(u_ref, q_ref, k_ref, v_ref, gate_ref, pw_ref, ps_ref, df_ref, db_ref, pool_ref, ret_ref,
                  pad_ref, pooled_ref, kb_ref, acc_ref, dec_ref, sf_ref, sb_ref):
    u_ref, q_ref, k_ref, v_ref, gate_ref, pool_ref, ret_ref = (
        r.at[0] for r in (u_ref, q_ref, k_ref, v_ref, gate_ref, pool_ref, ret_ref))
    seq = q_ref.shape[0]
    nh, dh = pw_ref.shape[0], pw_ref.shape[-1]
    c = _RET_CHUNK
    n = seq // c
    head_cols = [slice(hd * dh, (hd + 1) * dh) for hd in range(nh)]

    def tn_dot(a, b):
        return lax.dot_general(a, b, (((0,), (0,)), ((), ())), preferred_element_type=_F32)

    def nt_dot(a, b):
        return lax.dot_general(a, b, (((1,), (1,)), ((), ())), preferred_element_type=_F32)

    ri = lax.broadcasted_iota(jnp.int32, (c, c), 0).astype(_F32)
    cj = lax.broadcasted_iota(jnp.int32, (c, c), 1).astype(_F32)
    diff = ri - cj
    qk_scale = dh ** -0.5
    pool_tile, chunk_f, chunk_b = [], [], []
    n_pool_tiles = len(_pool_tiles(seq))
    for hd, cols in enumerate(head_cols):
        pool_tile.append(_pool_chunks(u_ref, pad_ref, pooled_ref, cols, pl.program_id(1) * nh + hd, seq))
        head = pl.program_id(1) * nh + hd
        lg_f = -jnp.exp(jnp.full((1, c), df_ref[head], _F32))
        lg_b = -jnp.exp(jnp.full((1, c), db_ref[head], _F32))
        dec_ref[hd, 0] = jnp.exp(jnp.where(diff >= 0, diff * lg_f, -diff * lg_b)) * qk_scale
        dec_ref[hd, 1] = jnp.exp((ri + 1.0) * lg_f) * qk_scale
        dec_ref[hd, 2] = jnp.exp((c - 1.0 - ri) * lg_f)
        dec_ref[hd, 3] = jnp.exp((c - ri) * lg_b) * qk_scale
        dec_ref[hd, 4] = jnp.exp(ri * lg_b)
        chunk_f.append(jnp.exp(c * lg_f))
        chunk_b.append(jnp.exp(c * lg_b))

    for ci in range(n):
        rows = slice(ci * c, (ci + 1) * c)
        for hd, cols in enumerate(head_cols):
            q = q_ref[rows, cols]
            k = k_ref[rows, cols]
            k32 = k.astype(_F32)
            if ci > 0:
                kb_ref[rows, cols] = (k32 * dec_ref[hd, 4]).astype(_BF16)
            v = v_ref[rows, cols]
            scores = nt_dot(q, k) * dec_ref[hd, 0]
            out = _dot(scores.astype(_BF16), v)
            if ci > 0:
                out = out + _dot(q, sf_ref[hd].astype(_BF16)) * dec_ref[hd, 1]
            acc_ref[rows, cols] = out
            if ci < n - 1:
                upd = tn_dot((k32 * dec_ref[hd, 2]).astype(_BF16), v)
                sf_ref[hd] = upd if ci == 0 else sf_ref[hd] * chunk_f[hd] + upd
            for ti in range(ci, n_pool_tiles, n):
                pool_tile[hd](ti)

    for ci in reversed(range(n)):
        rows = slice(ci * c, (ci + 1) * c)
        for hd, cols in enumerate(head_cols):
            pool_ref[rows, cols] = (_dot(pooled_ref[rows, cols], pw_ref[hd]) * ps_ref[hd]).astype(pool_ref.dtype)
            out = acc_ref[rows, cols]
            if ci < n - 1:
                out = out + _dot(q_ref[rows, cols], sb_ref[hd].astype(_BF16)) * dec_ref[hd, 3]
            ret = _rms_rows(out)
            ret_ref[rows, cols] = (gate_ref[rows, cols].astype(_F32) * ret).astype(ret_ref.dtype)
            if ci > 0:
                upd = tn_dot(kb_ref[rows, cols], v_ref[rows, cols])
                sb_ref[hd] = upd if ci == n - 1 else sb_ref[hd] * chunk_b[hd] + upd


def _mixer(parts, pool_w, pool_scale, dec_f, dec_b, bsz, seq):
    groups, t, wide = parts[0].shape
    heads = _RET_HEADS
    dh = pool_w.shape[-1]
    nh = _MIXER_HEADS_PER_STEP
    assert heads == groups * nh and len(_POOL_WINDOWS) == heads and wide == nh * dh
    seq_block = pl.BlockSpec((1, seq, wide), lambda b, j: (j, b, 0))
    per_head = lambda shape: pl.BlockSpec((nh,) + shape, lambda b, j: (j, 0, 0))
    scratch = [
        pltpu.VMEM((_pool_pad_rows(seq), wide), _BF16),
        pltpu.VMEM((seq, wide), _BF16),
        pltpu.VMEM((seq, wide), _BF16),
        pltpu.VMEM((seq, wide), _F32),
        pltpu.VMEM((nh, 5, _RET_CHUNK, _RET_CHUNK), _F32),
        pltpu.VMEM((nh, dh, dh), _F32),
        pltpu.VMEM((nh, dh, dh), _F32),
    ]
    vmem = (2 * 7 * _nbytes((seq, wide), _BF16) + 4 * _nbytes((seq, wide), _F32)
            + 8 * nh * _nbytes((dh, dh), _F32))
    return pl.pallas_call(
        _mixer_kernel,
        grid=(bsz, heads // nh),
        in_specs=[
            seq_block, seq_block, seq_block, seq_block, seq_block,
            per_head((dh, dh)),
            per_head((1, dh)),
            pl.BlockSpec(memory_space=pltpu.SMEM),
            pl.BlockSpec(memory_space=pltpu.SMEM),
        ],
        out_specs=[seq_block, seq_block],
        out_shape=[jax.ShapeDtypeStruct((groups, t, wide), _BF16),
                   jax.ShapeDtypeStruct((groups, t, wide), _BF16)],
        scratch_shapes=scratch,
        compiler_params=pltpu.CompilerParams(
            dimension_semantics=("arbitrary", "arbitrary"), vmem_limit_bytes=_vmem_limit(vmem)),
        name="mixer",
    )(*parts, pool_w, pool_scale, dec_f, dec_b)


def _outproj_kernel(p_ref, r_ref, x_ref, mod_ref, gain_ref, w_ref, x1_ref, h2_ref):
    groups, _, wide = p_ref.shape
    gate = mod_ref[0, 2:3, :]
    scale = gain_ref[...] * (1.0 + mod_ref[0, 4:5, :])
    shift = mod_ref[0, 3:4, :]
    for r0 in range(0, x_ref.shape[0], _DOT_ROWS):
        rows = slice(r0, r0 + _DOT_ROWS)
        mixed = None
        for part, src_ref in enumerate((p_ref, r_ref)):
            for j in range(groups):
                k0 = (part * groups + j) * wide
                term = _dot(src_ref[j, rows, :], w_ref[k0:k0 + wide, :])
                mixed = term if mixed is None else mixed + term
        x1_ref[rows, :] = x_ref[rows, :] + gate * mixed
        _rmsnorm_rows(x1_ref, h2_ref, r0, _DOT_ROWS, scale, shift)


def _outproj(pool_out, ret_out, x2d, mod3, gain, w, seq, *, tm=512):
    t, d = x2d.shape
    groups, _, wide = pool_out.shape
    kp = kr = groups * wide
    assert ret_out.shape == pool_out.shape and w.shape[0] == kp + kr
    per_batch = seq // tm
    vmem = (2 * _nbytes((tm, kp + kr), _BF16) + 4 * _nbytes((tm, d), _F32) + 2 * _nbytes((tm, d), _BF16)
            + _nbytes((kp + kr, d), _BF16) + _nbytes((_DOT_ROWS, d), _F32))
    return pl.pallas_call(
        _outproj_kernel,
        grid=(t // tm,),
        in_specs=[
            pl.BlockSpec((groups, tm, wide), lambda i: (0, i, 0)),
            pl.BlockSpec((groups, tm, wide), lambda i: (0, i, 0)),
            pl.BlockSpec((tm, d), lambda i: (i, 0)),
            pl.BlockSpec((1, _N_MOD, d), lambda i: (i // per_batch, 0, 0)),
            pl.BlockSpec((1, d), lambda i: (0, 0)),
            pl.BlockSpec((kp + kr, d), lambda i: (0, 0), pipeline_mode=pl.Buffered(1)),
        ],
        out_specs=[pl.BlockSpec((tm, d), lambda i: (i, 0)),
                   pl.BlockSpec((tm, d), lambda i: (i, 0))],
        out_shape=[jax.ShapeDtypeStruct((t, d), _F32),
                   jax.ShapeDtypeStruct((t, d), _BF16)],
        compiler_params=pltpu.CompilerParams(
            dimension_semantics=("arbitrary",), vmem_limit_bytes=_vmem_limit(vmem)),
        name="outproj",
    )(pool_out, ret_out, x2d, mod3, gain, w)


def _ffn_kernel(h2_ref, wgu_ref, wd_ref, x1_hbm, mod_ref, gain_ref, o_ref, x1_buf, sem, *, final_norm):
    i, f = pl.program_id(0), pl.program_id(1)
    tm, d = o_ref.shape

    def x1_copy():
        return pltpu.make_async_copy(x1_hbm.at[pl.ds(pl.multiple_of(i * tm, tm), tm), :], x1_buf, sem.at[0])

    last = pl.num_programs(1) - 1

    def hidden():
        gu = _dot(h2_ref[...], wgu_ref[...])
        parts = [_silu(gu[:, c0:c0 + _LANES]) * gu[:, c0 + _LANES:c0 + 2 * _LANES]
                 for c0 in range(0, gu.shape[1], 2 * _LANES)]
        return jnp.concatenate(parts, axis=1).astype(_BF16)

    def finish_rows(r0):
        rows = slice(r0, r0 + _NORM_ROWS)
        gate = mod_ref[0, 5:6, :]
        gain = gain_ref[...]
        ss = None
        for cols in _col_blocks(d):
            x2 = x1_buf[rows, cols] + gate[:, cols] * o_ref[rows, cols]
            o_ref[rows, cols] = x2
            ss = x2 * x2 if ss is None else ss + x2 * x2
        if final_norm:
            rs = lax.rsqrt(jnp.sum(ss, axis=-1, keepdims=True) / d + _EPS)
            for cols in _col_blocks(d):
                o_ref[rows, cols] = o_ref[rows, cols] * rs * gain[:, cols]

    @pl.when(f == 0)
    def _():
        x1_copy().start()
        o_ref[...] = _dot(hidden(), wd_ref[...])

    @pl.when((f > 0) & (f < last))
    def _():
        o_ref[...] += _dot(hidden(), wd_ref[...])

    @pl.when(f == last)
    def _():
        x1_copy().wait()
        hid = hidden()
        for r0 in range(0, tm, _DOT_ROWS):
            rows = slice(r0, r0 + _DOT_ROWS)
            o_ref[rows, :] += _dot(hid[rows, :], wd_ref[...])
            for s0 in range(r0, r0 + _DOT_ROWS, _NORM_ROWS):
                finish_rows(s0)


def _ffn(h2, x1, mod3, wgu, wd, final_gain, seq, *, final_norm, tm=1024, tf=512):
    t, d = x1.shape
    dff = wd.shape[0]
    assert wgu.shape == (d, 2 * dff) and tf % _LANES == 0
    assert dff // tf >= 2
    per_batch = seq // tm
    vmem = (2 * _nbytes((tm, d), _BF16) + 6 * _nbytes((d, tf), _BF16) + 3 * _nbytes((tm, d), _F32)
            + 3 * _nbytes((tm, tf), _F32))
    return pl.pallas_call(
        functools.partial(_ffn_kernel, final_norm=final_norm),
        grid=(t // tm, dff // tf),
        in_specs=[
            pl.BlockSpec((tm, d), lambda i, f: (i, 0)),
            pl.BlockSpec((d, 2 * tf), lambda i, f: (0, f)),
            pl.BlockSpec((tf, d), lambda i, f: (f, 0)),
            pl.BlockSpec(memory_space=pl.ANY),
            pl.BlockSpec((1, _N_MOD, d), lambda i, f: (i // per_batch, 0, 0)),
            pl.BlockSpec((1, d), lambda i, f: (0, 0)),
        ],
        out_specs=pl.BlockSpec((tm, d), lambda i, f: (i, 0)),
        out_shape=jax.ShapeDtypeStruct((t, d), _F32),
        scratch_shapes=[pltpu.VMEM((tm, d), _F32), pltpu.SemaphoreType.DMA((1,))],
        compiler_params=pltpu.CompilerParams(
            dimension_semantics=("arbitrary", "arbitrary"), vmem_limit_bytes=_vmem_limit(vmem)),
        name="ffn",
    )(h2, wgu, wd, x1, mod3, final_gain)


def kernel(x, c, w_ada, b_ada, norm1_g, w_in, pool_w, pool_scale, ret_decay_fwd, ret_decay_bwd,
           w_out, norm2_g, w_gate, w_up, w_down, final_g):
    bsz, seq, d = x.shape
    depth = w_ada.shape[0]
    heads = _RET_HEADS
    pool_width = pool_scale.shape[1]
    dh = (w_in.shape[2] - pool_width) // (4 * heads)
    assert pool_width == len(_POOL_WINDOWS) * dh and seq % _RET_CHUNK == 0 and dh == _RET_CHUNK
    assert all(w & (w - 1) == 0 and w // 2 <= _POOL_EDGE_ROWS for w in _POOL_WINDOWS)

    inv = 1.0 / (_ROPE_BASE ** np.linspace(0.0, 1.0, dh // 2))
    ang = np.arange(seq)[:, None] * inv[None, :]
    cos, sin = jnp.asarray(np.cos(ang), _F32), jnp.asarray(np.sin(ang), _F32)

    xf = x.reshape(bsz * seq, d)
    for l in range(depth):
        mod3 = _adaln(c, w_ada[l], b_ada[l]).reshape(bsz, _N_MOD, d)
        *parts, wo, wd, wgu = _inproj(xf, mod3, norm1_g[l].reshape(1, d), cos, sin, w_in[l].astype(_BF16), seq,
                                      pool_width, [w_out[l], w_down[l]], [w_gate[l], w_up[l]])
        pool_out, ret_out = _mixer(
            parts, pool_w[l].astype(_BF16), pool_scale[l].reshape(len(_POOL_WINDOWS), 1, dh),
            ret_decay_fwd[l].astype(_F32), ret_decay_bwd[l].astype(_F32), bsz, seq)
        x1, h2 = _outproj(pool_out, ret_out, xf, mod3, norm2_g[l].reshape(1, d), wo, seq)
        xf = _ffn(h2, x1, mod3, wgu, wd, final_g.reshape(1, d), seq, final_norm=(l == depth - 1))
    return xf.reshape(bsz, seq, d)
```

```python
import functools

import jax
import jax.numpy as jnp
import numpy as np
from jax import lax
from jax.experimental import pallas as pl
from jax.experimental.pallas import tpu as pltpu

_EPS = 1e-6
_POOL_WINDOWS = (2, 4, 8, 16)
_RET_HEADS = 4
_ROPE_BASE = 10000.0
_N_MOD = 6

_V7X_VMEM_BYTES = 64 * 1024 * 1024
_VMEM_UNCOUNTED_BYTES = 8 * 1024 * 1024
_VMEM_RESERVED_BYTES = 2 * 1024 * 1024
_LANES = 128
_BF16_ROWS = 16
_RET_CHUNK = 256
_POOL_HALO = 16
_POOL_SRC_ROWS = 256
_POOL_EDGE_ROWS = 8
_NORM_ROWS = 128
_DOT_ROWS = 256
_MIXER_HEADS_PER_STEP = 2

_F32 = jnp.float32
_BF16 = jnp.bfloat16


def _vmem_limit(nbytes):
    return int(min(_V7X_VMEM_BYTES - _VMEM_RESERVED_BYTES, nbytes + _VMEM_UNCOUNTED_BYTES))


def _nbytes(shape, dtype):
    n = 1
    for s in shape:
        n *= s
    return n * jnp.dtype(dtype).itemsize


def _dot(a, b):
    return jnp.dot(a, b, preferred_element_type=_F32)


def _silu(x):
    return x * jax.nn.sigmoid(x)


def _rms_rows(x):
    return x * lax.rsqrt(jnp.mean(x * x, axis=-1, keepdims=True) + _EPS)


def _col_blocks(d):
    return [slice(c0, c0 + _LANES) for c0 in range(0, d, _LANES)]


def _rmsnorm_rows(src_ref, dst_ref, row0, nrows, gain, shift=None):
    d = src_ref.shape[1]
    for r0 in range(row0, row0 + nrows, _NORM_ROWS):
        rows = slice(r0, r0 + _NORM_ROWS)
        ss = None
        for cols in _col_blocks(d):
            xc = src_ref[rows, cols]
            ss = xc * xc if ss is None else ss + xc * xc
        rs = lax.rsqrt(jnp.sum(ss, axis=-1, keepdims=True) / d + _EPS)
        for cols in _col_blocks(d):
            y = src_ref[rows, cols] * rs * gain[:, cols]
            if shift is not None:
                y = y + shift[:, cols]
            dst_ref[rows, cols] = y.astype(dst_ref.dtype)


def _adaln_kernel(c_ref, w_ref, b_ref, o_ref):
    c = c_ref[...]
    o_ref[...] = _dot(_silu(c).astype(_BF16), w_ref[...].astype(_BF16)) + b_ref[...]


def _adaln(c, w, b, *, tn=1024):
    bsz, d = c.shape
    n = w.shape[1]
    vmem = 2 * _nbytes((d, tn), _F32) + _nbytes((d, tn), _BF16) + 4 * _nbytes((bsz, d), _F32)
    return pl.pallas_call(
        _adaln_kernel,
        grid=(n // tn,),
        in_specs=[
            pl.BlockSpec((bsz, d), lambda j: (0, 0)),
            pl.BlockSpec((d, tn), lambda j: (0, j)),
            pl.BlockSpec((1, tn), lambda j: (0, j)),
        ],
        out_specs=pl.BlockSpec((bsz, tn), lambda j: (0, j)),
        out_shape=jax.ShapeDtypeStruct((bsz, n), _F32),
        compiler_params=pltpu.CompilerParams(
            dimension_semantics=("arbitrary",), vmem_limit_bytes=_vmem_limit(vmem)),
        name="adaln",
    )(c, w, b.reshape(1, n))


def _slab_specs(weights, steps, index_map):
    specs, shapes, nbytes = [], [], 0
    for w in weights:
        rows, cols = w.shape
        assert rows % steps == 0 and (rows // steps) % _BF16_ROWS == 0
        specs.append(pl.BlockSpec((rows // steps, cols), index_map))
        shapes.append(jax.ShapeDtypeStruct(w.shape, _BF16))
        nbytes += 2 * _nbytes((rows // steps, cols), _F32) + 2 * _nbytes((rows // steps, cols), _BF16)
    return specs, shapes, nbytes


def _cast_slabs(src_refs, dst_refs):
    for src, dst in zip(src_refs, dst_refs):
        dst[...] = src[...].astype(dst.dtype)


def _cast_gate_up(wg_ref, wu_ref, wgu_ref):
    for g in range(wg_ref.shape[1] // _LANES):
        cols = slice(g * _LANES, (g + 1) * _LANES)
        wgu_ref[:, 2 * g * _LANES:(2 * g + 1) * _LANES] = wg_ref[:, cols].astype(wgu_ref.dtype)
        wgu_ref[:, (2 * g + 1) * _LANES:(2 * g + 2) * _LANES] = wu_ref[:, cols].astype(wgu_ref.dtype)


def _inproj_kernel(x_ref, mod_ref, gain_ref, cos_ref, sin_ref, w_ref, *rest, n_cast, pool_width, heads):
    cast_in, (wg_ref, wu_ref) = rest[:n_cast], rest[n_cast:n_cast + 2]
    u_ref, q_ref, k_ref, v_ref, g_ref, *cast_out, wgu_ref, h_ref = rest[n_cast + 2:]
    groups, _, wide = q_ref.shape
    width = groups * wide
    dh = width // heads
    half = dh // 2
    scale = gain_ref[...] * (1.0 + mod_ref[0, 1:2, :])
    shift = mod_ref[0, 0:1, :]

    def store(dst_ref, rows, val):
        for j in range(dst_ref.shape[0]):
            dst_ref[j, rows, :] = val[:, j * wide:(j + 1) * wide].astype(dst_ref.dtype)

    for r0 in range(0, x_ref.shape[0], _DOT_ROWS):
        rows = slice(r0, r0 + _DOT_ROWS)
        _rmsnorm_rows(x_ref, h_ref, r0, _DOT_ROWS, scale, shift)
        h = h_ref[rows, :]
        c0 = pool_width + 3 * width
        store(g_ref, rows, _silu(_dot(h, w_ref[:, c0:c0 + width])))
        cos, sin = cos_ref[rows, :], sin_ref[rows, :]
        for part, dst_ref in enumerate((q_ref, k_ref)):
            c0 = pool_width + part * width
            t = _dot(h, w_ref[:, c0:c0 + width])
            for hd in range(heads):
                t1 = t[:, hd * dh:hd * dh + half]
                t2 = t[:, hd * dh + half:(hd + 1) * dh]
                j, c1 = (hd * dh) // wide, (hd * dh) % wide
                dst_ref[j, rows, c1:c1 + half] = (t1 * cos - t2 * sin).astype(dst_ref.dtype)
                dst_ref[j, rows, c1 + half:c1 + dh] = (t1 * sin + t2 * cos).astype(dst_ref.dtype)
        if r0 == 0:
            _cast_slabs(cast_in, cast_out)
            _cast_gate_up(wg_ref, wu_ref, wgu_ref)
        c0 = pool_width + 2 * width
        store(v_ref, rows, _dot(h, w_ref[:, c0:c0 + width]))
        store(u_ref, rows, _dot(h, w_ref[:, 0:pool_width]))


def _inproj(x2d, mod3, gain, cos, sin, w, seq, pool_width, cast_weights, gate_up, *, tm=512):
    t, d = x2d.shape
    n = w.shape[1]
    per_batch = seq // tm
    steps = t // tm
    wide = _MIXER_HEADS_PER_STEP * (n - pool_width) // (4 * _RET_HEADS)
    groups = _RET_HEADS // _MIXER_HEADS_PER_STEP
    assert pool_width == groups * wide
    part_spec = pl.BlockSpec((groups, tm, wide), lambda i: (0, i, 0))
    part_shape = jax.ShapeDtypeStruct((groups, t, wide), _BF16)
    cast_specs, cast_shapes, cast_bytes = _slab_specs(cast_weights, steps, lambda i: (i, 0))
    gu_specs, _, gu_bytes = _slab_specs(gate_up, steps, lambda i: (i, 0))
    gu_rows, dff = gate_up[0].shape
    assert gate_up[1].shape == (gu_rows, dff) and dff % _LANES == 0
    vmem = (2 * _nbytes((tm, d), _F32) + _nbytes((d, n), _BF16) + 2 * _nbytes((tm, n), _BF16)
            + _nbytes((tm, d), _BF16) + _nbytes((_DOT_ROWS, n), _F32) + 4 * _nbytes(cos.shape, _F32) // per_batch
            + cast_bytes + gu_bytes)
    return pl.pallas_call(
        functools.partial(_inproj_kernel, n_cast=len(cast_weights), pool_width=pool_width, heads=_RET_HEADS),
        grid=(t // tm,),
        in_specs=[
            pl.BlockSpec((tm, d), lambda i: (i, 0)),
            pl.BlockSpec((1, _N_MOD, d), lambda i: (i // per_batch, 0, 0)),
            pl.BlockSpec((1, d), lambda i: (0, 0)),
            pl.BlockSpec((tm, cos.shape[1]), lambda i: (i % per_batch, 0)),
            pl.BlockSpec((tm, sin.shape[1]), lambda i: (i % per_batch, 0)),
            pl.BlockSpec((d, n), lambda i: (0, 0), pipeline_mode=pl.Buffered(1)),
        ] + cast_specs + gu_specs,
        out_specs=([part_spec] * 5 + cast_specs
                   + [pl.BlockSpec((gu_rows // steps, 2 * dff), lambda i: (i, 0))]),
        out_shape=([part_shape] * 5 + cast_shapes
                   + [jax.ShapeDtypeStruct((gu_rows, 2 * dff), _BF16)]),
        scratch_shapes=[pltpu.VMEM((tm, d), _BF16)],
        compiler_params=pltpu.CompilerParams(
            dimension_semantics=("arbitrary",), vmem_limit_bytes=_vmem_limit(vmem)),
        name="inproj",
    )(x2d, mod3, gain, cos, sin, w, *cast_weights, *gate_up)


def _pool_tiles(seq):
    step = _POOL_SRC_ROWS - 2 * _POOL_HALO
    return [(s, min(step, seq - s)) for s in range(0, seq, step)]


def _pool_pad_rows(seq):
    return _pool_tiles(seq)[-1][0] + _POOL_SRC_ROWS


def _pool_chunks(u_ref, pad_ref, pooled_ref, cols, group, seq):
    tiles = _pool_tiles(seq)
    halo, src_rows = _POOL_HALO, _POOL_SRC_ROWS
    out_rows = src_rows - 2 * halo
    edge = _POOL_EDGE_ROWS
    assert tiles[-1][1] >= edge
    half = jnp.int32(0)
    inv_w = jnp.float32(0)
    for gi, window in enumerate(_POOL_WINDOWS):
        half = jnp.where(group == gi, jnp.int32(window // 2), half)
        inv_w = jnp.where(group == gi, jnp.float32(1.0 / window), inv_w)

    width_c = cols.stop - cols.start
    pad_ref[0:halo, cols] = jnp.zeros((halo, width_c), pad_ref.dtype)
    pad_ref[halo + seq:, cols] = jnp.zeros((pad_ref.shape[0] - halo - seq, width_c), pad_ref.dtype)
    pad_ref[halo:halo + seq, cols] = u_ref[:, cols]

    rel = (lax.broadcasted_iota(jnp.int32, (out_rows, src_rows), 1) - halo
           - lax.broadcasted_iota(jnp.int32, (out_rows, src_rows), 0))
    band = (jnp.where((rel >= -half) & (rel < half), inv_w, 0.0)
            - jnp.where(rel == 0, 1.0, 0.0)).astype(_BF16)

    t_lo = lax.broadcasted_iota(jnp.int32, (edge, width_c), 0)
    t_hi = t_lo + (seq - edge)
    width = (2 * half).astype(_F32)
    ratio_lo = width / (t_lo + half - jnp.maximum(t_lo - half, 0)).astype(_F32)
    ratio_hi = width / (jnp.minimum(t_hi + half, seq) - (t_hi - half)).astype(_F32)

    def pool_tile(ti):
        s, rows = tiles[ti]
        pooled = _dot(band, pad_ref[s:s + src_rows, cols])
        if ti == 0:
            u_lo = u_ref[0:edge, cols].astype(_F32)
            pooled = jnp.concatenate([(pooled[0:edge] + u_lo) * ratio_lo - u_lo, pooled[edge:]], axis=0)
        if ti == len(tiles) - 1:
            lo = seq - edge - s
            u_hi = u_ref[seq - edge:seq, cols].astype(_F32)
            pooled = jnp.concatenate(
                [pooled[:lo], (pooled[lo:lo + edge] + u_hi) * ratio_hi - u_hi, pooled[lo + edge:]], axis=0)
        pooled_ref[s:s + rows, cols] = pooled[:rows].astype(pooled_ref.dtype)

    return pool_tile


def _mixer_kernel(u_ref, q_ref, k_ref, v_ref, gate_ref, pw_ref, ps_ref, df_ref, db_ref, pool_ref, ret_ref,
                  pad_ref, pooled_ref, kb_ref, acc_ref, dec_ref, sf_ref, sb_ref):
    u_ref, q_ref, k_ref, v_ref, gate_ref, pool_ref, ret_ref = (
        r.at[0] for r in (u_ref, q_ref, k_ref, v_ref, gate_ref, pool_ref, ret_ref))
    seq = q_ref.shape[0]
    nh, dh = pw_ref.shape[0], pw_ref.shape[-1]
    c = _RET_CHUNK
    n = seq // c
    head_cols = [slice(hd * dh, (hd + 1) * dh) for hd in range(nh)]

    def tn_dot(a, b):
        return lax.dot_general(a, b, (((0,), (0,)), ((), ())), preferred_element_type=_F32)

    def nt_dot(a, b):
        return lax.dot_general(a, b, (((1,), (1,)), ((), ())), preferred_element_type=_F32)

    ri = lax.broadcasted_iota(jnp.int32, (c, c), 0).astype(_F32)
    cj = lax.broadcasted_iota(jnp.int32, (c, c), 1).astype(_F32)
    diff = ri - cj
    qk_scale = dh ** -0.5
    pool_tile, chunk_f, chunk_b = [], [], []
    n_pool_tiles = len(_pool_tiles(seq))
    for hd, cols in enumerate(head_cols):
        pool_tile.append(_pool_chunks(u_ref, pad_ref, pooled_ref, cols, pl.program_id(0) * nh + hd, seq))
        head = pl.program_id(0) * nh + hd
        lg_f = -jnp.exp(jnp.full((1, c), df_ref[head], _F32))
        lg_b = -jnp.exp(jnp.full((1, c), db_ref[head], _F32))
        chunk_f.append(jnp.exp(c * lg_f))
        chunk_b.append(jnp.exp(c * lg_b))

        @pl.when(pl.program_id(1) == 0)
        def _(hd=hd, lg_f=lg_f, lg_b=lg_b):
            dec_ref[hd, 0] = jnp.exp(jnp.where(diff >= 0, diff * lg_f, -diff * lg_b)) * qk_scale
            dec_ref[hd, 1] = jnp.exp((ri + 1.0) * lg_f) * qk_scale
            dec_ref[hd, 2] = jnp.exp((c - 1.0 - ri) * lg_f)
            dec_ref[hd, 3] = jnp.exp((c - ri) * lg_b) * qk_scale
            dec_ref[hd, 4] = jnp.exp(ri * lg_b)

    for ci in range(n):
        rows = slice(ci * c, (ci + 1) * c)
        for hd, cols in enumerate(head_cols):
            q = q_ref[rows, cols]
            k = k_ref[rows, cols]
            k32 = k.astype(_F32)
            if ci > 0:
                kb_ref[rows, cols] = (k32 * dec_ref[hd, 4]).astype(_BF16)
            v = v_ref[rows, cols]
            scores = nt_dot(q, k) * dec_ref[hd, 0]
            out = _dot(scores.astype(_BF16), v)
            if ci > 0:
                out = out + _dot(q, sf_ref[hd].astype(_BF16)) * dec_ref[hd, 1]
            acc_ref[rows, cols] = out
            if ci < n - 1:
                upd = tn_dot((k32 * dec_ref[hd, 2]).astype(_BF16), v)
                sf_ref[hd] = upd if ci == 0 else sf_ref[hd] * chunk_f[hd] + upd
            for ti in range(ci, n_pool_tiles, n):
                pool_tile[hd](ti)

    for ci in reversed(range(n)):
        rows = slice(ci * c, (ci + 1) * c)
        for hd, cols in enumerate(head_cols):
            pool_ref[rows, cols] = (_dot(pooled_ref[rows, cols], pw_ref[hd]) * ps_ref[hd]).astype(pool_ref.dtype)
            out = acc_ref[rows, cols]
            if ci < n - 1:
                out = out + _dot(q_ref[rows, cols], sb_ref[hd].astype(_BF16)) * dec_ref[hd, 3]
            ret = _rms_rows(out)
            ret_ref[rows, cols] = (gate_ref[rows, cols].astype(_F32) * ret).astype(ret_ref.dtype)
            if ci > 0:
                upd = tn_dot(kb_ref[rows, cols], v_ref[rows, cols])
                sb_ref[hd] = upd if ci == n - 1 else sb_ref[hd] * chunk_b[hd] + upd


def _mixer(parts, pool_w, pool_scale, dec_f, dec_b, bsz, seq):
    groups, t, wide = parts[0].shape
    heads = _RET_HEADS
    dh = pool_w.shape[-1]
    nh = _MIXER_HEADS_PER_STEP
    assert heads == groups * nh and len(_POOL_WINDOWS) == heads and wide == nh * dh
    seq_block = pl.BlockSpec((1, seq, wide), lambda j, b: (j, b, 0))
    per_head = lambda shape: pl.BlockSpec((nh,) + shape, lambda j, b: (j, 0, 0))
    scratch = [
        pltpu.VMEM((_pool_pad_rows(seq), wide), _BF16),
        pltpu.VMEM((seq, wide), _BF16),
        pltpu.VMEM((seq, wide), _BF16),
        pltpu.VMEM((seq, wide), _F32),
        pltpu.VMEM((nh, 5, _RET_CHUNK, _RET_CHUNK), _F32),
        pltpu.VMEM((nh, dh, dh), _F32),
        pltpu.VMEM((nh, dh, dh), _F32),
    ]
    vmem = (2 * 7 * _nbytes((seq, wide), _BF16) + 4 * _nbytes((seq, wide), _F32)
            + 8 * nh * _nbytes((dh, dh), _F32))
    return pl.pallas_call(
        _mixer_kernel,
        grid=(heads // nh, bsz),
        in_specs=[
            seq_block, seq_block, seq_block, seq_block, seq_block,
            per_head((dh, dh)),
            per_head((1, dh)),
            pl.BlockSpec(memory_space=pltpu.SMEM),
            pl.BlockSpec(memory_space=pltpu.SMEM),
        ],
        out_specs=[seq_block, seq_block],
        out_shape=[jax.ShapeDtypeStruct((groups, t, wide), _BF16),
                   jax.ShapeDtypeStruct((groups, t, wide), _BF16)],
        scratch_shapes=scratch,
        compiler_params=pltpu.CompilerParams(
            dimension_semantics=("arbitrary", "arbitrary"), vmem_limit_bytes=_vmem_limit(vmem)),
        name="mixer",
    )(*parts, pool_w, pool_scale, dec_f, dec_b)


def _outproj_kernel(p_ref, r_ref, x_ref, mod_ref, gain_ref, w_ref, x1_ref, h2_ref):
    groups, _, wide = p_ref.shape
    gate = mod_ref[0, 2:3, :]
    scale = gain_ref[...] * (1.0 + mod_ref[0, 4:5, :])
    shift = mod_ref[0, 3:4, :]
    for r0 in range(0, x_ref.shape[0], _DOT_ROWS):
        rows = slice(r0, r0 + _DOT_ROWS)
        mixed = None
        for part, src_ref in enumerate((p_ref, r_ref)):
            for j in range(groups):
                k0 = (part * groups + j) * wide
                term = _dot(src_ref[j, rows, :], w_ref[k0:k0 + wide, :])
                mixed = term if mixed is None else mixed + term
        x1_ref[rows, :] = x_ref[rows, :] + gate * mixed
        _rmsnorm_rows(x1_ref, h2_ref, r0, _DOT_ROWS, scale, shift)


def _outproj(pool_out, ret_out, x2d, mod3, gain, w, seq, *, tm=512):
    t, d = x2d.shape
    groups, _, wide = pool_out.shape
    kp = kr = groups * wide
    assert ret_out.shape == pool_out.shape and w.shape[0] == kp + kr
    per_batch = seq // tm
    vmem = (2 * _nbytes((tm, kp + kr), _BF16) + 4 * _nbytes((tm, d), _F32) + 2 * _nbytes((tm, d), _BF16)
            + _nbytes((kp + kr, d), _BF16) + _nbytes((_DOT_ROWS, d), _F32))
    return pl.pallas_call(
        _outproj_kernel,
        grid=(t // tm,),
        in_specs=[
            pl.BlockSpec((groups, tm, wide), lambda i: (0, i, 0)),
            pl.BlockSpec((groups, tm, wide), lambda i: (0, i, 0)),
            pl.BlockSpec((tm, d), lambda i: (i, 0)),
            pl.BlockSpec((1, _N_MOD, d), lambda i: (i // per_batch, 0, 0)),
            pl.BlockSpec((1, d), lambda i: (0, 0)),
            pl.BlockSpec((kp + kr, d), lambda i: (0, 0), pipeline_mode=pl.Buffered(1)),
        ],
        out_specs=[pl.BlockSpec((tm, d), lambda i: (i, 0)),
                   pl.BlockSpec((tm, d), lambda i: (i, 0))],
        out_shape=[jax.ShapeDtypeStruct((t, d), _F32),
                   jax.ShapeDtypeStruct((t, d), _BF16)],
        compiler_params=pltpu.CompilerParams(
            dimension_semantics=("arbitrary",), vmem_limit_bytes=_vmem_limit(vmem)),
        name="outproj",
    )(pool_out, ret_out, x2d, mod3, gain, w)


def _ffn_kernel(h2_ref, wgu_ref, wd_ref, x1_hbm, mod_ref, gain_ref, o_ref, x1_buf, hid_ref, sem, *, final_norm):
    i, f = pl.program_id(0), pl.program_id(1)
    tm, d = o_ref.shape

    def x1_copy():
        return pltpu.make_async_copy(x1_hbm.at[pl.ds(pl.multiple_of(i * tm, tm), tm), :], x1_buf, sem.at[0])

    last = pl.num_programs(1) - 1

    def hidden():
        gu = _dot(h2_ref[...], wgu_ref[...])
        for g, c0 in enumerate(range(0, gu.shape[1], 2 * _LANES)):
            hid_ref[:, g * _LANES:(g + 1) * _LANES] = (
                _silu(gu[:, c0:c0 + _LANES]) * gu[:, c0 + _LANES:c0 + 2 * _LANES]).astype(hid_ref.dtype)
        return hid_ref[...]

    def finish_rows(r0):
        rows = slice(r0, r0 + _NORM_ROWS)
        gate = mod_ref[0, 5:6, :]
        gain = gain_ref[...]
        ss = None
        for cols in _col_blocks(d):
            x2 = x1_buf[rows, cols] + gate[:, cols] * o_ref[rows, cols]
            o_ref[rows, cols] = x2
            ss = x2 * x2 if ss is None else ss + x2 * x2
        if final_norm:
            rs = lax.rsqrt(jnp.sum(ss, axis=-1, keepdims=True) / d + _EPS)
            for cols in _col_blocks(d):
                o_ref[rows, cols] = o_ref[rows, cols] * rs * gain[:, cols]

    @pl.when(f == 0)
    def _():
        x1_copy().start()
        o_ref[...] = _dot(hidden(), wd_ref[...])

    @pl.when((f > 0) & (f < last))
    def _():
        o_ref[...] += _dot(hidden(), wd_ref[...])

    @pl.when(f == last)
    def _():
        x1_copy().wait()
        hid = hidden()
        for r0 in range(0, tm, _DOT_ROWS):
            rows = slice(r0, r0 + _DOT_ROWS)
            o_ref[rows, :] += _dot(hid[rows, :], wd_ref[...])
            for s0 in range(r0, r0 + _DOT_ROWS, _NORM_ROWS):
                finish_rows(s0)


def _ffn(h2, x1, mod3, wgu, wd, final_gain, seq, *, final_norm, tm=1024, tf=512):
    t, d = x1.shape
    dff = wd.shape[0]
    assert wgu.shape == (d, 2 * dff) and tf % _LANES == 0
    assert dff // tf >= 2
    per_batch = seq // tm
    vmem = (2 * _nbytes((tm, d), _BF16) + 6 * _nbytes((d, tf), _BF16) + 3 * _nbytes((tm, d), _F32)
            + 3 * _nbytes((tm, tf), _F32))
    return pl.pallas_call(
        functools.partial(_ffn_kernel, final_norm=final_norm),
        grid=(t // tm, dff // tf),
        in_specs=[
            pl.BlockSpec((tm, d), lambda i, f: (i, 0)),
            pl.BlockSpec((d, 2 * tf), lambda i, f: (0, f)),
            pl.BlockSpec((tf, d), lambda i, f: (f, 0)),
            pl.BlockSpec(memory_space=pl.ANY),
            pl.BlockSpec((1, _N_MOD, d), lambda i, f: (i // per_batch, 0, 0)),
            pl.BlockSpec((1, d), lambda i, f: (0, 0)),
        ],
        out_specs=pl.BlockSpec((tm, d), lambda i, f: (i, 0)),
        out_shape=jax.ShapeDtypeStruct((t, d), _F32),
        scratch_shapes=[pltpu.VMEM((tm, d), _F32), pltpu.VMEM((tm, tf), _BF16), pltpu.SemaphoreType.DMA((1,))],
        compiler_params=pltpu.CompilerParams(
            dimension_semantics=("arbitrary", "arbitrary"), vmem_limit_bytes=_vmem_limit(vmem)),
        name="ffn",
    )(h2, wgu, wd, x1, mod3, final_gain)


def kernel(x, c, w_ada, b_ada, norm1_g, w_in, pool_w, pool_scale, ret_decay_fwd, ret_decay_bwd,
           w_out, norm2_g, w_gate, w_up, w_down, final_g):
    bsz, seq, d = x.shape
    depth = w_ada.shape[0]
    heads = _RET_HEADS
    pool_width = pool_scale.shape[1]
    dh = (w_in.shape[2] - pool_width) // (4 * heads)
    assert pool_width == len(_POOL_WINDOWS) * dh and seq % _RET_CHUNK == 0 and dh == _RET_CHUNK
    assert all(w & (w - 1) == 0 and w // 2 <= _POOL_EDGE_ROWS for w in _POOL_WINDOWS)

    inv = 1.0 / (_ROPE_BASE ** np.linspace(0.0, 1.0, dh // 2))
    ang = np.arange(seq)[:, None] * inv[None, :]
    cos, sin = jnp.asarray(np.cos(ang), _F32), jnp.asarray(np.sin(ang), _F32)

    xf = x.reshape(bsz * seq, d)
    for l in range(depth):
        mod3 = _adaln(c, w_ada[l], b_ada[l]).reshape(bsz, _N_MOD, d)
        *parts, wo, wd, wgu = _inproj(xf, mod3, norm1_g[l].reshape(1, d), cos, sin, w_in[l].astype(_BF16), seq,
                                      pool_width, [w_out[l], w_down[l]], [w_gate[l], w_up[l]])
        pool_out, ret_out = _mixer(
            parts, pool_w[l].astype(_BF16), pool_scale[l].reshape(len(_POOL_WINDOWS), 1, dh),
            ret_decay_fwd[l].astype(_F32), ret_decay_bwd[l].astype(_F32), bsz, seq)
        x1, h2 = _outproj(pool_out, ret_out, xf, mod3, norm2_g[l].reshape(1, d), wo, seq)
        xf = _ffn(h2, x1, mod3, wgu, wd, final_g.reshape(1, d), seq, final_norm=(l == depth - 1))
    return xf.reshape(bsz, seq, d)
```

```python
import functools

import jax
import jax.numpy as jnp
import numpy as np
from jax import lax
from jax.experimental import pallas as pl
from jax.experimental.pallas import tpu as pltpu

_EPS = 1e-6
_POOL_WINDOWS = (2, 4, 8, 16)
_RET_HEADS = 4
_ROPE_BASE = 10000.0
_N_MOD = 6

_V7X_VMEM_BYTES = 64 * 1024 * 1024
_VMEM_UNCOUNTED_BYTES = 8 * 1024 * 1024
_VMEM_RESERVED_BYTES = 2 * 1024 * 1024
_LANES = 128
_BF16_ROWS = 16
_RET_CHUNK = 256
_POOL_HALO = 16
_POOL_SRC_ROWS = 256
_POOL_EDGE_ROWS = 8
_NORM_ROWS = 128
_DOT_ROWS = 256
_MIXER_HEADS_PER_STEP = 2

_F32 = jnp.float32
_BF16 = jnp.bfloat16


def _vmem_limit(nbytes):
    return int(min(_V7X_VMEM_BYTES - _VMEM_RESERVED_BYTES, nbytes + _VMEM_UNCOUNTED_BYTES))


def _nbytes(shape, dtype):
    n = 1
    for s in shape:
        n *= s
    return n * jnp.dtype(dtype).itemsize


def _dot(a, b):
    return jnp.dot(a, b, preferred_element_type=_F32)


def _silu(x):
    return x * jax.nn.sigmoid(x)


def _rms_rows(x):
    return x * lax.rsqrt(jnp.mean(x * x, axis=-1, keepdims=True) + _EPS)


def _col_blocks(d):
    return [slice(c0, c0 + _LANES) for c0 in range(0, d, _LANES)]


def _rmsnorm_rows(src_ref, dst_ref, row0, nrows, gain, shift=None):
    d = src_ref.shape[1]
    for r0 in range(row0, row0 + nrows, _NORM_ROWS):
        rows = slice(r0, r0 + _NORM_ROWS)
        ss = None
        for cols in _col_blocks(d):
            xc = src_ref[rows, cols]
            ss = xc * xc if ss is None else ss + xc * xc
        rs = lax.rsqrt(jnp.sum(ss, axis=-1, keepdims=True) / d + _EPS)
        for cols in _col_blocks(d):
            y = src_ref[rows, cols] * rs * gain[:, cols]
            if shift is not None:
                y = y + shift[:, cols]
            dst_ref[rows, cols] = y.astype(dst_ref.dtype)


def _adaln_kernel(c_ref, w_ref, b_ref, o_ref):
    c = c_ref[...]
    o_ref[...] = _dot(_silu(c).astype(_BF16), w_ref[...].astype(_BF16)) + b_ref[...]


def _adaln(c, w, b, *, tn=1024):
    bsz, d = c.shape
    n = w.shape[1]
    vmem = 2 * _nbytes((d, tn), _F32) + _nbytes((d, tn), _BF16) + 4 * _nbytes((bsz, d), _F32)
    return pl.pallas_call(
        _adaln_kernel,
        grid=(n // tn,),
        in_specs=[
            pl.BlockSpec((bsz, d), lambda j: (0, 0)),
            pl.BlockSpec((d, tn), lambda j: (0, j)),
            pl.BlockSpec((1, tn), lambda j: (0, j)),
        ],
        out_specs=pl.BlockSpec((bsz, tn), lambda j: (0, j)),
        out_shape=jax.ShapeDtypeStruct((bsz, n), _F32),
        compiler_params=pltpu.CompilerParams(
            dimension_semantics=("arbitrary",), vmem_limit_bytes=_vmem_limit(vmem)),
        name="adaln",
    )(c, w, b.reshape(1, n))


def _slab_specs(weights, steps, index_map):
    specs, shapes, nbytes = [], [], 0
    for w in weights:
        rows, cols = w.shape
        assert rows % steps == 0 and (rows // steps) % _BF16_ROWS == 0
        specs.append(pl.BlockSpec((rows // steps, cols), index_map))
        shapes.append(jax.ShapeDtypeStruct(w.shape, _BF16))
        nbytes += 2 * _nbytes((rows // steps, cols), _F32) + 2 * _nbytes((rows // steps, cols), _BF16)
    return specs, shapes, nbytes


def _cast_slabs(src_refs, dst_refs):
    for src, dst in zip(src_refs, dst_refs):
        dst[...] = src[...].astype(dst.dtype)


def _cast_gate_up(wg_ref, wu_ref, wgu_ref):
    for g in range(wg_ref.shape[1] // _LANES):
        cols = slice(g * _LANES, (g + 1) * _LANES)
        wgu_ref[:, 2 * g * _LANES:(2 * g + 1) * _LANES] = wg_ref[:, cols].astype(wgu_ref.dtype)
        wgu_ref[:, (2 * g + 1) * _LANES:(2 * g + 2) * _LANES] = wu_ref[:, cols].astype(wgu_ref.dtype)


def _inproj_kernel(x_ref, mod_ref, gain_ref, cos_ref, sin_ref, w_ref, *rest, n_cast, pool_width, heads):
    cast_in, (wg_ref, wu_ref) = rest[:n_cast], rest[n_cast:n_cast + 2]
    u_ref, q_ref, k_ref, v_ref, g_ref, *cast_out, wgu_ref, h_ref = rest[n_cast + 2:]
    groups, _, wide = q_ref.shape
    width = groups * wide
    dh = width // heads
    half = dh // 2
    scale = gain_ref[...] * (1.0 + mod_ref[0, 1:2, :])
    shift = mod_ref[0, 0:1, :]

    def store(dst_ref, rows, val):
        for j in range(dst_ref.shape[0]):
            dst_ref[j, rows, :] = val[:, j * wide:(j + 1) * wide].astype(dst_ref.dtype)

    for r0 in range(0, x_ref.shape[0], _DOT_ROWS):
        rows = slice(r0, r0 + _DOT_ROWS)
        _rmsnorm_rows(x_ref, h_ref, r0, _DOT_ROWS, scale, shift)
        h = h_ref[rows, :]
        c0 = pool_width + 3 * width
        store(g_ref, rows, _silu(_dot(h, w_ref[:, c0:c0 + width])))
        cos, sin = cos_ref[rows, :], sin_ref[rows, :]
        for part in range(2):
            c0 = pool_width + part * width
            t = _dot(h, w_ref[:, c0:c0 + width])
            for hd in range(heads):
                t1 = t[:, hd * dh:hd * dh + half]
                t2 = t[:, hd * dh + half:(hd + 1) * dh]
                r1, r2 = t1 * cos - t2 * sin, t1 * sin + t2 * cos
                if part == 0:
                    j, c1 = (hd * dh) // wide, (hd * dh) % wide
                    q_ref[j, rows, c1:c1 + half] = r1.astype(q_ref.dtype)
                    q_ref[j, rows, c1 + half:c1 + dh] = r2.astype(q_ref.dtype)
                else:
                    k_ref[hd, 0:half, rows] = r1.T.astype(k_ref.dtype)
                    k_ref[hd, half:dh, rows] = r2.T.astype(k_ref.dtype)
        if r0 == 0:
            _cast_slabs(cast_in, cast_out)
            _cast_gate_up(wg_ref, wu_ref, wgu_ref)
        c0 = pool_width + 2 * width
        store(v_ref, rows, _dot(h, w_ref[:, c0:c0 + width]))
        store(u_ref, rows, _dot(h, w_ref[:, 0:pool_width]))


def _inproj(x2d, mod3, gain, cos, sin, w, seq, pool_width, cast_weights, gate_up, *, tm=512):
    t, d = x2d.shape
    n = w.shape[1]
    per_batch = seq // tm
    steps = t // tm
    wide = _MIXER_HEADS_PER_STEP * (n - pool_width) // (4 * _RET_HEADS)
    groups = _RET_HEADS // _MIXER_HEADS_PER_STEP
    assert pool_width == groups * wide
    part_spec = pl.BlockSpec((groups, tm, wide), lambda i: (0, i, 0))
    part_shape = jax.ShapeDtypeStruct((groups, t, wide), _BF16)
    cast_specs, cast_shapes, cast_bytes = _slab_specs(cast_weights, steps, lambda i: (i, 0))
    gu_specs, _, gu_bytes = _slab_specs(gate_up, steps, lambda i: (i, 0))
    gu_rows, dff = gate_up[0].shape
    assert gate_up[1].shape == (gu_rows, dff) and dff % _LANES == 0
    vmem = (2 * _nbytes((tm, d), _F32) + _nbytes((d, n), _BF16) + 2 * _nbytes((tm, n), _BF16)
            + _nbytes((tm, d), _BF16) + _nbytes((_DOT_ROWS, n), _F32) + 4 * _nbytes(cos.shape, _F32) // per_batch
            + cast_bytes + gu_bytes)
    return pl.pallas_call(
        functools.partial(_inproj_kernel, n_cast=len(cast_weights), pool_width=pool_width, heads=_RET_HEADS),
        grid=(t // tm,),
        in_specs=[
            pl.BlockSpec((tm, d), lambda i: (i, 0)),
            pl.BlockSpec((1, _N_MOD, d), lambda i: (i // per_batch, 0, 0)),
            pl.BlockSpec((1, d), lambda i: (0, 0)),
            pl.BlockSpec((tm, cos.shape[1]), lambda i: (i % per_batch, 0)),
            pl.BlockSpec((tm, sin.shape[1]), lambda i: (i % per_batch, 0)),
            pl.BlockSpec((d, n), lambda i: (0, 0), pipeline_mode=pl.Buffered(1)),
        ] + cast_specs + gu_specs,
        out_specs=([part_spec, part_spec, pl.BlockSpec((_RET_HEADS, wide // _MIXER_HEADS_PER_STEP, tm),
                                                       lambda i: (0, 0, i)), part_spec, part_spec] + cast_specs
                   + [pl.BlockSpec((gu_rows // steps, 2 * dff), lambda i: (i, 0))]),
        out_shape=([part_shape, part_shape,
                    jax.ShapeDtypeStruct((_RET_HEADS, wide // _MIXER_HEADS_PER_STEP, t), _BF16),
                    part_shape, part_shape] + cast_shapes
                   + [jax.ShapeDtypeStruct((gu_rows, 2 * dff), _BF16)]),
        scratch_shapes=[pltpu.VMEM((tm, d), _BF16)],
        compiler_params=pltpu.CompilerParams(
            dimension_semantics=("arbitrary",), vmem_limit_bytes=_vmem_limit(vmem)),
        name="inproj",
    )(x2d, mod3, gain, cos, sin, w, *cast_weights, *gate_up)


def _pool_tiles(seq):
    step = _POOL_SRC_ROWS - 2 * _POOL_HALO
    return [(s, min(step, seq - s)) for s in range(0, seq, step)]


def _pool_pad_rows(seq):
    return _pool_tiles(seq)[-1][0] + _POOL_SRC_ROWS


def _pool_chunks(u_ref, pad_ref, pooled_ref, cols, group, seq):
    tiles = _pool_tiles(seq)
    halo, src_rows = _POOL_HALO, _POOL_SRC_ROWS
    out_rows = src_rows - 2 * halo
    edge = _POOL_EDGE_ROWS
    assert tiles[-1][1] >= edge
    half = jnp.int32(0)
    inv_w = jnp.float32(0)
    for gi, window in enumerate(_POOL_WINDOWS):
        half = jnp.where(group == gi, jnp.int32(window // 2), half)
        inv_w = jnp.where(group == gi, jnp.float32(1.0 / window), inv_w)

    width_c = cols.stop - cols.start
    pad_ref[0:halo, cols] = jnp.zeros((halo, width_c), pad_ref.dtype)
    pad_ref[halo + seq:, cols] = jnp.zeros((pad_ref.shape[0] - halo - seq, width_c), pad_ref.dtype)
    pad_ref[halo:halo + seq, cols] = u_ref[:, cols]

    rel = (lax.broadcasted_iota(jnp.int32, (out_rows, src_rows), 1) - halo
           - lax.broadcasted_iota(jnp.int32, (out_rows, src_rows), 0))
    band = (jnp.where((rel >= -half) & (rel < half), inv_w, 0.0)
            - jnp.where(rel == 0, 1.0, 0.0)).astype(_BF16)

    t_lo = lax.broadcasted_iota(jnp.int32, (edge, width_c), 0)
    t_hi = t_lo + (seq - edge)
    width = (2 * half).astype(_F32)
    ratio_lo = width / (t_lo + half - jnp.maximum(t_lo - half, 0)).astype(_F32)
    ratio_hi = width / (jnp.minimum(t_hi + half, seq) - (t_hi - half)).astype(_F32)

    def pool_tile(ti):
        s, rows = tiles[ti]
        pooled = _dot(band, pad_ref[s:s + src_rows, cols])
        if ti == 0:
            u_lo = u_ref[0:edge, cols].astype(_F32)
            pooled = jnp.concatenate([(pooled[0:edge] + u_lo) * ratio_lo - u_lo, pooled[edge:]], axis=0)
        if ti == len(tiles) - 1:
            lo = seq - edge - s
            u_hi = u_ref[seq - edge:seq, cols].astype(_F32)
            pooled = jnp.concatenate(
                [pooled[:lo], (pooled[lo:lo + edge] + u_hi) * ratio_hi - u_hi, pooled[lo + edge:]], axis=0)
        pooled_ref[s:s + rows, cols] = pooled[:rows].astype(pooled_ref.dtype)

    return pool_tile


def _mixer_kernel(u_ref, q_ref, kt_ref, v_ref, gate_ref, pw_ref, ps_ref, df_ref, db_ref, pool_ref, ret_ref,
                  pad_ref, pooled_ref, kb_ref, acc_ref, dec_ref, sf_ref, sb_ref):
    u_ref, q_ref, v_ref, gate_ref, pool_ref, ret_ref = (
        r.at[0] for r in (u_ref, q_ref, v_ref, gate_ref, pool_ref, ret_ref))
    seq = q_ref.shape[0]
    nh, dh = pw_ref.shape[0], pw_ref.shape[-1]
    c = _RET_CHUNK
    n = seq // c
    head_cols = [slice(hd * dh, (hd + 1) * dh) for hd in range(nh)]

    ri = lax.broadcasted_iota(jnp.int32, (c, c), 0).astype(_F32)
    cj = lax.broadcasted_iota(jnp.int32, (c, c), 1).astype(_F32)
    pos = lax.broadcasted_iota(jnp.int32, (1, c), 1).astype(_F32)
    diff = ri - cj
    qk_scale = dh ** -0.5
    pool_tile, chunk_f, chunk_b, key_f, key_b = [], [], [], [], []
    n_pool_tiles = len(_pool_tiles(seq))
    for hd, cols in enumerate(head_cols):
        pool_tile.append(_pool_chunks(u_ref, pad_ref, pooled_ref, cols, pl.program_id(1) * nh + hd, seq))
        head = pl.program_id(1) * nh + hd
        lg_f = -jnp.exp(jnp.full((1, c), df_ref[head], _F32))
        lg_b = -jnp.exp(jnp.full((1, c), db_ref[head], _F32))
        dec_ref[hd, 0] = jnp.exp(jnp.where(diff >= 0, diff * lg_f, -diff * lg_b)) * qk_scale
        dec_ref[hd, 1] = jnp.exp((ri + 1.0) * lg_f) * qk_scale
        dec_ref[hd, 2] = jnp.exp((c - ri) * lg_b) * qk_scale
        key_f.append(jnp.exp((c - 1.0 - pos) * lg_f))
        key_b.append(jnp.exp(pos * lg_b))
        chunk_f.append(jnp.exp(c * lg_f))
        chunk_b.append(jnp.exp(c * lg_b))

    for ci in range(n):
        rows = slice(ci * c, (ci + 1) * c)
        for hd, cols in enumerate(head_cols):
            q = q_ref[rows, cols]
            kt = kt_ref[hd, :, rows]
            kt32 = kt.astype(_F32)
            if ci > 0:
                kb_ref[hd, :, rows] = (kt32 * key_b[hd]).astype(_BF16)
            v = v_ref[rows, cols]
            scores = _dot(q, kt) * dec_ref[hd, 0]
            out = _dot(scores.astype(_BF16), v)
            if ci > 0:
                out = out + _dot(q, sf_ref[hd].astype(_BF16)) * dec_ref[hd, 1]
            acc_ref[rows, cols] = out
            if ci < n - 1:
                upd = _dot((kt32 * key_f[hd]).astype(_BF16), v)
                sf_ref[hd] = upd if ci == 0 else sf_ref[hd] * chunk_f[hd] + upd
            for ti in range(ci, n_pool_tiles, n):
                pool_tile[hd](ti)

    for ci in reversed(range(n)):
        rows = slice(ci * c, (ci + 1) * c)
        for hd, cols in enumerate(head_cols):
            pool_ref[rows, cols] = (_dot(pooled_ref[rows, cols], pw_ref[hd]) * ps_ref[hd]).astype(pool_ref.dtype)
            out = acc_ref[rows, cols]
            if ci < n - 1:
                out = out + _dot(q_ref[rows, cols], sb_ref[hd].astype(_BF16)) * dec_ref[hd, 2]
            ret = _rms_rows(out)
            ret_ref[rows, cols] = (gate_ref[rows, cols].astype(_F32) * ret).astype(ret_ref.dtype)
            if ci > 0:
                upd = _dot(kb_ref[hd, :, rows], v_ref[rows, cols])
                sb_ref[hd] = upd if ci == n - 1 else sb_ref[hd] * chunk_b[hd] + upd


def _mixer(parts, pool_w, pool_scale, dec_f, dec_b, bsz, seq):
    groups, t, wide = parts[0].shape
    heads = _RET_HEADS
    dh = pool_w.shape[-1]
    nh = _MIXER_HEADS_PER_STEP
    assert heads == groups * nh and len(_POOL_WINDOWS) == heads and wide == nh * dh
    seq_block = pl.BlockSpec((1, seq, wide), lambda b, j: (j, b, 0))
    per_head = lambda shape: pl.BlockSpec((nh,) + shape, lambda b, j: (j, 0, 0))
    scratch = [
        pltpu.VMEM((_pool_pad_rows(seq), wide), _BF16),
        pltpu.VMEM((seq, wide), _BF16),
        pltpu.VMEM((nh, dh, seq), _BF16),
        pltpu.VMEM((seq, wide), _F32),
        pltpu.VMEM((nh, 3, _RET_CHUNK, _RET_CHUNK), _F32),
        pltpu.VMEM((nh, dh, dh), _F32),
        pltpu.VMEM((nh, dh, dh), _F32),
    ]
    vmem = (2 * 7 * _nbytes((seq, wide), _BF16) + 4 * _nbytes((seq, wide), _F32)
            + 8 * nh * _nbytes((dh, dh), _F32))
    return pl.pallas_call(
        _mixer_kernel,
        grid=(bsz, heads // nh),
        in_specs=[
            seq_block, seq_block,
            pl.BlockSpec((nh, dh, seq), lambda b, j: (j, 0, b)),
            seq_block, seq_block,
            per_head((dh, dh)),
            per_head((1, dh)),
            pl.BlockSpec(memory_space=pltpu.SMEM),
            pl.BlockSpec(memory_space=pltpu.SMEM),
        ],
        out_specs=[seq_block, seq_block],
        out_shape=[jax.ShapeDtypeStruct((groups, t, wide), _BF16),
                   jax.ShapeDtypeStruct((groups, t, wide), _BF16)],
        scratch_shapes=scratch,
        compiler_params=pltpu.CompilerParams(
            dimension_semantics=("arbitrary", "arbitrary"), vmem_limit_bytes=_vmem_limit(vmem)),
        name="mixer",
    )(*parts, pool_w, pool_scale, dec_f, dec_b)


def _outproj_kernel(p_ref, r_ref, x_ref, mod_ref, gain_ref, w_ref, x1_ref, h2_ref):
    groups, _, wide = p_ref.shape
    gate = mod_ref[0, 2:3, :]
    scale = gain_ref[...] * (1.0 + mod_ref[0, 4:5, :])
    shift = mod_ref[0, 3:4, :]
    for r0 in range(0, x_ref.shape[0], _DOT_ROWS):
        rows = slice(r0, r0 + _DOT_ROWS)
        mixed = None
        for part, src_ref in enumerate((p_ref, r_ref)):
            for j in range(groups):
                k0 = (part * groups + j) * wide
                term = _dot(src_ref[j, rows, :], w_ref[k0:k0 + wide, :])
                mixed = term if mixed is None else mixed + term
        x1_ref[rows, :] = x_ref[rows, :] + gate * mixed
        _rmsnorm_rows(x1_ref, h2_ref, r0, _DOT_ROWS, scale, shift)


def _outproj(pool_out, ret_out, x2d, mod3, gain, w, seq, *, tm=512):
    t, d = x2d.shape
    groups, _, wide = pool_out.shape
    kp = kr = groups * wide
    assert ret_out.shape == pool_out.shape and w.shape[0] == kp + kr
    per_batch = seq // tm
    vmem = (2 * _nbytes((tm, kp + kr), _BF16) + 4 * _nbytes((tm, d), _F32) + 2 * _nbytes((tm, d), _BF16)
            + _nbytes((kp + kr, d), _BF16) + _nbytes((_DOT_ROWS, d), _F32))
    return pl.pallas_call(
        _outproj_kernel,
        grid=(t // tm,),
        in_specs=[
            pl.BlockSpec((groups, tm, wide), lambda i: (0, i, 0)),
            pl.BlockSpec((groups, tm, wide), lambda i: (0, i, 0)),
            pl.BlockSpec((tm, d), lambda i: (i, 0)),
            pl.BlockSpec((1, _N_MOD, d), lambda i: (i // per_batch, 0, 0)),
            pl.BlockSpec((1, d), lambda i: (0, 0)),
            pl.BlockSpec((kp + kr, d), lambda i: (0, 0), pipeline_mode=pl.Buffered(1)),
        ],
        out_specs=[pl.BlockSpec((tm, d), lambda i: (i, 0)),
                   pl.BlockSpec((tm, d), lambda i: (i, 0))],
        out_shape=[jax.ShapeDtypeStruct((t, d), _F32),
                   jax.ShapeDtypeStruct((t, d), _BF16)],
        compiler_params=pltpu.CompilerParams(
            dimension_semantics=("arbitrary",), vmem_limit_bytes=_vmem_limit(vmem)),
        name="outproj",
    )(pool_out, ret_out, x2d, mod3, gain, w)


def _ffn_kernel(h2_ref, wgu_ref, wd_ref, x1_hbm, mod_ref, gain_ref, o_ref, x1_buf, sem, *, final_norm):
    i, f = pl.program_id(0), pl.program_id(1)
    tm, d = o_ref.shape

    def x1_copy():
        return pltpu.make_async_copy(x1_hbm.at[pl.ds(pl.multiple_of(i * tm, tm), tm), :], x1_buf, sem.at[0])

    last = pl.num_programs(1) - 1

    def hidden():
        gu = _dot(h2_ref[...], wgu_ref[...])
        parts = [_silu(gu[:, c0:c0 + _LANES]) * gu[:, c0 + _LANES:c0 + 2 * _LANES]
                 for c0 in range(0, gu.shape[1], 2 * _LANES)]
        return jnp.concatenate(parts, axis=1).astype(_BF16)

    def finish_rows(r0):
        rows = slice(r0, r0 + _NORM_ROWS)
        gate = mod_ref[0, 5:6, :]
        gain = gain_ref[...]
        ss = None
        for cols in _col_blocks(d):
            x2 = x1_buf[rows, cols] + gate[:, cols] * o_ref[rows, cols]
            o_ref[rows, cols] = x2
            ss = x2 * x2 if ss is None else ss + x2 * x2
        if final_norm:
            rs = lax.rsqrt(jnp.sum(ss, axis=-1, keepdims=True) / d + _EPS)
            for cols in _col_blocks(d):
                o_ref[rows, cols] = o_ref[rows, cols] * rs * gain[:, cols]

    @pl.when(f == 0)
    def _():
        x1_copy().start()
        o_ref[...] = _dot(hidden(), wd_ref[...])

    @pl.when((f > 0) & (f < last))
    def _():
        o_ref[...] += _dot(hidden(), wd_ref[...])

    @pl.when(f == last)
    def _():
        x1_copy().wait()
        hid = hidden()
        for r0 in range(0, tm, _DOT_ROWS):
            rows = slice(r0, r0 + _DOT_ROWS)
            o_ref[rows, :] += _dot(hid[rows, :], wd_ref[...])
            for s0 in range(r0, r0 + _DOT_ROWS, _NORM_ROWS):
                finish_rows(s0)


def _ffn(h2, x1, mod3, wgu, wd, final_gain, seq, *, final_norm, tm=1024, tf=512):
    t, d = x1.shape
    dff = wd.shape[0]
    assert wgu.shape == (d, 2 * dff) and tf % _LANES == 0
    assert dff // tf >= 2
    per_batch = seq // tm
    vmem = (2 * _nbytes((tm, d), _BF16) + 6 * _nbytes((d, tf), _BF16) + 3 * _nbytes((tm, d), _F32)
            + 3 * _nbytes((tm, tf), _F32))
    return pl.pallas_call(
        functools.partial(_ffn_kernel, final_norm=final_norm),
        grid=(t // tm, dff // tf),
        in_specs=[
            pl.BlockSpec((tm, d), lambda i, f: (i, 0)),
            pl.BlockSpec((d, 2 * tf), lambda i, f: (0, f)),
            pl.BlockSpec((tf, d), lambda i, f: (f, 0)),
            pl.BlockSpec(memory_space=pl.ANY),
            pl.BlockSpec((1, _N_MOD, d), lambda i, f: (i // per_batch, 0, 0)),
            pl.BlockSpec((1, d), lambda i, f: (0, 0)),
        ],
        out_specs=pl.BlockSpec((tm, d), lambda i, f: (i, 0)),
        out_shape=jax.ShapeDtypeStruct((t, d), _F32),
        scratch_shapes=[pltpu.VMEM((tm, d), _F32), pltpu.SemaphoreType.DMA((1,))],
        compiler_params=pltpu.CompilerParams(
            dimension_semantics=("arbitrary", "arbitrary"), vmem_limit_bytes=_vmem_limit(vmem)),
        name="ffn",
    )(h2, wgu, wd, x1, mod3, final_gain)


def kernel(x, c, w_ada, b_ada, norm1_g, w_in, pool_w, pool_scale, ret_decay_fwd, ret_decay_bwd,
           w_out, norm2_g, w_gate, w_up, w_down, final_g):
    bsz, seq, d = x.shape
    depth = w_ada.shape[0]
    heads = _RET_HEADS
    pool_width = pool_scale.shape[1]
    dh = (w_in.shape[2] - pool_width) // (4 * heads)
    assert pool_width == len(_POOL_WINDOWS) * dh and seq % _RET_CHUNK == 0 and dh == _RET_CHUNK
    assert all(w & (w - 1) == 0 and w // 2 <= _POOL_EDGE_ROWS for w in _POOL_WINDOWS)

    inv = 1.0 / (_ROPE_BASE ** np.linspace(0.0, 1.0, dh // 2))
    ang = np.arange(seq)[:, None] * inv[None, :]
    cos, sin = jnp.asarray(np.cos(ang), _F32), jnp.asarray(np.sin(ang), _F32)

    xf = x.reshape(bsz * seq, d)
    for l in range(depth):
        mod3 = _adaln(c, w_ada[l], b_ada[l]).reshape(bsz, _N_MOD, d)
        *parts, wo, wd, wgu = _inproj(xf, mod3, norm1_g[l].reshape(1, d), cos, sin, w_in[l].astype(_BF16), seq,
                                      pool_width, [w_out[l], w_down[l]], [w_gate[l], w_up[l]])
        pool_out, ret_out = _mixer(
            parts, pool_w[l].astype(_BF16), pool_scale[l].reshape(len(_POOL_WINDOWS), 1, dh),
            ret_decay_fwd[l].astype(_F32), ret_decay_bwd[l].astype(_F32), bsz, seq)
        x1, h2 = _outproj(pool_out, ret_out, xf, mod3, norm2_g[l].reshape(1, d), wo, seq)
        xf = _ffn(h2, x1, mod3, wgu, wd, final_g.reshape(1, d), seq, final_norm=(l == depth - 1))
    return xf.reshape(bsz, seq, d)
```

```python
import functools

import jax
import jax.numpy as jnp
import numpy as np
from jax import lax
from jax.experimental import pallas as pl
from jax.experimental.pallas import tpu as pltpu

_EPS = 1e-6
_POOL_WINDOWS = (2, 4, 8, 16)
_RET_HEADS = 4
_ROPE_BASE = 10000.0
_N_MOD = 6

_V7X_VMEM_BYTES = 64 * 1024 * 1024
_VMEM_UNCOUNTED_BYTES = 8 * 1024 * 1024
_VMEM_RESERVED_BYTES = 2 * 1024 * 1024
_LANES = 128
_BF16_ROWS = 16
_RET_CHUNK = 256
_POOL_HALO = 16
_POOL_SRC_ROWS = 256
_POOL_EDGE_ROWS = 8
_NORM_ROWS = 128
_DOT_ROWS = 256
_MIXER_HEADS_PER_STEP = 2

_F32 = jnp.float32
_BF16 = jnp.bfloat16


def _vmem_limit(nbytes):
    return int(min(_V7X_VMEM_BYTES - _VMEM_RESERVED_BYTES, nbytes + _VMEM_UNCOUNTED_BYTES))


def _nbytes(shape, dtype):
    n = 1
    for s in shape:
        n *= s
    return n * jnp.dtype(dtype).itemsize


def _dot(a, b):
    return jnp.dot(a, b, preferred_element_type=_F32)


def _silu(x):
    return x * jax.nn.sigmoid(x)


def _rms_rows(x):
    return x * lax.rsqrt(jnp.mean(x * x, axis=-1, keepdims=True) + _EPS)


def _col_blocks(d):
    return [slice(c0, c0 + _LANES) for c0 in range(0, d, _LANES)]


def _rmsnorm_rows(src_ref, dst_ref, row0, nrows, gain, shift=None):
    d = src_ref.shape[1]
    for r0 in range(row0, row0 + nrows, _NORM_ROWS):
        rows = slice(r0, r0 + _NORM_ROWS)
        ss = None
        for cols in _col_blocks(d):
            xc = src_ref[rows, cols]
            ss = xc * xc if ss is None else ss + xc * xc
        rs = lax.rsqrt(jnp.sum(ss, axis=-1, keepdims=True) / d + _EPS)
        for cols in _col_blocks(d):
            y = src_ref[rows, cols] * rs * gain[:, cols]
            if shift is not None:
                y = y + shift[:, cols]
            dst_ref[rows, cols] = y.astype(dst_ref.dtype)


def _adaln_kernel(c_ref, w_ref, b_ref, cast_in_ref, o_ref, cast_out_ref):
    c = c_ref[...]
    o_ref[...] = _dot(_silu(c).astype(_BF16), w_ref[...].astype(_BF16)) + b_ref[...]
    _cast_slabs([cast_in_ref], [cast_out_ref])


def _adaln(c, w, b, cast_weight, *, tn=768):
    bsz, d = c.shape
    n = w.shape[1]
    steps = n // tn
    cast_specs, cast_shapes, cast_bytes = _slab_specs([cast_weight], steps, lambda j: (j, 0))
    vmem = (2 * _nbytes((d, tn), _F32) + _nbytes((d, tn), _BF16) + 4 * _nbytes((bsz, d), _F32)
            + cast_bytes)
    return pl.pallas_call(
        _adaln_kernel,
        grid=(steps,),
        in_specs=[
            pl.BlockSpec((bsz, d), lambda j: (0, 0)),
            pl.BlockSpec((d, tn), lambda j: (0, j)),
            pl.BlockSpec((1, tn), lambda j: (0, j)),
        ] + cast_specs,
        out_specs=[pl.BlockSpec((bsz, tn), lambda j: (0, j))] + cast_specs,
        out_shape=[jax.ShapeDtypeStruct((bsz, n), _F32)] + cast_shapes,
        compiler_params=pltpu.CompilerParams(
            dimension_semantics=("arbitrary",), vmem_limit_bytes=_vmem_limit(vmem)),
        name="adaln",
    )(c, w, b.reshape(1, n), cast_weight)


def _slab_specs(weights, steps, index_map):
    specs, shapes, nbytes = [], [], 0
    for w in weights:
        rows, cols = w.shape
        assert rows % steps == 0 and (rows // steps) % _BF16_ROWS == 0
        specs.append(pl.BlockSpec((rows // steps, cols), index_map))
        shapes.append(jax.ShapeDtypeStruct(w.shape, _BF16))
        nbytes += 2 * _nbytes((rows // steps, cols), _F32) + 2 * _nbytes((rows // steps, cols), _BF16)
    return specs, shapes, nbytes


def _cast_slabs(src_refs, dst_refs):
    for src, dst in zip(src_refs, dst_refs):
        dst[...] = src[...].astype(dst.dtype)


def _cast_gate_up(wg_ref, wu_ref, wgu_ref):
    for g in range(wg_ref.shape[1] // _LANES):
        cols = slice(g * _LANES, (g + 1) * _LANES)
        wgu_ref[:, 2 * g * _LANES:(2 * g + 1) * _LANES] = wg_ref[:, cols].astype(wgu_ref.dtype)
        wgu_ref[:, (2 * g + 1) * _LANES:(2 * g + 2) * _LANES] = wu_ref[:, cols].astype(wgu_ref.dtype)


def _inproj_kernel(x_ref, mod_ref, gain_ref, cos_ref, sin_ref, w_ref, *rest, n_cast, pool_width, heads):
    cast_in, (wg_ref, wu_ref) = rest[:n_cast], rest[n_cast:n_cast + 2]
    u_ref, q_ref, k_ref, v_ref, g_ref, *cast_out, wgu_ref, h_ref = rest[n_cast + 2:]
    groups, _, wide = q_ref.shape
    width = groups * wide
    dh = width // heads
    half = dh // 2
    scale = gain_ref[...] * (1.0 + mod_ref[0, 1:2, :])
    shift = mod_ref[0, 0:1, :]

    def store(dst_ref, rows, val):
        for j in range(dst_ref.shape[0]):
            dst_ref[j, rows, :] = val[:, j * wide:(j + 1) * wide].astype(dst_ref.dtype)

    for r0 in range(0, x_ref.shape[0], _DOT_ROWS):
        rows = slice(r0, r0 + _DOT_ROWS)
        _rmsnorm_rows(x_ref, h_ref, r0, _DOT_ROWS, scale, shift)
        h = h_ref[rows, :]
        c0 = pool_width + 3 * width
        store(g_ref, rows, _silu(_dot(h, w_ref[:, c0:c0 + width])))
        cos, sin = cos_ref[rows, :], sin_ref[rows, :]
        for part in range(2):
            c0 = pool_width + part * width
            t = _dot(h, w_ref[:, c0:c0 + width])
            for hd in range(heads):
                t1 = t[:, hd * dh:hd * dh + half]
                t2 = t[:, hd * dh + half:(hd + 1) * dh]
                r1, r2 = t1 * cos - t2 * sin, t1 * sin + t2 * cos
                if part == 0:
                    j, c1 = (hd * dh) // wide, (hd * dh) % wide
                    q_ref[j, rows, c1:c1 + half] = r1.astype(q_ref.dtype)
                    q_ref[j, rows, c1 + half:c1 + dh] = r2.astype(q_ref.dtype)
                else:
                    k_ref[hd, 0:half, rows] = r1.T.astype(k_ref.dtype)
                    k_ref[hd, half:dh, rows] = r2.T.astype(k_ref.dtype)
        if r0 == 0:
            _cast_slabs(cast_in, cast_out)
            _cast_gate_up(wg_ref, wu_ref, wgu_ref)
        c0 = pool_width + 2 * width
        store(v_ref, rows, _dot(h, w_ref[:, c0:c0 + width]))
        store(u_ref, rows, _dot(h, w_ref[:, 0:pool_width]))


def _inproj(x2d, mod3, gain, cos, sin, w, seq, pool_width, cast_weights, gate_up, *, tm=512):
    t, d = x2d.shape
    n = w.shape[1]
    per_batch = seq // tm
    steps = t // tm
    wide = _MIXER_HEADS_PER_STEP * (n - pool_width) // (4 * _RET_HEADS)
    groups = _RET_HEADS // _MIXER_HEADS_PER_STEP
    assert pool_width == groups * wide
    part_spec = pl.BlockSpec((groups, tm, wide), lambda i: (0, i, 0))
    part_shape = jax.ShapeDtypeStruct((groups, t, wide), _BF16)
    cast_specs, cast_shapes, cast_bytes = _slab_specs(cast_weights, steps, lambda i: (i, 0))
    gu_specs, _, gu_bytes = _slab_specs(gate_up, steps, lambda i: (i, 0))
    gu_rows, dff = gate_up[0].shape
    assert gate_up[1].shape == (gu_rows, dff) and dff % _LANES == 0
    vmem = (2 * _nbytes((tm, d), _F32) + _nbytes((d, n), _BF16) + 2 * _nbytes((tm, n), _BF16)
            + _nbytes((tm, d), _BF16) + _nbytes((_DOT_ROWS, n), _F32) + 4 * _nbytes(cos.shape, _F32) // per_batch
            + cast_bytes + gu_bytes)
    return pl.pallas_call(
        functools.partial(_inproj_kernel, n_cast=len(cast_weights), pool_width=pool_width, heads=_RET_HEADS),
        grid=(t // tm,),
        in_specs=[
            pl.BlockSpec((tm, d), lambda i: (i, 0)),
            pl.BlockSpec((1, _N_MOD, d), lambda i: (i // per_batch, 0, 0)),
            pl.BlockSpec((1, d), lambda i: (0, 0)),
            pl.BlockSpec((tm, cos.shape[1]), lambda i: (i % per_batch, 0)),
            pl.BlockSpec((tm, sin.shape[1]), lambda i: (i % per_batch, 0)),
            pl.BlockSpec((d, n), lambda i: (0, 0), pipeline_mode=pl.Buffered(1)),
        ] + cast_specs + gu_specs,
        out_specs=([part_spec, part_spec, pl.BlockSpec((_RET_HEADS, wide // _MIXER_HEADS_PER_STEP, tm),
                                                       lambda i: (0, 0, i)), part_spec, part_spec] + cast_specs
                   + [pl.BlockSpec((gu_rows // steps, 2 * dff), lambda i: (i, 0))]),
        out_shape=([part_shape, part_shape,
                    jax.ShapeDtypeStruct((_RET_HEADS, wide // _MIXER_HEADS_PER_STEP, t), _BF16),
                    part_shape, part_shape] + cast_shapes
                   + [jax.ShapeDtypeStruct((gu_rows, 2 * dff), _BF16)]),
        scratch_shapes=[pltpu.VMEM((tm, d), _BF16)],
        compiler_params=pltpu.CompilerParams(
            dimension_semantics=("arbitrary",), vmem_limit_bytes=_vmem_limit(vmem)),
        name="inproj",
    )(x2d, mod3, gain, cos, sin, w, *cast_weights, *gate_up)


def _pool_tiles(seq):
    step = _POOL_SRC_ROWS - 2 * _POOL_HALO
    return [(s, min(step, seq - s)) for s in range(0, seq, step)]


def _pool_pad_rows(seq):
    return _pool_tiles(seq)[-1][0] + _POOL_SRC_ROWS


def _pool_chunks(u_ref, pad_ref, pooled_ref, cols, group, seq):
    tiles = _pool_tiles(seq)
    halo, src_rows = _POOL_HALO, _POOL_SRC_ROWS
    out_rows = src_rows - 2 * halo
    edge = _POOL_EDGE_ROWS
    assert tiles[-1][1] >= edge
    half = jnp.int32(0)
    inv_w = jnp.float32(0)
    for gi, window in enumerate(_POOL_WINDOWS):
        half = jnp.where(group == gi, jnp.int32(window // 2), half)
        inv_w = jnp.where(group == gi, jnp.float32(1.0 / window), inv_w)

    width_c = cols.stop - cols.start
    pad_ref[0:halo, cols] = jnp.zeros((halo, width_c), pad_ref.dtype)
    pad_ref[halo + seq:, cols] = jnp.zeros((pad_ref.shape[0] - halo - seq, width_c), pad_ref.dtype)
    pad_ref[halo:halo + seq, cols] = u_ref[:, cols]

    rel = (lax.broadcasted_iota(jnp.int32, (out_rows, src_rows), 1) - halo
           - lax.broadcasted_iota(jnp.int32, (out_rows, src_rows), 0))
    band = (jnp.where((rel >= -half) & (rel < half), inv_w, 0.0)
            - jnp.where(rel == 0, 1.0, 0.0)).astype(_BF16)

    t_lo = lax.broadcasted_iota(jnp.int32, (edge, width_c), 0)
    t_hi = t_lo + (seq - edge)
    width = (2 * half).astype(_F32)
    ratio_lo = width / (t_lo + half - jnp.maximum(t_lo - half, 0)).astype(_F32)
    ratio_hi = width / (jnp.minimum(t_hi + half, seq) - (t_hi - half)).astype(_F32)

    def pool_tile(ti):
        s, rows = tiles[ti]
        pooled = _dot(band, pad_ref[s:s + src_rows, cols])
        if ti == 0:
            u_lo = u_ref[0:edge, cols].astype(_F32)
            pooled = jnp.concatenate([(pooled[0:edge] + u_lo) * ratio_lo - u_lo, pooled[edge:]], axis=0)
        if ti == len(tiles) - 1:
            lo = seq - edge - s
            u_hi = u_ref[seq - edge:seq, cols].astype(_F32)
            pooled = jnp.concatenate(
                [pooled[:lo], (pooled[lo:lo + edge] + u_hi) * ratio_hi - u_hi, pooled[lo + edge:]], axis=0)
        pooled_ref[s:s + rows, cols] = pooled[:rows].astype(pooled_ref.dtype)

    return pool_tile


def _mixer_kernel(u_ref, q_ref, kt_ref, v_ref, gate_ref, pw_ref, ps_ref, df_ref, db_ref, pool_ref, ret_ref,
                  pad_ref, pooled_ref, kb_ref, acc_ref, dec_ref, sf_ref, sb_ref):
    u_ref, q_ref, v_ref, gate_ref, pool_ref, ret_ref = (
        r.at[0] for r in (u_ref, q_ref, v_ref, gate_ref, pool_ref, ret_ref))
    seq = q_ref.shape[0]
    nh, dh = pw_ref.shape[0], pw_ref.shape[-1]
    c = _RET_CHUNK
    n = seq // c
    head_cols = [slice(hd * dh, (hd + 1) * dh) for hd in range(nh)]

    ri = lax.broadcasted_iota(jnp.int32, (c, c), 0).astype(_F32)
    cj = lax.broadcasted_iota(jnp.int32, (c, c), 1).astype(_F32)
    pos = lax.broadcasted_iota(jnp.int32, (1, c), 1).astype(_F32)
    diff = ri - cj
    qk_scale = dh ** -0.5
    pool_tile, chunk_f, chunk_b, key_f, key_b = [], [], [], [], []
    n_pool_tiles = len(_pool_tiles(seq))
    for hd, cols in enumerate(head_cols):
        pool_tile.append(_pool_chunks(u_ref, pad_ref, pooled_ref, cols, pl.program_id(1) * nh + hd, seq))
        head = pl.program_id(1) * nh + hd
        lg_f = -jnp.exp(jnp.full((1, c), df_ref[head], _F32))
        lg_b = -jnp.exp(jnp.full((1, c), db_ref[head], _F32))
        dec_ref[hd, 0] = jnp.exp(jnp.where(diff >= 0, diff * lg_f, -diff * lg_b)) * qk_scale
        dec_ref[hd, 1] = jnp.exp((ri + 1.0) * lg_f) * qk_scale
        dec_ref[hd, 2] = jnp.exp((c - ri) * lg_b) * qk_scale
        key_f.append(jnp.exp((c - 1.0 - pos) * lg_f))
        key_b.append(jnp.exp(pos * lg_b))
        chunk_f.append(jnp.exp(c * lg_f))
        chunk_b.append(jnp.exp(c * lg_b))

    for ci in range(n):
        rows = slice(ci * c, (ci + 1) * c)
        for hd, cols in enumerate(head_cols):
            q = q_ref[rows, cols]
            kt = kt_ref[hd, :, rows]
            kt32 = kt.astype(_F32)
            if ci > 0:
                kb_ref[hd, :, rows] = (kt32 * key_b[hd]).astype(_BF16)
            v = v_ref[rows, cols]
            scores = _dot(q, kt) * dec_ref[hd, 0]
            out = _dot(scores.astype(_BF16), v)
            if ci > 0:
                out = out + _dot(q, sf_ref[hd].astype(_BF16)) * dec_ref[hd, 1]
            acc_ref[rows, cols] = out
            if ci < n - 1:
                upd = _dot((kt32 * key_f[hd]).astype(_BF16), v)
                sf_ref[hd] = upd if ci == 0 else sf_ref[hd] * chunk_f[hd] + upd
            for ti in range(ci, n_pool_tiles, n):
                pool_tile[hd](ti)

    for ci in reversed(range(n)):
        rows = slice(ci * c, (ci + 1) * c)
        for hd, cols in enumerate(head_cols):
            pool_ref[rows, cols] = (_dot(pooled_ref[rows, cols], pw_ref[hd]) * ps_ref[hd]).astype(pool_ref.dtype)
            out = acc_ref[rows, cols]
            if ci < n - 1:
                out = out + _dot(q_ref[rows, cols], sb_ref[hd].astype(_BF16)) * dec_ref[hd, 2]
            ret = _rms_rows(out)
            ret_ref[rows, cols] = (gate_ref[rows, cols].astype(_F32) * ret).astype(ret_ref.dtype)
            if ci > 0:
                upd = _dot(kb_ref[hd, :, rows], v_ref[rows, cols])
                sb_ref[hd] = upd if ci == n - 1 else sb_ref[hd] * chunk_b[hd] + upd


def _mixer(parts, pool_w, pool_scale, dec_f, dec_b, bsz, seq):
    groups, t, wide = parts[0].shape
    heads = _RET_HEADS
    dh = pool_w.shape[-1]
    nh = _MIXER_HEADS_PER_STEP
    assert heads == groups * nh and len(_POOL_WINDOWS) == heads and wide == nh * dh
    seq_block = pl.BlockSpec((1, seq, wide), lambda b, j: (j, b, 0))
    per_head = lambda shape: pl.BlockSpec((nh,) + shape, lambda b, j: (j, 0, 0))
    scratch = [
        pltpu.VMEM((_pool_pad_rows(seq), wide), _BF16),
        pltpu.VMEM((seq, wide), _BF16),
        pltpu.VMEM((nh, dh, seq), _BF16),
        pltpu.VMEM((seq, wide), _F32),
        pltpu.VMEM((nh, 3, _RET_CHUNK, _RET_CHUNK), _F32),
        pltpu.VMEM((nh, dh, dh), _F32),
        pltpu.VMEM((nh, dh, dh), _F32),
    ]
    vmem = (2 * 7 * _nbytes((seq, wide), _BF16) + 4 * _nbytes((seq, wide), _F32)
            + 8 * nh * _nbytes((dh, dh), _F32))
    return pl.pallas_call(
        _mixer_kernel,
        grid=(bsz, heads // nh),
        in_specs=[
            seq_block, seq_block,
            pl.BlockSpec((nh, dh, seq), lambda b, j: (j, 0, b)),
            seq_block, seq_block,
            per_head((dh, dh)),
            per_head((1, dh)),
            pl.BlockSpec(memory_space=pltpu.SMEM),
            pl.BlockSpec(memory_space=pltpu.SMEM),
        ],
        out_specs=[seq_block, seq_block],
        out_shape=[jax.ShapeDtypeStruct((groups, t, wide), _BF16),
                   jax.ShapeDtypeStruct((groups, t, wide), _BF16)],
        scratch_shapes=scratch,
        compiler_params=pltpu.CompilerParams(
            dimension_semantics=("arbitrary", "arbitrary"), vmem_limit_bytes=_vmem_limit(vmem)),
        name="mixer",
    )(*parts, pool_w, pool_scale, dec_f, dec_b)


def _outproj_kernel(p_ref, r_ref, x_ref, mod_ref, gain_ref, w_ref, x1_ref, h2_ref):
    groups, _, wide = p_ref.shape
    gate = mod_ref[0, 2:3, :]
    scale = gain_ref[...] * (1.0 + mod_ref[0, 4:5, :])
    shift = mod_ref[0, 3:4, :]
    for r0 in range(0, x_ref.shape[0], _DOT_ROWS):
        rows = slice(r0, r0 + _DOT_ROWS)
        mixed = None
        for part, src_ref in enumerate((p_ref, r_ref)):
            for j in range(groups):
                k0 = (part * groups + j) * wide
                term = _dot(src_ref[j, rows, :], w_ref[k0:k0 + wide, :])
                mixed = term if mixed is None else mixed + term
        x1_ref[rows, :] = x_ref[rows, :] + gate * mixed
        _rmsnorm_rows(x1_ref, h2_ref, r0, _DOT_ROWS, scale, shift)


def _outproj(pool_out, ret_out, x2d, mod3, gain, w, seq, *, tm=512):
    t, d = x2d.shape
    groups, _, wide = pool_out.shape
    kp = kr = groups * wide
    assert ret_out.shape == pool_out.shape and w.shape[0] == kp + kr
    per_batch = seq // tm
    vmem = (2 * _nbytes((tm, kp + kr), _BF16) + 4 * _nbytes((tm, d), _F32) + 2 * _nbytes((tm, d), _BF16)
            + _nbytes((kp + kr, d), _BF16) + _nbytes((_DOT_ROWS, d), _F32))
    return pl.pallas_call(
        _outproj_kernel,
        grid=(t // tm,),
        in_specs=[
            pl.BlockSpec((groups, tm, wide), lambda i: (0, i, 0)),
            pl.BlockSpec((groups, tm, wide), lambda i: (0, i, 0)),
            pl.BlockSpec((tm, d), lambda i: (i, 0)),
            pl.BlockSpec((1, _N_MOD, d), lambda i: (i // per_batch, 0, 0)),
            pl.BlockSpec((1, d), lambda i: (0, 0)),
            pl.BlockSpec((kp + kr, d), lambda i: (0, 0), pipeline_mode=pl.Buffered(1)),
        ],
        out_specs=[pl.BlockSpec((tm, d), lambda i: (i, 0)),
                   pl.BlockSpec((tm, d), lambda i: (i, 0))],
        out_shape=[jax.ShapeDtypeStruct((t, d), _F32),
                   jax.ShapeDtypeStruct((t, d), _BF16)],
        compiler_params=pltpu.CompilerParams(
            dimension_semantics=("arbitrary",), vmem_limit_bytes=_vmem_limit(vmem)),
        name="outproj",
    )(pool_out, ret_out, x2d, mod3, gain, w)


def _ffn_kernel(h2_ref, wgu_ref, wd_ref, x1_hbm, mod_ref, gain_ref, o_ref, x1_buf, sem, *, final_norm):
    i, f = pl.program_id(0), pl.program_id(1)
    tm, d = o_ref.shape

    def x1_copy():
        return pltpu.make_async_copy(x1_hbm.at[pl.ds(pl.multiple_of(i * tm, tm), tm), :], x1_buf, sem.at[0])

    last = pl.num_programs(1) - 1

    def hidden():
        gu = _dot(h2_ref[...], wgu_ref[...])
        parts = [_silu(gu[:, c0:c0 + _LANES]) * gu[:, c0 + _LANES:c0 + 2 * _LANES]
                 for c0 in range(0, gu.shape[1], 2 * _LANES)]
        return jnp.concatenate(parts, axis=1).astype(_BF16)

    def finish_rows(r0):
        rows = slice(r0, r0 + _NORM_ROWS)
        gate = mod_ref[0, 5:6, :]
        gain = gain_ref[...]
        ss = None
        for cols in _col_blocks(d):
            x2 = x1_buf[rows, cols] + gate[:, cols] * o_ref[rows, cols]
            o_ref[rows, cols] = x2
            ss = x2 * x2 if ss is None else ss + x2 * x2
        if final_norm:
            rs = lax.rsqrt(jnp.sum(ss, axis=-1, keepdims=True) / d + _EPS)
            for cols in _col_blocks(d):
                o_ref[rows, cols] = o_ref[rows, cols] * rs * gain[:, cols]

    @pl.when(f == 0)
    def _():
        x1_copy().start()
        o_ref[...] = _dot(hidden(), wd_ref[...])

    @pl.when((f > 0) & (f < last))
    def _():
        o_ref[...] += _dot(hidden(), wd_ref[...])

    @pl.when(f == last)
    def _():
        x1_copy().wait()
        hid = hidden()
        for r0 in range(0, tm, _DOT_ROWS):
            rows = slice(r0, r0 + _DOT_ROWS)
            o_ref[rows, :] += _dot(hid[rows, :], wd_ref[...])
            for s0 in range(r0, r0 + _DOT_ROWS, _NORM_ROWS):
                finish_rows(s0)


def _ffn(h2, x1, mod3, wgu, wd, final_gain, seq, *, final_norm, tm=1024, tf=512):
    t, d = x1.shape
    dff = wd.shape[0]
    assert wgu.shape == (d, 2 * dff) and tf % _LANES == 0
    assert dff // tf >= 2
    per_batch = seq // tm
    vmem = (2 * _nbytes((tm, d), _BF16) + 6 * _nbytes((d, tf), _BF16) + 3 * _nbytes((tm, d), _F32)
            + 3 * _nbytes((tm, tf), _F32))
    return pl.pallas_call(
        functools.partial(_ffn_kernel, final_norm=final_norm),
        grid=(t // tm, dff // tf),
        in_specs=[
            pl.BlockSpec((tm, d), lambda i, f: (i, 0)),
            pl.BlockSpec((d, 2 * tf), lambda i, f: (0, f)),
            pl.BlockSpec((tf, d), lambda i, f: (f, 0)),
            pl.BlockSpec(memory_space=pl.ANY),
            pl.BlockSpec((1, _N_MOD, d), lambda i, f: (i // per_batch, 0, 0)),
            pl.BlockSpec((1, d), lambda i, f: (0, 0)),
        ],
        out_specs=pl.BlockSpec((tm, d), lambda i, f: (i, 0)),
        out_shape=jax.ShapeDtypeStruct((t, d), _F32),
        scratch_shapes=[pltpu.VMEM((tm, d), _F32), pltpu.SemaphoreType.DMA((1,))],
        compiler_params=pltpu.CompilerParams(
            dimension_semantics=("arbitrary", "arbitrary"), vmem_limit_bytes=_vmem_limit(vmem)),
        name="ffn",
    )(h2, wgu, wd, x1, mod3, final_gain)


def kernel(x, c, w_ada, b_ada, norm1_g, w_in, pool_w, pool_scale, ret_decay_fwd, ret_decay_bwd,
           w_out, norm2_g, w_gate, w_up, w_down, final_g):
    bsz, seq, d = x.shape
    depth = w_ada.shape[0]
    heads = _RET_HEADS
    pool_width = pool_scale.shape[1]
    dh = (w_in.shape[2] - pool_width) // (4 * heads)
    assert pool_width == len(_POOL_WINDOWS) * dh and seq % _RET_CHUNK == 0 and dh == _RET_CHUNK
    assert all(w & (w - 1) == 0 and w // 2 <= _POOL_EDGE_ROWS for w in _POOL_WINDOWS)

    inv = 1.0 / (_ROPE_BASE ** np.linspace(0.0, 1.0, dh // 2))
    ang = np.arange(seq)[:, None] * inv[None, :]
    cos, sin = jnp.asarray(np.cos(ang), _F32), jnp.asarray(np.sin(ang), _F32)

    xf = x.reshape(bsz * seq, d)
    for l in range(depth):
        mod, wi = _adaln(c, w_ada[l], b_ada[l], w_in[l])
        mod3 = mod.reshape(bsz, _N_MOD, d)
        *parts, wo, wd, wgu = _inproj(xf, mod3, norm1_g[l].reshape(1, d), cos, sin, wi, seq,
                                      pool_width, [w_out[l], w_down[l]], [w_gate[l], w_up[l]])
        pool_out, ret_out = _mixer(
            parts, pool_w[l].astype(_BF16), pool_scale[l].reshape(len(_POOL_WINDOWS), 1, dh),
            ret_decay_fwd[l].astype(_F32), ret_decay_bwd[l].astype(_F32), bsz, seq)
        x1, h2 = _outproj(pool_out, ret_out, xf, mod3, norm2_g[l].reshape(1, d), wo, seq)
        xf = _ffn(h2, x1, mod3, wgu, wd, final_g.reshape(1, d), seq, final_norm=(l == depth - 1))
    return xf.reshape(bsz, seq, d)
```

```python
import functools

import jax
import jax.numpy as jnp
import numpy as np
from jax import lax
from jax.experimental import pallas as pl
from jax.experimental.pallas import tpu as pltpu

_EPS = 1e-6
_POOL_WINDOWS = (2, 4, 8, 16)
_RET_HEADS = 4
_ROPE_BASE = 10000.0
_N_MOD = 6

_V7X_VMEM_BYTES = 64 * 1024 * 1024
_VMEM_UNCOUNTED_BYTES = 8 * 1024 * 1024
_VMEM_RESERVED_BYTES = 2 * 1024 * 1024
_LANES = 128
_BF16_ROWS = 16
_RET_CHUNK = 256
_POOL_HALO = 16
_POOL_SRC_ROWS = 256
_POOL_EDGE_ROWS = 8
_NORM_ROWS = 128
_DOT_ROWS = 256
_MIXER_HEADS_PER_STEP = 2

_F32 = jnp.float32
_BF16 = jnp.bfloat16


def _vmem_limit(nbytes):
    return int(min(_V7X_VMEM_BYTES - _VMEM_RESERVED_BYTES, nbytes + _VMEM_UNCOUNTED_BYTES))


def _nbytes(shape, dtype):
    n = 1
    for s in shape:
        n *= s
    return n * jnp.dtype(dtype).itemsize


def _dot(a, b):
    return jnp.dot(a, b, preferred_element_type=_F32)


def _silu(x):
    return x * jax.nn.sigmoid(x)


def _rms_rows(x):
    return x * lax.rsqrt(jnp.mean(x * x, axis=-1, keepdims=True) + _EPS)


def _col_blocks(d):
    return [slice(c0, c0 + _LANES) for c0 in range(0, d, _LANES)]


def _rmsnorm_rows(src_ref, dst_ref, row0, nrows, gain, shift=None):
    d = src_ref.shape[1]
    for r0 in range(row0, row0 + nrows, _NORM_ROWS):
        rows = slice(r0, r0 + _NORM_ROWS)
        ss = None
        for cols in _col_blocks(d):
            xc = src_ref[rows, cols]
            ss = xc * xc if ss is None else ss + xc * xc
        rs = lax.rsqrt(jnp.sum(ss, axis=-1, keepdims=True) / d + _EPS)
        for cols in _col_blocks(d):
            y = src_ref[rows, cols] * rs * gain[:, cols]
            if shift is not None:
                y = y + shift[:, cols]
            dst_ref[rows, cols] = y.astype(dst_ref.dtype)


def _adaln_kernel(c_ref, w_ref, b_ref, o_ref):
    c = c_ref[...]
    o_ref[...] = _dot(_silu(c).astype(_BF16), w_ref[...].astype(_BF16)) + b_ref[...]


def _adaln(c, w, b, *, tn=1024):
    bsz, d = c.shape
    n = w.shape[1]
    vmem = 2 * _nbytes((d, tn), _F32) + _nbytes((d, tn), _BF16) + 4 * _nbytes((bsz, d), _F32)
    return pl.pallas_call(
        _adaln_kernel,
        grid=(n // tn,),
        in_specs=[
            pl.BlockSpec((bsz, d), lambda j: (0, 0)),
            pl.BlockSpec((d, tn), lambda j: (0, j)),
            pl.BlockSpec((1, tn), lambda j: (0, j)),
        ],
        out_specs=pl.BlockSpec((bsz, tn), lambda j: (0, j)),
        out_shape=jax.ShapeDtypeStruct((bsz, n), _F32),
        compiler_params=pltpu.CompilerParams(
            dimension_semantics=("arbitrary",), vmem_limit_bytes=_vmem_limit(vmem)),
        name="adaln",
    )(c, w, b.reshape(1, n))


def _slab_specs(weights, steps, index_map):
    specs, shapes, nbytes = [], [], 0
    for w in weights:
        rows, cols = w.shape
        assert rows % steps == 0 and (rows // steps) % _BF16_ROWS == 0
        specs.append(pl.BlockSpec((rows // steps, cols), index_map))
        shapes.append(jax.ShapeDtypeStruct(w.shape, _BF16))
        nbytes += 2 * _nbytes((rows // steps, cols), _F32) + 2 * _nbytes((rows // steps, cols), _BF16)
    return specs, shapes, nbytes


def _cast_slabs(src_refs, dst_refs):
    for src, dst in zip(src_refs, dst_refs):
        dst[...] = src[...].astype(dst.dtype)


def _cast_gate_up(wg_ref, wu_ref, wgu_ref):
    for g in range(wg_ref.shape[1] // _LANES):
        cols = slice(g * _LANES, (g + 1) * _LANES)
        wgu_ref[:, 2 * g * _LANES:(2 * g + 1) * _LANES] = wg_ref[:, cols].astype(wgu_ref.dtype)
        wgu_ref[:, (2 * g + 1) * _LANES:(2 * g + 2) * _LANES] = wu_ref[:, cols].astype(wgu_ref.dtype)


def _inproj_kernel(x_ref, mod_ref, gain_ref, cos_ref, sin_ref, w_ref, *rest, n_cast, pool_width, heads):
    cast_in, (wg_ref, wu_ref) = rest[:n_cast], rest[n_cast:n_cast + 2]
    u_ref, q_ref, k_ref, v_ref, g_ref, *cast_out, wgu_ref, h_ref = rest[n_cast + 2:]
    groups, _, wide = q_ref.shape
    width = groups * wide
    dh = width // heads
    half = dh // 2
    scale = gain_ref[...] * (1.0 + mod_ref[0, 1:2, :])
    shift = mod_ref[0, 0:1, :]

    def store(dst_ref, rows, val):
        for j in range(dst_ref.shape[0]):
            dst_ref[j, rows, :] = val[:, j * wide:(j + 1) * wide].astype(dst_ref.dtype)

    for r0 in range(0, x_ref.shape[0], _DOT_ROWS):
        rows = slice(r0, r0 + _DOT_ROWS)
        _rmsnorm_rows(x_ref, h_ref, r0, _DOT_ROWS, scale, shift)
        h = h_ref[rows, :]
        c0 = pool_width + 3 * width
        store(g_ref, rows, _silu(_dot(h, w_ref[:, c0:c0 + width])))
        cos, sin = cos_ref[rows, :], sin_ref[rows, :]
        for part in range(2):
            c0 = pool_width + part * width
            t = _dot(h, w_ref[:, c0:c0 + width])
            for hd in range(heads):
                t1 = t[:, hd * dh:hd * dh + half]
                t2 = t[:, hd * dh + half:(hd + 1) * dh]
                r1, r2 = t1 * cos - t2 * sin, t1 * sin + t2 * cos
                if part == 0:
                    j, c1 = (hd * dh) // wide, (hd * dh) % wide
                    q_ref[j, rows, c1:c1 + half] = r1.astype(q_ref.dtype)
                    q_ref[j, rows, c1 + half:c1 + dh] = r2.astype(q_ref.dtype)
                else:
                    k_ref[hd, 0:half, rows] = r1.T.astype(k_ref.dtype)
                    k_ref[hd, half:dh, rows] = r2.T.astype(k_ref.dtype)
        if r0 == 0:
            _cast_slabs(cast_in, cast_out)
            _cast_gate_up(wg_ref, wu_ref, wgu_ref)
        c0 = pool_width + 2 * width
        store(v_ref, rows, _dot(h, w_ref[:, c0:c0 + width]))
        store(u_ref, rows, _dot(h, w_ref[:, 0:pool_width]))


def _inproj(x2d, mod3, gain, cos, sin, w, seq, pool_width, cast_weights, gate_up, *, tm=512):
    t, d = x2d.shape
    n = w.shape[1]
    per_batch = seq // tm
    steps = t // tm
    wide = _MIXER_HEADS_PER_STEP * (n - pool_width) // (4 * _RET_HEADS)
    groups = _RET_HEADS // _MIXER_HEADS_PER_STEP
    assert pool_width == groups * wide
    part_spec = pl.BlockSpec((groups, tm, wide), lambda i: (0, i, 0))
    part_shape = jax.ShapeDtypeStruct((groups, t, wide), _BF16)
    cast_specs, cast_shapes, cast_bytes = _slab_specs(cast_weights, steps, lambda i: (i, 0))
    gu_specs, _, gu_bytes = _slab_specs(gate_up, steps, lambda i: (i, 0))
    gu_rows, dff = gate_up[0].shape
    assert gate_up[1].shape == (gu_rows, dff) and dff % _LANES == 0
    vmem = (2 * _nbytes((tm, d), _F32) + _nbytes((d, n), _BF16) + 2 * _nbytes((tm, n), _BF16)
            + _nbytes((tm, d), _BF16) + _nbytes((_DOT_ROWS, n), _F32) + 4 * _nbytes(cos.shape, _F32) // per_batch
            + cast_bytes + gu_bytes)
    return pl.pallas_call(
        functools.partial(_inproj_kernel, n_cast=len(cast_weights), pool_width=pool_width, heads=_RET_HEADS),
        grid=(t // tm,),
        in_specs=[
            pl.BlockSpec((tm, d), lambda i: (i, 0)),
            pl.BlockSpec((1, _N_MOD, d), lambda i: (i // per_batch, 0, 0)),
            pl.BlockSpec((1, d), lambda i: (0, 0)),
            pl.BlockSpec((tm, cos.shape[1]), lambda i: (i % per_batch, 0)),
            pl.BlockSpec((tm, sin.shape[1]), lambda i: (i % per_batch, 0)),
            pl.BlockSpec((d, n), lambda i: (0, 0), pipeline_mode=pl.Buffered(1)),
        ] + cast_specs + gu_specs,
        out_specs=([part_spec, part_spec, pl.BlockSpec((_RET_HEADS, wide // _MIXER_HEADS_PER_STEP, tm),
                                                       lambda i: (0, 0, i)), part_spec, part_spec] + cast_specs
                   + [pl.BlockSpec((gu_rows // steps, 2 * dff), lambda i: (i, 0))]),
        out_shape=([part_shape, part_shape,
                    jax.ShapeDtypeStruct((_RET_HEADS, wide // _MIXER_HEADS_PER_STEP, t), _BF16),
                    part_shape, part_shape] + cast_shapes
                   + [jax.ShapeDtypeStruct((gu_rows, 2 * dff), _BF16)]),
        scratch_shapes=[pltpu.VMEM((tm, d), _BF16)],
        compiler_params=pltpu.CompilerParams(
            dimension_semantics=("arbitrary",), vmem_limit_bytes=_vmem_limit(vmem)),
        name="inproj",
    )(x2d, mod3, gain, cos, sin, w, *cast_weights, *gate_up)


def _pool_tiles(seq):
    step = _POOL_SRC_ROWS - 2 * _POOL_HALO
    return [(s, min(step, seq - s)) for s in range(0, seq, step)]


def _pool_pad_rows(seq):
    return _pool_tiles(seq)[-1][0] + _POOL_SRC_ROWS


def _pool_chunks(u_ref, pad_ref, pooled_ref, cols, group, seq):
    tiles = _pool_tiles(seq)
    halo, src_rows = _POOL_HALO, _POOL_SRC_ROWS
    out_rows = src_rows - 2 * halo
    edge = _POOL_EDGE_ROWS
    assert tiles[-1][1] >= edge
    half = jnp.int32(0)
    inv_w = jnp.float32(0)
    for gi, window in enumerate(_POOL_WINDOWS):
        half = jnp.where(group == gi, jnp.int32(window // 2), half)
        inv_w = jnp.where(group == gi, jnp.float32(1.0 / window), inv_w)

    width_c = cols.stop - cols.start
    pad_ref[0:halo, cols] = jnp.zeros((halo, width_c), pad_ref.dtype)
    pad_ref[halo + seq:, cols] = jnp.zeros((pad_ref.shape[0] - halo - seq, width_c), pad_ref.dtype)
    pad_ref[halo:halo + seq, cols] = u_ref[:, cols]

    rel = (lax.broadcasted_iota(jnp.int32, (out_rows, src_rows), 1) - halo
           - lax.broadcasted_iota(jnp.int32, (out_rows, src_rows), 0))
    band = (jnp.where((rel >= -half) & (rel < half), inv_w, 0.0)
            - jnp.where(rel == 0, 1.0, 0.0)).astype(_BF16)

    t_lo = lax.broadcasted_iota(jnp.int32, (edge, width_c), 0)
    t_hi = t_lo + (seq - edge)
    width = (2 * half).astype(_F32)
    ratio_lo = width / (t_lo + half - jnp.maximum(t_lo - half, 0)).astype(_F32)
    ratio_hi = width / (jnp.minimum(t_hi + half, seq) - (t_hi - half)).astype(_F32)

    def pool_tile(ti):
        s, rows = tiles[ti]
        pooled = _dot(band, pad_ref[s:s + src_rows, cols])
        if ti == 0:
            u_lo = u_ref[0:edge, cols].astype(_F32)
            pooled = jnp.concatenate([(pooled[0:edge] + u_lo) * ratio_lo - u_lo, pooled[edge:]], axis=0)
        if ti == len(tiles) - 1:
            lo = seq - edge - s
            u_hi = u_ref[seq - edge:seq, cols].astype(_F32)
            pooled = jnp.concatenate(
                [pooled[:lo], (pooled[lo:lo + edge] + u_hi) * ratio_hi - u_hi, pooled[lo + edge:]], axis=0)
        pooled_ref[s:s + rows, cols] = pooled[:rows].astype(pooled_ref.dtype)

    return pool_tile


def _mixer_kernel(u_ref, q_ref, kt_ref, v_ref, gate_ref, pw_ref, ps_ref, df_ref, db_ref, pool_ref, ret_ref,
                  pad_ref, pooled_ref, kb_ref, acc_ref, dec_ref, sf_ref, sb_ref):
    u_ref, q_ref, v_ref, gate_ref, pool_ref, ret_ref = (
        r.at[0] for r in (u_ref, q_ref, v_ref, gate_ref, pool_ref, ret_ref))
    seq = q_ref.shape[0]
    nh, dh = pw_ref.shape[0], pw_ref.shape[-1]
    c = _RET_CHUNK
    n = seq // c
    head_cols = [slice(hd * dh, (hd + 1) * dh) for hd in range(nh)]

    ri = lax.broadcasted_iota(jnp.int32, (c, c), 0).astype(_F32)
    cj = lax.broadcasted_iota(jnp.int32, (c, c), 1).astype(_F32)
    pos = lax.broadcasted_iota(jnp.int32, (1, c), 1).astype(_F32)
    diff = ri - cj
    qk_scale = dh ** -0.5
    pool_tile, chunk_f, chunk_b, key_f, key_b = [], [], [], [], []
    n_pool_tiles = len(_pool_tiles(seq))
    for hd, cols in enumerate(head_cols):
        pool_tile.append(_pool_chunks(u_ref, pad_ref, pooled_ref, cols, pl.program_id(1) * nh + hd, seq))
        head = pl.program_id(1) * nh + hd
        lg_f = -jnp.exp(jnp.full((1, c), df_ref[head], _F32))
        lg_b = -jnp.exp(jnp.full((1, c), db_ref[head], _F32))
        dec_ref[hd, 0] = jnp.exp(jnp.where(diff >= 0, diff * lg_f, -diff * lg_b)) * qk_scale
        dec_ref[hd, 1] = jnp.exp((ri + 1.0) * lg_f) * qk_scale
        dec_ref[hd, 2] = jnp.exp((c - ri) * lg_b) * qk_scale
        key_f.append(jnp.exp((c - 1.0 - pos) * lg_f))
        key_b.append(jnp.exp(pos * lg_b))
        chunk_f.append(jnp.exp(c * lg_f))
        chunk_b.append(jnp.exp(c * lg_b))

    for ci in range(n):
        rows = slice(ci * c, (ci + 1) * c)
        for hd, cols in enumerate(head_cols):
            q = q_ref[rows, cols]
            kt = kt_ref[hd, :, rows]
            kt32 = kt.astype(_F32)
            if ci > 0:
                kb_ref[hd, :, rows] = (kt32 * key_b[hd]).astype(_BF16)
            v = v_ref[rows, cols]
            scores = _dot(q, kt) * dec_ref[hd, 0]
            out = _dot(scores.astype(_BF16), v)
            if ci > 0:
                out = out + _dot(q, sf_ref[hd].astype(_BF16)) * dec_ref[hd, 1]
            acc_ref[rows, cols] = out
            if ci < n - 1:
                upd = _dot((kt32 * key_f[hd]).astype(_BF16), v)
                sf_ref[hd] = upd if ci == 0 else sf_ref[hd] * chunk_f[hd] + upd
            for ti in range(ci, n_pool_tiles, n):
                pool_tile[hd](ti)

    for ci in reversed(range(n)):
        rows = slice(ci * c, (ci + 1) * c)
        for hd, cols in enumerate(head_cols):
            pool_ref[rows, cols] = (_dot(pooled_ref[rows, cols], pw_ref[hd]) * ps_ref[hd]).astype(pool_ref.dtype)
            out = acc_ref[rows, cols]
            if ci < n - 1:
                out = out + _dot(q_ref[rows, cols], sb_ref[hd].astype(_BF16)) * dec_ref[hd, 2]
            ret = _rms_rows(out)
            ret_ref[rows, cols] = (gate_ref[rows, cols].astype(_F32) * ret).astype(ret_ref.dtype)
            if ci > 0:
                upd = _dot(kb_ref[hd, :, rows], v_ref[rows, cols])
                sb_ref[hd] = upd if ci == n - 1 else sb_ref[hd] * chunk_b[hd] + upd


def _mixer(parts, pool_w, pool_scale, dec_f, dec_b, bsz, seq):
    groups, t, wide = parts[0].shape
    heads = _RET_HEADS
    dh = pool_w.shape[-1]
    nh = _MIXER_HEADS_PER_STEP
    assert heads == groups * nh and len(_POOL_WINDOWS) == heads and wide == nh * dh
    seq_block = pl.BlockSpec((1, seq, wide), lambda b, j: (j, b, 0))
    per_head = lambda shape: pl.BlockSpec((nh,) + shape, lambda b, j: (j, 0, 0))
    scratch = [
        pltpu.VMEM((_pool_pad_rows(seq), wide), _BF16),
        pltpu.VMEM((seq, wide), _BF16),
        pltpu.VMEM((nh, dh, seq), _BF16),
        pltpu.VMEM((seq, wide), _F32),
        pltpu.VMEM((nh, 3, _RET_CHUNK, _RET_CHUNK), _F32),
        pltpu.VMEM((nh, dh, dh), _F32),
        pltpu.VMEM((nh, dh, dh), _F32),
    ]
    vmem = (2 * 7 * _nbytes((seq, wide), _BF16) + 4 * _nbytes((seq, wide), _F32)
            + 8 * nh * _nbytes((dh, dh), _F32))
    return pl.pallas_call(
        _mixer_kernel,
        grid=(bsz, heads // nh),
        in_specs=[
            seq_block, seq_block,
            pl.BlockSpec((nh, dh, seq), lambda b, j: (j, 0, b)),
            seq_block, seq_block,
            per_head((dh, dh)),
            per_head((1, dh)),
            pl.BlockSpec(memory_space=pltpu.SMEM),
            pl.BlockSpec(memory_space=pltpu.SMEM),
        ],
        out_specs=[seq_block, seq_block],
        out_shape=[jax.ShapeDtypeStruct((groups, t, wide), _BF16),
                   jax.ShapeDtypeStruct((groups, t, wide), _BF16)],
        scratch_shapes=scratch,
        compiler_params=pltpu.CompilerParams(
            dimension_semantics=("arbitrary", "arbitrary"), vmem_limit_bytes=_vmem_limit(vmem)),
        name="mixer",
    )(*parts, pool_w, pool_scale, dec_f, dec_b)


def _outproj_kernel(p_ref, r_ref, x_ref, mod_ref, gain_ref, w_ref, x1_ref, h2_ref):
    groups, _, wide = p_ref.shape
    gate = mod_ref[0, 2:3, :]
    scale = gain_ref[...] * (1.0 + mod_ref[0, 4:5, :])
    shift = mod_ref[0, 3:4, :]
    for r0 in range(0, x_ref.shape[0], _DOT_ROWS):
        rows = slice(r0, r0 + _DOT_ROWS)
        mixed = None
        for part, src_ref in enumerate((p_ref, r_ref)):
            for j in range(groups):
                k0 = (part * groups + j) * wide
                term = _dot(src_ref[j, rows, :], w_ref[k0:k0 + wide, :])
                mixed = term if mixed is None else mixed + term
        x1_ref[rows, :] = x_ref[rows, :] + gate * mixed
        _rmsnorm_rows(x1_ref, h2_ref, r0, _DOT_ROWS, scale, shift)


def _outproj(pool_out, ret_out, x2d, mod3, gain, w, seq, *, tm=512):
    t, d = x2d.shape
    groups, _, wide = pool_out.shape
    kp = kr = groups * wide
    assert ret_out.shape == pool_out.shape and w.shape[0] == kp + kr
    per_batch = seq // tm
    vmem = (2 * _nbytes((tm, kp + kr), _BF16) + 4 * _nbytes((tm, d), _F32) + 2 * _nbytes((tm, d), _BF16)
            + _nbytes((kp + kr, d), _BF16) + _nbytes((_DOT_ROWS, d), _F32))
    return pl.pallas_call(
        _outproj_kernel,
        grid=(t // tm,),
        in_specs=[
            pl.BlockSpec((groups, tm, wide), lambda i: (0, i, 0)),
            pl.BlockSpec((groups, tm, wide), lambda i: (0, i, 0)),
            pl.BlockSpec((tm, d), lambda i: (i, 0)),
            pl.BlockSpec((1, _N_MOD, d), lambda i: (i // per_batch, 0, 0)),
            pl.BlockSpec((1, d), lambda i: (0, 0)),
            pl.BlockSpec((kp + kr, d), lambda i: (0, 0), pipeline_mode=pl.Buffered(1)),
        ],
        out_specs=[pl.BlockSpec((tm, d), lambda i: (i, 0)),
                   pl.BlockSpec((tm, d), lambda i: (i, 0))],
        out_shape=[jax.ShapeDtypeStruct((t, d), _F32),
                   jax.ShapeDtypeStruct((t, d), _BF16)],
        compiler_params=pltpu.CompilerParams(
            dimension_semantics=("arbitrary",), vmem_limit_bytes=_vmem_limit(vmem)),
        name="outproj",
    )(pool_out, ret_out, x2d, mod3, gain, w)


def _ffn_kernel(h2_ref, wgu_ref, wd_ref, x1_hbm, mod_ref, gain_ref, o_ref, x1_buf, sem, *, final_norm):
    i, f = pl.program_id(0), pl.program_id(1)
    tm, d = o_ref.shape

    def x1_copy():
        return pltpu.make_async_copy(x1_hbm.at[pl.ds(pl.multiple_of(i * tm, tm), tm), :], x1_buf, sem.at[0])

    last = pl.num_programs(1) - 1

    def hidden():
        gu = _dot(h2_ref[...], wgu_ref[...])
        parts = [_silu(gu[:, c0:c0 + _LANES]) * gu[:, c0 + _LANES:c0 + 2 * _LANES]
                 for c0 in range(0, gu.shape[1], 2 * _LANES)]
        return jnp.concatenate(parts, axis=1).astype(_BF16)

    def finish_rows(r0):
        rows = slice(r0, r0 + _NORM_ROWS)
        gate = mod_ref[0, 5:6, :]
        gain = gain_ref[...]
        ss = None
        for cols in _col_blocks(d):
            x2 = x1_buf[rows, cols] + gate[:, cols] * o_ref[rows, cols]
            o_ref[rows, cols] = x2
            ss = x2 * x2 if ss is None else ss + x2 * x2
        if final_norm:
            rs = lax.rsqrt(jnp.sum(ss, axis=-1, keepdims=True) / d + _EPS)
            for cols in _col_blocks(d):
                o_ref[rows, cols] = o_ref[rows, cols] * rs * gain[:, cols]

    @pl.when(f == 0)
    def _():
        x1_copy().start()
        o_ref[...] = _dot(hidden(), wd_ref[...])

    @pl.when((f > 0) & (f < last))
    def _():
        o_ref[...] += _dot(hidden(), wd_ref[...])

    @pl.when(f == last)
    def _():
        x1_copy().wait()
        hid = hidden()
        for r0 in range(0, tm, _DOT_ROWS):
            rows = slice(r0, r0 + _DOT_ROWS)
            o_ref[rows, :] += _dot(hid[rows, :], wd_ref[...])
            for s0 in range(r0, r0 + _DOT_ROWS, _NORM_ROWS):
                finish_rows(s0)


def _ffn(h2, x1, mod3, wgu, wd, final_gain, seq, *, final_norm, tm=1024, tf=512):
    t, d = x1.shape
    dff = wd.shape[0]
    assert wgu.shape == (d, 2 * dff) and tf % _LANES == 0
    assert dff // tf >= 2
    per_batch = seq // tm
    vmem = (2 * _nbytes((tm, d), _BF16) + 6 * _nbytes((d, tf), _BF16) + 3 * _nbytes((tm, d), _F32)
            + 3 * _nbytes((tm, tf), _F32))
    return pl.pallas_call(
        functools.partial(_ffn_kernel, final_norm=final_norm),
        grid=(t // tm, dff // tf),
        in_specs=[
            pl.BlockSpec((tm, d), lambda i, f: (i, 0)),
            pl.BlockSpec((d, 2 * tf), lambda i, f: (0, f)),
            pl.BlockSpec((tf, d), lambda i, f: (f, 0)),
            pl.BlockSpec(memory_space=pl.ANY),
            pl.BlockSpec((1, _N_MOD, d), lambda i, f: (i // per_batch, 0, 0)),
            pl.BlockSpec((1, d), lambda i, f: (0, 0)),
        ],
        out_specs=pl.BlockSpec((tm, d), lambda i, f: (i, 0)),
        out_shape=jax.ShapeDtypeStruct((t, d), _F32),
        scratch_shapes=[pltpu.VMEM((tm, d), _F32), pltpu.SemaphoreType.DMA((1,))],
        compiler_params=pltpu.CompilerParams(
            dimension_semantics=("arbitrary", "arbitrary"), vmem_limit_bytes=_vmem_limit(vmem)),
        name="ffn",
    )(h2, wgu, wd, x1, mod3, final_gain)


def kernel(x, c, w_ada, b_ada, norm1_g, w_in, pool_w, pool_scale, ret_decay_fwd, ret_decay_bwd,
           w_out, norm2_g, w_gate, w_up, w_down, final_g):
    bsz, seq, d = x.shape
    depth = w_ada.shape[0]
    heads = _RET_HEADS
    pool_width = pool_scale.shape[1]
    dh = (w_in.shape[2] - pool_width) // (4 * heads)
    assert pool_width == len(_POOL_WINDOWS) * dh and seq % _RET_CHUNK == 0 and dh == _RET_CHUNK
    assert all(w & (w - 1) == 0 and w // 2 <= _POOL_EDGE_ROWS for w in _POOL_WINDOWS)

    inv = 1.0 / (_ROPE_BASE ** np.linspace(0.0, 1.0, dh // 2))
    ang = np.arange(seq)[:, None] * inv[None, :]
    cos, sin = jnp.asarray(np.cos(ang), _F32), jnp.asarray(np.sin(ang), _F32)

    xf = x.reshape(bsz * seq, d)
    for l in range(depth):
        mod3 = _adaln(c, w_ada[l], b_ada[l]).reshape(bsz, _N_MOD, d)
        *parts, wo, wd, wgu = _inproj(xf, mod3, norm1_g[l].reshape(1, d), cos, sin, w_in[l].astype(_BF16), seq,
                                      pool_width, [w_out[l], w_down[l]], [w_gate[l], w_up[l]])
        pool_out, ret_out = _mixer(
            parts, pool_w[l].astype(_BF16), pool_scale[l].reshape(len(_POOL_WINDOWS), 1, dh),
            ret_decay_fwd[l].astype(_F32), ret_decay_bwd[l].astype(_F32), bsz, seq)
        x1, h2 = _outproj(pool_out, ret_out, xf, mod3, norm2_g[l].reshape(1, d), wo, seq)
        xf = _ffn(h2, x1, mod3, wgu, wd, final_g.reshape(1, d), seq, final_norm=(l == depth - 1))
    return xf.reshape(bsz, seq, d)
```
